```python
import math
import jax, jax.numpy as jnp
from jax import lax
import numpy as np

D_MODEL = 1024
BATCH = 1
SEQ = 16384
DEPTH = 2

HEAD_DIM = 64
A_HEADS = 4
A_KEY_DIM = 64
A_VAL_DIM = 64
A_CHUNK = 16
B_CHANNELS = 256
B_KERNEL = 31
C_HEADS = 4
C_PATTERNS = ((128, 1), (512, 4), (2048, 16))
C_BLOCK = 64
D_HEADS = 4
D_KV_HEADS = 2
D_HALF_WINDOW = 128
ROPE_THETA = 500000.0
ROPE_DIM = HEAD_DIM // 4
FFN_DIM = 2816
N_EXPERTS = 8
TOP_K = 2
EXPERT_DIM = 3584
MOE_BLOCK = 128
A_WIDTH = A_HEADS * A_VAL_DIM
B_WIDTH = B_CHANNELS
C_WIDTH = C_HEADS * HEAD_DIM
D_WIDTH = D_HEADS * HEAD_DIM
MIX_WIDTH = A_WIDTH + B_WIDTH + C_WIDTH + D_WIDTH
IN_SIZES = (A_HEADS * A_KEY_DIM, A_HEADS * A_KEY_DIM, A_HEADS * A_KEY_DIM, A_WIDTH, A_WIDTH,
            B_CHANNELS, B_CHANNELS,
            C_WIDTH, C_WIDTH, C_WIDTH,
            D_WIDTH, D_KV_HEADS * HEAD_DIM, D_KV_HEADS * HEAD_DIM)
IN_WIDTH = sum(IN_SIZES)
DEEPNORM_ALPHA = (2 * DEPTH) ** 0.25
DEEPNORM_BETA = (8 * DEPTH) ** -0.25
LN_EPS = 1e-5
RMS_EPS = 1e-6
NEG_INF = -1e30

kernel_name = "hybrid_parallel_groups_deepnorm_encoder"


def layer_norm(x, w, b):
    xf = x.astype(jnp.float32)
    mu = jnp.mean(xf, -1, keepdims=True)
    var = jnp.mean(jnp.square(xf - mu), -1, keepdims=True)
    return ((xf - mu) * lax.rsqrt(var + LN_EPS) * w.astype(jnp.float32) + b.astype(jnp.float32)).astype(x.dtype)


def rms_norm(x, w):
    xf = x.astype(jnp.float32)
    return xf * lax.rsqrt(jnp.mean(xf * xf, -1, keepdims=True) + RMS_EPS) * w.astype(jnp.float32)


def split_heads(t, n_heads):
    b, s, w = t.shape
    return t.reshape(b, s, n_heads, w // n_heads).transpose(0, 2, 1, 3)


def merge_heads(t):
    b, h, s, d = t.shape
    return t.transpose(0, 2, 1, 3).reshape(b, s, h * d)


def partial_rotary(x, positions):
    half = ROPE_DIM // 2
    inv_freq = ROPE_THETA ** (-jnp.arange(half, dtype=jnp.float32) * (2.0 / ROPE_DIM))
    ang = positions.astype(jnp.float32)[:, None, :, None] * inv_freq
    cos, sin = jnp.cos(ang), jnp.sin(ang)
    xr = x[..., :ROPE_DIM].astype(jnp.float32)
    x1, x2 = xr[..., :half], xr[..., half:]
    rot = jnp.concatenate([x1 * cos - x2 * sin, x2 * cos + x1 * sin], -1).astype(x.dtype)
    return jnp.concatenate([rot, x[..., ROPE_DIM:]], -1)


def banded_attention(q, k, v, half_window, block):
    n, g, length, hd = q.shape
    nb = -(-length // block)
    lp = nb * block
    pad = lp - length
    qb = jnp.pad(q, ((0, 0), (0, 0), (0, pad), (0, 0))).reshape(n, g, nb, block, hd)
    kp = jnp.pad(k, ((0, 0), (block, pad + block), (0, 0)))
    vp = jnp.pad(v, ((0, 0), (block, pad + block), (0, 0)))

    def windows(t):
        return jnp.concatenate([t[:, j * block: j * block + lp].reshape(n, nb, block, hd) for j in range(3)], axis=2)

    kw, vw = windows(kp), windows(vp)
    s = jnp.einsum('ngiqd,nikd->ngiqk', qb, kw, preferred_element_type=jnp.float32) * (hd ** -0.5)
    qpos = jnp.arange(lp).reshape(nb, block)
    kpos = (jnp.arange(nb)[:, None] - 1) * block + jnp.arange(3 * block)[None, :]
    rel = kpos[:, None, :] - qpos[:, :, None]
    mask = (jnp.abs(rel) <= half_window) & (kpos[:, None, :] >= 0) & (kpos[:, None, :] < length)
    s = jnp.where(mask, s, NEG_INF)
    m = jnp.max(s, -1)
    p = jnp.exp(s - m[..., None])
    l = jnp.sum(p, -1)
    o = jnp.einsum('ngiqk,nikd->ngiqd', p, vw.astype(jnp.float32))
    return (o.reshape(n, g, lp, hd)[:, :, :length],
            m.reshape(n, g, lp)[:, :, :length],
            l.reshape(n, g, lp)[:, :, :length])


def hgrn2_direction(q, k, v, logf):
    b_, h_, s_, dk = q.shape
    dv = v.shape[-1]
    n = s_ // A_CHUNK
    q, k, v, logf = [t.reshape(b_, h_, n, A_CHUNK, t.shape[-1]) for t in (q, k, v, logf)]
    cum = jnp.cumsum(logf, axis=3)
    lower = jnp.tril(jnp.ones((A_CHUNK, A_CHUNK), bool))
    diff = cum[:, :, :, :, None, :] - cum[:, :, :, None, :, :]
    decay = jnp.where(lower[:, :, None], jnp.exp(jnp.minimum(diff, 0.0)), 0.0)
    scores = jnp.einsum('bhntk,bhnsk,bhntsk->bhnts', q, k, decay)
    o_intra = jnp.einsum('bhnts,bhnsv->bhntv', scores, v)
    last = cum[:, :, :, -1:, :]
    chunk_kv = jnp.einsum('bhnsk,bhnsv->bhnkv', k * jnp.exp(last - cum), v)
    chunk_decay = jnp.exp(last[:, :, :, 0, :])

    def step(state, inp):
        dec, kv = inp
        return dec[..., None] * state + kv, state

    _, states = lax.scan(step, jnp.zeros((b_, h_, dk, dv), jnp.float32),
                         (jnp.moveaxis(chunk_decay, 2, 0), jnp.moveaxis(chunk_kv, 2, 0)))
    o_inter = jnp.einsum('bhntk,nbhkv->bhntv', q * jnp.exp(cum), states)
    return (o_intra + o_inter).reshape(b_, h_, s_, dv)


def hgrn2_mixer(q_raw, ffwd_raw, fbwd_raw, i_raw, g_raw, lb, norm_w):
    f32 = jnp.float32
    q = jax.nn.silu(split_heads(q_raw, A_HEADS).astype(f32))
    v = split_heads(i_raw, A_HEADS).astype(f32)
    lb = lb.astype(f32).reshape(A_HEADS, 1, A_KEY_DIM)

    def gates(z_raw):
        z = split_heads(z_raw, A_HEADS).astype(f32)
        logf = jnp.logaddexp(jnp.log(lb), jnp.log1p(-lb) + jax.nn.log_sigmoid(z))
        return logf, (1.0 - lb) * jax.nn.sigmoid(-z)

    logf_f, k_f = gates(ffwd_raw)
    logf_b, k_b = gates(fbwd_raw)
    flip = lambda t: jnp.flip(t, axis=2)
    o = hgrn2_direction(q, k_f, v, logf_f) + flip(hgrn2_direction(flip(q), flip(k_b), flip(v), flip(logf_b)))
    o = rms_norm(o, norm_w) * jax.nn.silu(split_heads(g_raw, A_HEADS).astype(f32))
    return merge_heads(o)


def conv_module(val, gate, conv_w, conv_b, norm_w, norm_b):
    u = val * jax.nn.sigmoid(gate)
    pad = B_KERNEL // 2
    y = lax.conv_general_dilated(u, conv_w[:, None, :].astype(u.dtype), window_strides=(1,),
                                 padding=((pad, pad),), dimension_numbers=('NWC', 'WIO', 'NWC'),
                                 feature_group_count=B_CHANNELS) + conv_b
    return jax.nn.silu(layer_norm(y, norm_w, norm_b))


def stride_split(t, dil):
    b, h, s, d = t.shape
    return t.reshape(b, h, s // dil, dil, d).transpose(0, 1, 3, 2, 4).reshape(b * h * dil, s // dil, d)


def stride_merge(t, b, h, dil):
    rest = t.shape[2:]
    t = t.reshape((b, h, dil) + t.shape[1:])
    t = jnp.swapaxes(t, 2, 3)
    return t.reshape((b, h, t.shape[2] * dil) + rest)


def dilated_attention(q, k, v):
    b, h, s, hd = q.shape
    nums, maxes, dens = [], [], []
    for window, dil in C_PATTERNS:
        o, m, l = banded_attention(stride_split(q, dil)[:, None], stride_split(k, dil), stride_split(v, dil),
                                   window // (2 * dil), C_BLOCK)
        nums.append(stride_merge(o[:, 0], b, h, dil))
        maxes.append(stride_merge(m[:, 0], b, h, dil))
        dens.append(stride_merge(l[:, 0], b, h, dil))
    m_all = jnp.stack(maxes)
    w = jnp.exp(m_all - jnp.max(m_all, 0))
    den = jnp.sum(w * jnp.stack(dens), 0)
    num = jnp.sum(w[..., None] * jnp.stack(nums), 0)
    return num / den[..., None]


def window_gqa_with_sink(q, k, v, sink):
    b, hq, s, hd = q.shape
    hkv = k.shape[1]
    g = hq // hkv
    o, m, l = banded_attention(q.reshape(b * hkv, g, s, hd), k.reshape(b * hkv, s, hd),
                               v.reshape(b * hkv, s, hd), D_HALF_WINDOW, D_HALF_WINDOW)
    o = o.reshape(b, hq, s, hd)
    m = m.reshape(b, hq, s)
    l = l.reshape(b, hq, s)
    sk = sink.astype(jnp.float32)[None, :, None]
    m_tot = jnp.maximum(m, sk)
    w = jnp.exp(m - m_tot)
    den = l * w + jnp.exp(sk - m_tot)
    return o * (w / den)[..., None]


def hybrid_mixer(h, positions, w_in, w_out, lb, a_norm_w, conv_w, conv_b, cn_w, cn_b, sink):
    proj = h @ w_in
    cuts = np.cumsum(IN_SIZES)[:-1].tolist()
    aq, aff, afb, ai, ag, bv, bg, cq, ck, cv, dq, dk, dv = jnp.split(proj, cuts, axis=-1)
    ya = hgrn2_mixer(aq, aff, afb, ai, ag, lb, a_norm_w)
    yb = conv_module(bv, bg, conv_w, conv_b, cn_w, cn_b)
    yc = merge_heads(dilated_attention(partial_rotary(split_heads(cq, C_HEADS), positions),
                                       partial_rotary(split_heads(ck, C_HEADS), positions),
                                       split_heads(cv, C_HEADS)))
    yd = merge_heads(window_gqa_with_sink(partial_rotary(split_heads(dq, D_HEADS), positions),
                                          partial_rotary(split_heads(dk, D_KV_HEADS), positions),
                                          split_heads(dv, D_KV_HEADS), sink))
    y = jnp.concatenate([ya.astype(h.dtype), yb.astype(h.dtype), yc.astype(h.dtype), yd.astype(h.dtype)], -1)
    return y @ w_out


def swiglu(h, w_up, w_down):
    gate, up = jnp.split(h @ w_up, 2, axis=-1)
    return (jax.nn.silu(gate) * up) @ w_down


def moe_swiglu(h, w_router, w_up, w_down):
    b, s, d = h.shape
    t = b * s
    hf = h.reshape(t, d)
    logits = jnp.einsum('td,de->te', hf, w_router, preferred_element_type=jnp.float32)
    top_logit, top_e = lax.top_k(logits, TOP_K)
    top_gate = jax.nn.softmax(top_logit, axis=-1)
    flat_e = top_e.reshape(-1).astype(jnp.int32)
    flat_tok = jnp.repeat(jnp.arange(t, dtype=jnp.int32), TOP_K)
    flat_gate = top_gate.reshape(-1)
    n_assign = t * TOP_K
    n_blocks = -(-n_assign // MOE_BLOCK) + N_EXPERTS
    counts = jnp.zeros((N_EXPERTS,), jnp.int32).at[flat_e].add(1)
    padded = (counts + MOE_BLOCK - 1) // MOE_BLOCK * MOE_BLOCK
    padded_end = jnp.cumsum(padded)
    padded_start = padded_end - padded
    start = jnp.cumsum(counts) - counts
    order = jnp.argsort(flat_e)
    sorted_e = flat_e[order]
    slot = padded_start[sorted_e] + jnp.arange(n_assign, dtype=jnp.int32) - start[sorted_e]
    slot_tok = jnp.full((n_blocks * MOE_BLOCK,), t, jnp.int32).at[slot].set(flat_tok[order])
    slot_gate = jnp.zeros((n_blocks * MOE_BLOCK,), jnp.float32).at[slot].set(flat_gate[order])
    block_e = jnp.minimum(jnp.searchsorted(padded_end, jnp.arange(n_blocks, dtype=jnp.int32) * MOE_BLOCK,
                                           side='right'), N_EXPERTS - 1)
    h_pad = jnp.concatenate([hf, jnp.zeros((1, d), hf.dtype)], 0)

    def expert_block(args):
        tok, e = args
        gate, up = jnp.split(h_pad[tok] @ w_up[e], 2, axis=-1)
        return (jax.nn.silu(gate) * up) @ w_down[e]

    y = lax.map(expert_block, (slot_tok.reshape(n_blocks, MOE_BLOCK), block_e))
    out = jnp.zeros((t + 1, d), jnp.float32).at[slot_tok].add(
        slot_gate[:, None] * y.reshape(-1, d).astype(jnp.float32))
    return out[:t].reshape(b, s, d).astype(h.dtype)


def setup_inputs(seed: int = 0) -> dict:
    key = jax.random.key(seed)
    ks = jax.random.split(key, 24)
    nrm = lambda k, shape, scale: jax.random.normal(k, shape, jnp.float32) * scale
    n_dense = (DEPTH + 1) // 2
    n_moe = DEPTH // 2
    return {
        "x": nrm(ks[0], (BATCH, SEQ, D_MODEL), 1.0),
        "c": nrm(ks[1], (BATCH, D_MODEL), 1.0),
        "positions": jnp.broadcast_to(jnp.arange(SEQ, dtype=jnp.int32), (BATCH, SEQ)),
        "w_ada": nrm(ks[2], (DEPTH, D_MODEL, 6 * D_MODEL), 0.1 * D_MODEL ** -0.5),
        "b_ada": nrm(ks[3], (DEPTH, 6 * D_MODEL), 0.01),
        "w_in": nrm(ks[4], (DEPTH, D_MODEL, IN_WIDTH), D_MODEL ** -0.5),
        "w_out": nrm(ks[5], (DEPTH, MIX_WIDTH, D_MODEL), DEEPNORM_BETA * MIX_WIDTH ** -0.5),
        "a_lower_bound": nrm(ks[6], (DEPTH, A_HEADS * A_KEY_DIM), 0.5),
        "a_norm_w": 1.0 + nrm(ks[7], (DEPTH, A_VAL_DIM), 0.02),
        "b_conv_w": nrm(ks[8], (DEPTH, B_KERNEL, B_CHANNELS), B_KERNEL ** -0.5),
        "b_conv_b": nrm(ks[9], (DEPTH, B_CHANNELS), 0.02),
        "b_norm_w": 1.0 + nrm(ks[10], (DEPTH, B_CHANNELS), 0.02),
        "b_norm_b": nrm(ks[11], (DEPTH, B_CHANNELS), 0.02),
        "d_sink": nrm(ks[12], (DEPTH, D_HEADS), 1.0),
        "ln_w": 1.0 + nrm(ks[13], (DEPTH, 2, D_MODEL), 0.02),
        "ln_b": nrm(ks[14], (DEPTH, 2, D_MODEL), 0.02),
        "ffn_w_up": nrm(ks[15], (n_dense, D_MODEL, 2 * FFN_DIM), D_MODEL ** -0.5),
        "ffn_w_down": nrm(ks[16], (n_dense, FFN_DIM, D_MODEL), DEEPNORM_BETA * FFN_DIM ** -0.5),
        "moe_router": nrm(ks[17], (n_moe, D_MODEL, N_EXPERTS), D_MODEL ** -0.5),
        "moe_w_up": nrm(ks[18], (n_moe, N_EXPERTS, D_MODEL, 2 * EXPERT_DIM), D_MODEL ** -0.5),
        "moe_w_down": nrm(ks[19], (n_moe, N_EXPERTS, EXPERT_DIM, D_MODEL), DEEPNORM_BETA * EXPERT_DIM ** -0.5),
    }


def reference(x, c, positions, w_ada, b_ada, w_in, w_out, a_lower_bound, a_norm_w, b_conv_w, b_conv_b,
              b_norm_w, b_norm_b, d_sink, ln_w, ln_b, ffn_w_up, ffn_w_down, moe_router, moe_w_up, moe_w_down):
    mod = jnp.einsum('bd,lde->lbe', jax.nn.silu(c), w_ada) + b_ada[:, None, :]
    lb_cum = jnp.cumsum(jax.nn.softmax(a_lower_bound.astype(jnp.float32), axis=0), axis=0)
    lb_all = lb_cum - lb_cum[0]
    for layer in range(DEPTH):
        shift1, scale1, gate1, shift2, scale2, gate2 = jnp.split(mod[layer][:, None, :], 6, axis=-1)
        h = x * (1.0 + scale1) + shift1
        y = hybrid_mixer(h, positions, w_in[layer], w_out[layer], lb_all[layer], a_norm_w[layer],
                         b_conv_w[layer], b_conv_b[layer], b_norm_w[layer], b_norm_b[layer], d_sink[layer])
        x = layer_norm(DEEPNORM_ALPHA * x + (1.0 + gate1) * y, ln_w[layer, 0], ln_b[layer, 0])
        h = x * (1.0 + scale2) + shift2
        if layer % 2 == 0:
            f = swiglu(h, ffn_w_up[layer // 2], ffn_w_down[layer // 2])
        else:
            f = moe_swiglu(h, moe_router[layer // 2], moe_w_up[layer // 2], moe_w_down[layer // 2])
        x = layer_norm(DEEPNORM_ALPHA * x + (1.0 + gate2) * f, ln_w[layer, 1], ln_b[layer, 1])
    return x
```

```python
import functools

import numpy as np
import jax
import jax.numpy as jnp
from jax import lax
from jax.experimental import pallas as pl
from jax.experimental.pallas import tpu as pltpu

F32 = jnp.float32
BF16 = jnp.bfloat16

D_MODEL = 1024
DEPTH = 2
HEAD_DIM = 64
A_CHUNK = 16
B_KERNEL = 31
GROUP_W = 256
C_PATTERNS = ((128, 1), (512, 4), (2048, 16))
C_SIDE = 64
C_HALO = 1024
D_HALF_WINDOW = 128
ROPE_THETA = 500000.0
ROPE_DIM = HEAD_DIM // 4
FFN_DIM = 2816
N_EXPERTS = 8
TOP_K = 2
EXPERT_DIM = 3584
IN_WIDTH = 3072
DEEPNORM_ALPHA = (2 * DEPTH) ** 0.25
LN_EPS = 1e-5
RMS_EPS = 1e-6
NEG_INF = -1e30

LANES = 128
VMEM_LIMIT = 56 * 1024 * 1024


def _cparams(n_axes=1, vmem=VMEM_LIMIT):
    return pltpu.CompilerParams(dimension_semantics=("arbitrary",) * n_axes, vmem_limit_bytes=vmem)


def _sigmoid(x):
    return 1.0 / (1.0 + jnp.exp(-x))


def _silu(x):
    return x * _sigmoid(x)


def _split_bf16(x):
    hi = x.astype(BF16)
    lo = (x - hi.astype(F32)).astype(BF16)
    return hi, lo


def _layer_norm(r, w, b):
    mu = jnp.mean(r, axis=-1, keepdims=True)
    d = r - mu
    var = jnp.mean(d * d, axis=-1, keepdims=True)
    return d * lax.rsqrt(var + LN_EPS) * w + b


def _dot_nt(a, b):
    return lax.dot_general(a, b, (((1,), (1,)), ((), ())), preferred_element_type=F32)


def _dot_tn(a, b):
    return lax.dot_general(a, b, (((0,), (0,)), ((), ())), preferred_element_type=F32)


def _dot(a, b):
    return jnp.dot(a, b, preferred_element_type=F32)


def _adaln_kernel(c_ref, w_ref, b_ref, o_ref):
    a_hi, a_lo = _split_bf16(_silu(c_ref[...]))
    w_hi, w_lo = _split_bf16(w_ref[...])
    o_ref[...] = _dot(a_hi, w_hi) + _dot(a_hi, w_lo) + _dot(a_lo, w_hi) + b_ref[...]


def _adaln(c, w_ada, b_ada):
    depth, d, n = w_ada.shape
    tn = 1536
    c8 = jnp.broadcast_to(c.reshape(1, d), (8, d))
    out = pl.pallas_call(
        _adaln_kernel,
        grid=(depth, n // tn),
        in_specs=[pl.BlockSpec((8, d), lambda l, j: (0, 0)),
                  pl.BlockSpec((None, d, tn), lambda l, j: (l, 0, j)),
                  pl.BlockSpec((None, 1, tn), lambda l, j: (l, 0, j))],
        out_specs=pl.BlockSpec((None, 8, tn), lambda l, j: (l, 0, j)),
        out_shape=jax.ShapeDtypeStruct((depth, 8, n), F32),
        compiler_params=_cparams(2),
        name="adaln",
    )(c8, w_ada, b_ada.reshape(depth, 1, n))
    return out[:, 0:1, :]


def _rope_kernel(pos_ref, invf_ref, c_ref, s1_ref, s2_ref):
    ang = pos_ref[...].astype(F32) * invf_ref[...]
    j = lax.broadcasted_iota(jnp.int32, ang.shape, 1) & (HEAD_DIM - 1)
    cosv, sinv = jnp.cos(ang), jnp.sin(ang)
    half = ROPE_DIM // 2
    c_ref[...] = jnp.where(j < ROPE_DIM, cosv, 1.0)
    s1_ref[...] = jnp.where(j < half, -sinv, 0.0)
    s2_ref[...] = jnp.where((j >= half) & (j < ROPE_DIM), sinv, 0.0)


def _rope_tables(positions):
    s = positions.shape[-1]
    ts = 2048
    half = ROPE_DIM // 2
    inv_freq = np.float32(ROPE_THETA) ** (-np.arange(half, dtype=np.float32) * np.float32(2.0 / ROPE_DIM))
    invf = jnp.asarray(inv_freq[(np.arange(LANES) % HEAD_DIM) % half].reshape(1, LANES).astype(np.float32))
    spec = pl.BlockSpec((ts, LANES), lambda i: (i, 0))
    return pl.pallas_call(
        _rope_kernel,
        grid=(s // ts,),
        in_specs=[pl.BlockSpec((ts, 1), lambda i: (i, 0)), pl.BlockSpec((1, LANES), lambda i: (0, 0))],
        out_specs=[spec, spec, spec],
        out_shape=[jax.ShapeDtypeStruct((s, LANES), F32)] * 3,
        compiler_params=_cparams(1),
        name="rope_tables",
    )(positions.reshape(s, 1), invf)


_ROPE_COL_CHUNKS = (1792, 1920, 2048, 2176, 2560, 2688, 2816)


def _inproj_kernel(x_ref, sc_ref, sh_ref, w_ref, rc_ref, rs1_ref, rs2_ref, o_ref):
    h = (x_ref[...] * (1.0 + sc_ref[...]) + sh_ref[...]).astype(BF16)
    half = ROPE_DIM // 2
    for c0 in range(0, IN_WIDTH, GROUP_W):
        acc = _dot(h, w_ref[:, c0:c0 + GROUP_W])
        for k in range(GROUP_W // LANES):
            col = c0 + k * LANES
            part = acc[:, k * LANES:(k + 1) * LANES]
            if col in _ROPE_COL_CHUNKS:
                part = (part * rc_ref[...] + pltpu.roll(part, LANES - half, 1) * rs1_ref[...]
                        + pltpu.roll(part, half, 1) * rs2_ref[...])
            o_ref[:, col:col + LANES] = part


def _inproj(x, scale, shift, w_bf16, rope):
    s, d = x.shape
    tm = 512
    row = lambda i: (i, 0)
    const = lambda i: (0, 0)
    return pl.pallas_call(
        _inproj_kernel,
        grid=(s // tm,),
        in_specs=[pl.BlockSpec((tm, d), row), pl.BlockSpec((1, d), const), pl.BlockSpec((1, d), const),
                  pl.BlockSpec((d, IN_WIDTH), const),
                  pl.BlockSpec((tm, LANES), row), pl.BlockSpec((tm, LANES), row), pl.BlockSpec((tm, LANES), row)],
        out_specs=pl.BlockSpec((tm, IN_WIDTH), row),
        out_shape=jax.ShapeDtypeStruct((s, IN_WIDTH), F32),
        compiler_params=_cparams(1),
        name="inproj",
    )(x, scale, shift, w_bf16, *rope)


def _hgrn_core(aq, z, v, lb, st_ref, kbuf, cbuf, vbuf, qt_ref, kt_ref, vt_ref, dec_ref, oi_ref, rev):
    ts = aq.shape[0]
    nsub = ts // A_CHUNK
    q = _silu(aq)
    ls = jnp.minimum(z, 0.0) - jnp.log1p(jnp.exp(-jnp.abs(z)))
    b = jnp.log1p(-lb) + ls
    loglb = jnp.log(lb)
    logf = jnp.maximum(loglb, b) + jnp.log1p(jnp.exp(-jnp.abs(loglb - b)))
    kk = (1.0 - lb) * _sigmoid(-z)

    r = lax.broadcasted_iota(jnp.int32, (2 * ts, ts), 0)
    c = lax.broadcasted_iota(jnp.int32, (2 * ts, ts), 1)
    rr = jnp.where(r >= ts, r - ts, r)
    same = (rr >> 4) == (c >> 4)
    keep = same & ((r >= ts) | ((c >= rr) if rev else (c <= rr)))
    tmat = jnp.where(keep, 1.0, 0.0).astype(BF16)
    lf_hi, lf_lo = _split_bf16(logf)
    res = _dot(tmat, lf_hi) + _dot(tmat, lf_lo)
    cum, tot = res[:ts], res[ts:]

    zpad = jnp.zeros((A_CHUNK, GROUP_W), F32)
    for buf, val in ((kbuf, kk), (cbuf, cum), (vbuf, v)):
        buf[0:A_CHUNK, :] = zpad
        buf[A_CHUNK + ts:, :] = zpad
        buf[A_CHUNK:A_CHUNK + ts, :] = val

    li = lax.broadcasted_iota(jnp.int32, (GROUP_W, GROUP_W), 0)
    lj = lax.broadcasted_iota(jnp.int32, (GROUP_W, GROUP_W), 1)
    bd = (li >> 6) == (lj >> 6)
    ones_bd = jnp.where(bd, 1.0, 0.0).astype(BF16)
    tmod = lax.broadcasted_iota(jnp.int32, (ts, 1), 0) & (A_CHUNK - 1)

    o = jnp.zeros((ts, GROUP_W), F32)
    for d in range(A_CHUNK):
        off = A_CHUNK + d if rev else A_CHUNK - d
        ks = kbuf[pl.ds(off, ts), :]
        cs = cbuf[pl.ds(off, ts), :]
        vs = vbuf[pl.ds(off, ts), :]
        valid = (tmod <= A_CHUNK - 1 - d) if rev else (tmod >= d)
        xx = jnp.where(valid, q * ks * jnp.exp(cum - cs), 0.0)
        o = o + _dot(xx.astype(BF16), ones_bd) * vs

    qt_ref[...] = (q * jnp.exp(cum)).astype(BF16)
    kt_ref[...] = (kk * jnp.exp(tot - cum)).astype(BF16)
    vt_ref[...] = v.astype(BF16)
    dec_ref[...] = jnp.exp(tot)

    def step(i, carry):
        n = (nsub - 1 - i) if rev else i
        r0 = pl.multiple_of(n * A_CHUNK, A_CHUNK)
        rows = pl.ds(r0, A_CHUNK)
        state = st_ref[...]
        oi_ref[rows, :] = _dot_nt(qt_ref[rows, :], state.astype(BF16))
        kv = _dot_tn(vt_ref[rows, :], kt_ref[rows, :])
        st_ref[...] = state * dec_ref[pl.ds(r0, 1), :] + jnp.where(bd, kv, 0.0)
        return carry

    lax.fori_loop(0, nsub, step, 0)
    return o + oi_ref[...]


def _hgrn_fwd_kernel(q_ref, f_ref, i_ref, lb_ref, o_ref, st_ref, *scratch):
    @pl.when(pl.program_id(0) == 0)
    def _():
        st_ref[...] = jnp.zeros_like(st_ref)

    o_ref[...] = _hgrn_core(q_ref[...], f_ref[...], i_ref[...], lb_ref[...], st_ref, *scratch, rev=False)


def _hgrn_bwd_kernel(q_ref, f_ref, i_ref, g_ref, of_ref, lb_ref, nw_ref, o_ref, st_ref, *scratch):
    @pl.when(pl.program_id(0) == 0)
    def _():
        st_ref[...] = jnp.zeros_like(st_ref)

    o = of_ref[...] + _hgrn_core(q_ref[...], f_ref[...], i_ref[...], lb_ref[...], st_ref, *scratch, rev=True)
    li = lax.broadcasted_iota(jnp.int32, (GROUP_W, GROUP_W), 0)
    lj = lax.broadcasted_iota(jnp.int32, (GROUP_W, GROUP_W), 1)
    ones_bd = jnp.where((li >> 6) == (lj >> 6), 1.0, 0.0).astype(BF16)
    sq_hi, sq_lo = _split_bf16(o * o)
    ms = (_dot(sq_hi, ones_bd) + _dot(sq_lo, ones_bd)) * (1.0 / HEAD_DIM)
    o_ref[...] = o * lax.rsqrt(ms + RMS_EPS) * nw_ref[...] * _silu(g_ref[...])


def _hgrn_scratch(ts):
    pad = ts + 2 * A_CHUNK
    return [pltpu.VMEM((GROUP_W, GROUP_W), F32),
            pltpu.VMEM((pad, GROUP_W), F32), pltpu.VMEM((pad, GROUP_W), F32), pltpu.VMEM((pad, GROUP_W), F32),
            pltpu.VMEM((ts, GROUP_W), BF16), pltpu.VMEM((ts, GROUP_W), BF16), pltpu.VMEM((ts, GROUP_W), BF16),
            pltpu.VMEM((ts, GROUP_W), F32), pltpu.VMEM((ts, GROUP_W), F32)]


def _hgrn2(proj, lb, norm_w):
    s = proj.shape[0]
    ts = 256
    n = s // ts
    col = lambda j: pl.BlockSpec((ts, GROUP_W), lambda i: (i, j))
    rcol = lambda j: pl.BlockSpec((ts, GROUP_W), lambda i: (n - 1 - i, j))
    vec = pl.BlockSpec((1, GROUP_W), lambda i: (0, 0))
    o_f = pl.pallas_call(
        _hgrn_fwd_kernel,
        grid=(n,),
        in_specs=[col(0), col(1), col(3), vec],
        out_specs=pl.BlockSpec((ts, GROUP_W), lambda i: (i, 0)),
        out_shape=jax.ShapeDtypeStruct((s, GROUP_W), F32),
        scratch_shapes=_hgrn_scratch(ts),
        compiler_params=_cparams(1),
        name="hgrn_fwd",
    )(proj, proj, proj, lb)
    return pl.pallas_call(
        _hgrn_bwd_kernel,
        grid=(n,),
        in_specs=[rcol(0), rcol(2), rcol(3), rcol(4), pl.BlockSpec((ts, GROUP_W), lambda i: (n - 1 - i, 0)),
                  vec, vec],
        out_specs=pl.BlockSpec((ts, GROUP_W), lambda i: (n - 1 - i, 0)),
        out_shape=jax.ShapeDtypeStruct((s, GROUP_W), F32),
        scratch_shapes=_hgrn_scratch(ts),
        compiler_params=_cparams(1),
        name="hgrn_bwd",
    )(proj, proj, proj, proj, o_f, lb, norm_w)


B_PAD = 16


def _conv_kernel(v_ref, g_ref, vp_ref, gp_ref, vn_ref, gn_ref, w_ref, b_ref, nw_ref, nb_ref, o_ref, ubuf):
    i = pl.program_id(0)
    ts = v_ref.shape[0]
    glu = lambda a, b: a * _sigmoid(b)
    ubuf[0:B_PAD, :] = jnp.where(i > 0, glu(vp_ref[...], gp_ref[...]), 0.0)
    ubuf[B_PAD:B_PAD + ts, :] = glu(v_ref[...], g_ref[...])
    ubuf[B_PAD + ts:, :] = jnp.where(i < pl.num_programs(0) - 1, glu(vn_ref[...], gn_ref[...]), 0.0)
    acc = jnp.zeros((ts, GROUP_W), F32) + b_ref[...]
    for j in range(B_KERNEL):
        acc = acc + ubuf[pl.ds(B_PAD - B_KERNEL // 2 + j, ts), :] * w_ref[j:j + 1, :]
    o_ref[...] = _silu(_layer_norm(acc, nw_ref[...], nb_ref[...]))


def _conv_module(proj, conv_w, conv_b, norm_w, norm_b):
    s = proj.shape[0]
    ts = 512
    n = s // ts
    per = ts // B_PAD
    last = s // B_PAD - 1
    cur = lambda j: pl.BlockSpec((ts, GROUP_W), lambda i: (i, j))
    prev = lambda j: pl.BlockSpec((B_PAD, GROUP_W), lambda i: (jnp.maximum(i * per - 1, 0), j))
    nxt = lambda j: pl.BlockSpec((B_PAD, GROUP_W), lambda i: (jnp.minimum((i + 1) * per, last), j))
    vec = pl.BlockSpec((1, GROUP_W), lambda i: (0, 0))
    wpad = jnp.pad(conv_w, ((0, 1), (0, 0)))
    return pl.pallas_call(
        _conv_kernel,
        grid=(n,),
        in_specs=[cur(5), cur(6), prev(5), prev(6), nxt(5), nxt(6),
                  pl.BlockSpec((B_KERNEL + 1, GROUP_W), lambda i: (0, 0)), vec, vec, vec],
        out_specs=pl.BlockSpec((ts, GROUP_W), lambda i: (i, 0)),
        out_shape=jax.ShapeDtypeStruct((s, GROUP_W), F32),
        scratch_shapes=[pltpu.VMEM((ts + 2 * B_PAD, GROUP_W), F32)],
        compiler_params=_cparams(1),
        name="conv_module",
    )(proj, proj, proj, proj, proj, proj, wpad, conv_b, norm_w, norm_b)


def _dilated_kernel(q_ref, kp_ref, kc_ref, kn_ref, vp_ref, vc_ref, vn_ref, o_ref, qw, kw, vw, ob, mb, lb, *, seq, span):
    i = pl.program_id(0)

    def put(dst, row0, val):
        n = val.shape[0]
        dst[0, row0:row0 + n, :] = val[:, :LANES]
        dst[1, row0:row0 + n, :] = val[:, LANES:]

    put(qw, 0, q_ref[...] * (HEAD_DIM ** -0.5))
    for dst, refs in ((kw, (kp_ref, kc_ref, kn_ref)), (vw, (vp_ref, vc_ref, vn_ref))):
        put(dst, 0, refs[0][...])
        put(dst, C_HALO, refs[1][...])
        put(dst, C_HALO + span, refs[2][...])

    head = lax.broadcasted_iota(jnp.int32, (1, GROUP_W), 1) >> 6
    heads = [head == h for h in range(4)]

    def rows(start, size, dil):
        return pl.ds(start, size) if dil == 1 else pl.ds(start, size, stride=dil)

    def load(src, sel):
        return jnp.concatenate([src[0, sel, :], src[1, sel, :]], axis=1)

    def store(dst, sel, val):
        dst[0, sel, :] = val[:, :LANES]
        dst[1, sel, :] = val[:, LANES:]

    def by_head(x, nq):
        out = jnp.where(heads[0], x[0:nq], 0.0)
        for h in range(1, 4):
            out = out + jnp.where(heads[h], x[h * nq:(h + 1) * nq], 0.0)
        return out

    def attend(first, r, dil, blk, nq):
        nk = nq + 2 * C_SIDE
        qrows = rows(r + dil * blk * nq, nq, dil)
        koff = C_HALO + r + dil * (blk * nq - C_SIDE)
        qs = load(qw, qrows)
        ks = load(kw, rows(koff, nk, dil)).astype(BF16)
        vs = load(vw, rows(koff, nk, dil)).astype(BF16)
        lhs = jnp.concatenate([jnp.where(hm, qs, 0.0) for hm in heads], axis=0).astype(BF16)
        sc = _dot_nt(lhs, ks)
        a = lax.broadcasted_iota(jnp.int32, sc.shape, 0) & (nq - 1)
        b = lax.broadcasted_iota(jnp.int32, sc.shape, 1)
        kpos = i * span + (koff - C_HALO) + dil * b
        mask = (jnp.abs(a - (b - C_SIDE)) <= C_SIDE) & (kpos >= 0) & (kpos < seq)
        sc = jnp.where(mask, sc, NEG_INF)
        m = jnp.max(sc, axis=1, keepdims=True)
        p = jnp.exp(sc - m)
        l = jnp.sum(p, axis=1, keepdims=True)
        o_new = by_head(_dot(p.astype(BF16), vs), nq)
        m_new = by_head(jnp.broadcast_to(m, (4 * nq, GROUP_W)), nq)
        l_new = by_head(jnp.broadcast_to(l, (4 * nq, GROUP_W)), nq)
        if first:
            store(ob, qrows, o_new)
            store(mb, qrows, m_new)
            store(lb, qrows, l_new)
        else:
            m_old = load(mb, qrows)
            m_tot = jnp.maximum(m_old, m_new)
            w_old, w_new = jnp.exp(m_old - m_tot), jnp.exp(m_new - m_tot)
            store(ob, qrows, w_old * load(ob, qrows) + w_new * o_new)
            store(lb, qrows, w_old * load(lb, qrows) + w_new * l_new)
            store(mb, qrows, m_tot)

    for pat, (_, dil) in enumerate(C_PATTERNS):
        sub = span // dil
        nq = min(sub, 256)
        for r in range(dil):
            for blk in range(sub // nq):
                attend(pat == 0, r, dil, blk, nq)
    full = slice(0, span)
    o_ref[...] = load(ob, full) / load(lb, full)


def _dilated_attention(proj):
    s = proj.shape[0]
    span = 1024
    n = s // span
    per = span // C_HALO
    last = s // C_HALO - 1
    cur = lambda j: pl.BlockSpec((span, GROUP_W), lambda i: (i, j))
    prev = lambda j: pl.BlockSpec((C_HALO, GROUP_W), lambda i: (jnp.maximum(i * per - 1, 0), j))
    nxt = lambda j: pl.BlockSpec((C_HALO, GROUP_W), lambda i: (jnp.minimum((i + 1) * per, last), j))
    win = span + 2 * C_HALO
    return pl.pallas_call(
        functools.partial(_dilated_kernel, seq=s, span=span),
        grid=(n,),
        in_specs=[cur(7), prev(8), cur(8), nxt(8), prev(9), cur(9), nxt(9)],
        out_specs=pl.BlockSpec((span, GROUP_W), lambda i: (i, 0)),
        out_shape=jax.ShapeDtypeStruct((s, GROUP_W), F32),
        scratch_shapes=[pltpu.VMEM((2, span, LANES), F32),
                        pltpu.VMEM((2, win, LANES), F32), pltpu.VMEM((2, win, LANES), F32),
                        pltpu.VMEM((2, span, LANES), F32), pltpu.VMEM((2, span, LANES), F32),
                        pltpu.VMEM((2, span, LANES), F32)],
        compiler_params=_cparams(1),
        name="dilated_attention",
    )(proj, proj, proj, proj, proj, proj, proj)


def _window_kernel(sink_ref, q_ref, kp_ref, kc_ref, kn_ref, vp_ref, vc_ref, vn_ref, o_ref, *, seq):
    i = pl.program_id(0)
    tq = q_ref.shape[0]
    hw = D_HALF_WINDOW
    nk = tq + 2 * hw
    kwin = jnp.concatenate([kp_ref[...], kc_ref[...], kn_ref[...]], axis=0).astype(BF16)
    vwin = jnp.concatenate([vp_ref[...], vc_ref[...], vn_ref[...]], axis=0).astype(BF16)
    lo = lax.broadcasted_iota(jnp.int32, (1, LANES), 1) < HEAD_DIM
    q = q_ref[...] * (HEAD_DIM ** -0.5)
    qa, qb = q[:, :LANES], q[:, LANES:]
    swap = lambda x: pltpu.roll(x, HEAD_DIM, 1)
    lhs = (jnp.concatenate([jnp.where(lo, qa, 0.0), jnp.where(lo, swap(qa), 0.0)], axis=0),
           jnp.concatenate([jnp.where(lo, 0.0, swap(qb)), jnp.where(lo, 0.0, qb)], axis=0))
    a = lax.broadcasted_iota(jnp.int32, (2 * tq, nk), 0) & (tq - 1)
    b = lax.broadcasted_iota(jnp.int32, (2 * tq, nk), 1)
    kpos = i * tq - hw + b
    mask = (jnp.abs(b - hw - a) <= hw) & (kpos >= 0) & (kpos < seq)
    second = lax.broadcasted_iota(jnp.int32, (2 * tq, 1), 0) >= tq
    outs = []
    for g in range(2):
        sc = jnp.where(mask, _dot_nt(lhs[g].astype(BF16), kwin), NEG_INF)
        m = jnp.max(sc, axis=1, keepdims=True)
        p = jnp.exp(sc - m)
        l = jnp.sum(p, axis=1, keepdims=True)
        o = _dot(p.astype(BF16), vwin)
        sk = jnp.where(second, sink_ref[2 * g + 1], sink_ref[2 * g])
        m_tot = jnp.maximum(m, sk)
        w = jnp.exp(m - m_tot)
        den = l * w + jnp.exp(sk - m_tot)
        outs.append(o * (w / den))
    o_ref[:, :LANES] = jnp.where(lo, outs[0][:tq], swap(outs[0][tq:]))
    o_ref[:, LANES:] = jnp.where(lo, swap(outs[1][:tq]), outs[1][tq:])


def _window_attention(proj, sink):
    s = proj.shape[0]
    tq = 256
    n = s // tq
    hw = D_HALF_WINDOW
    per = tq // hw
    last = s // hw - 1
    kcol, vcol = 2816 // LANES, 2944 // LANES
    cur = lambda j: pl.BlockSpec((tq, LANES), lambda i: (i, j))
    prev = lambda j: pl.BlockSpec((hw, LANES), lambda i: (jnp.maximum(i * per - 1, 0), j))
    nxt = lambda j: pl.BlockSpec((hw, LANES), lambda i: (jnp.minimum((i + 1) * per, last), j))
    return pl.pallas_call(
        functools.partial(_window_kernel, seq=s),
        grid=(n,),
        in_specs=[pl.BlockSpec(memory_space=pltpu.SMEM),
                  pl.BlockSpec((tq, GROUP_W), lambda i: (i, 10)),
                  prev(kcol), cur(kcol), nxt(kcol), prev(vcol), cur(vcol), nxt(vcol)],
        out_specs=pl.BlockSpec((tq, GROUP_W), lambda i: (i, 0)),
        out_shape=jax.ShapeDtypeStruct((s, GROUP_W), F32),
        compiler_params=_cparams(1),
        name="window_attention",
    )(sink, proj, proj, proj, proj, proj, proj, proj)


def _outproj_kernel(ya_ref, yb_ref, yc_ref, yd_ref, w_ref, x_ref, g_ref, lw_ref, lb_ref, o_ref):
    y = None
    for k, ref in enumerate((ya_ref, yb_ref, yc_ref, yd_ref)):
        part = _dot(ref[...].astype(BF16), w_ref[k * GROUP_W:(k + 1) * GROUP_W, :])
        y = part if y is None else y + part
    r = DEEPNORM_ALPHA * x_ref[...] + (1.0 + g_ref[...]) * y
    o_ref[...] = _layer_norm(r, lw_ref[...], lb_ref[...])


def _outproj(ys, w_bf16, x, gate, ln_w, ln_b):
    s, d = x.shape
    tm = 512
    row = lambda i: (i, 0)
    const = lambda i: (0, 0)
    vec = pl.BlockSpec((1, d), const)
    grp = pl.BlockSpec((tm, GROUP_W), row)
    return pl.pallas_call(
        _outproj_kernel,
        grid=(s // tm,),
        in_specs=[grp, grp, grp, grp, pl.BlockSpec((4 * GROUP_W, d), const), pl.BlockSpec((tm, d), row),
                  vec, vec, vec],
        out_specs=pl.BlockSpec((tm, d), row),
        out_shape=jax.ShapeDtypeStruct((s, d), F32),
        compiler_params=_cparams(1),
        name="outproj_ln",
    )(*ys, w_bf16, x, gate, ln_w, ln_b)


FFN_CHUNK = 256


def _ffn_kernel(x_ref, sc_ref, sh_ref, wu_ref, wd_ref, g_ref, lw_ref, lb_ref, o_ref):
    x = x_ref[...]
    h = (x * (1.0 + sc_ref[...]) + sh_ref[...]).astype(BF16)
    acc = jnp.zeros(x.shape, F32)
    for c0 in range(0, FFN_DIM, FFN_CHUNK):
        gate = _dot(h, wu_ref[:, c0:c0 + FFN_CHUNK])
        up = _dot(h, wu_ref[:, FFN_DIM + c0:FFN_DIM + c0 + FFN_CHUNK])
        acc = acc + _dot((_silu(gate) * up).astype(BF16), wd_ref[c0:c0 + FFN_CHUNK, :])
    r = DEEPNORM_ALPHA * x + (1.0 + g_ref[...]) * acc
    o_ref[...] = _layer_norm(r, lw_ref[...], lb_ref[...])


def _dense_ffn(x, scale, shift, wu_bf16, wd_bf16, gate, ln_w, ln_b):
    s, d = x.shape
    tm = 512
    row = lambda i: (i, 0)
    const = lambda i: (0, 0)
    vec = pl.BlockSpec((1, d), const)
    return pl.pallas_call(
        _ffn_kernel,
        grid=(s // tm,),
        in_specs=[pl.BlockSpec((tm, d), row), vec, vec,
                  pl.BlockSpec((d, 2 * FFN_DIM), const), pl.BlockSpec((FFN_DIM, d), const), vec, vec, vec],
        out_specs=pl.BlockSpec((tm, d), row),
        out_shape=jax.ShapeDtypeStruct((s, d), F32),
        compiler_params=_cparams(1),
        name="dense_ffn_ln",
    )(x, scale, shift, wu_bf16, wd_bf16, gate, ln_w, ln_b)


def _router_kernel(x_ref, sc_ref, sh_ref, w_ref, h_ref, e_ref, p_ref):
    h = x_ref[...] * (1.0 + sc_ref[...]) + sh_ref[...]
    h_ref[...] = h.astype(BF16)
    h_hi, h_lo = _split_bf16(h)
    w_hi, w_lo = _split_bf16(w_ref[...])
    logits = _dot(h_hi, w_hi) + _dot(h_hi, w_lo) + _dot(h_lo, w_hi)
    lane = lax.broadcasted_iota(jnp.int32, logits.shape, 1)
    logits = jnp.where(lane < N_EXPERTS, logits, NEG_INF)
    m1 = jnp.max(logits, axis=1, keepdims=True)
    e1 = jnp.min(jnp.where(logits == m1, lane, LANES), axis=1, keepdims=True)
    rest = jnp.where(lane == e1, NEG_INF, logits)
    m2 = jnp.max(rest, axis=1, keepdims=True)
    e2 = jnp.min(jnp.where(rest == m2, lane, LANES), axis=1, keepdims=True)
    t = jnp.exp(m2 - m1)
    g1 = 1.0 / (1.0 + t)
    e_ref[...] = jnp.where(lane == 0, e1, jnp.where(lane == 1, e2, 0))
    p_ref[...] = jnp.where(lane == 0, g1, jnp.where(lane == 1, t * g1, 0.0))


def _router(x, scale, shift, w_router):
    s, d = x.shape
    tm = 512
    row = lambda i: (i, 0)
    const = lambda i: (0, 0)
    vec = pl.BlockSpec((1, d), const)
    wpad = jnp.pad(w_router, ((0, 0), (0, LANES - N_EXPERTS)))
    return pl.pallas_call(
        _router_kernel,
        grid=(s // tm,),
        in_specs=[pl.BlockSpec((tm, d), row), vec, vec, pl.BlockSpec((d, LANES), const)],
        out_specs=[pl.BlockSpec((tm, d), row), pl.BlockSpec((tm, LANES), row), pl.BlockSpec((tm, LANES), row)],
        out_shape=[jax.ShapeDtypeStruct((s, d), BF16), jax.ShapeDtypeStruct((s, LANES), jnp.int32),
                   jax.ShapeDtypeStruct((s, LANES), F32)],
        compiler_params=_cparams(1),
        name="moe_router",
    )(x, scale, shift, wpad)


MOE_TILE = 512
MOE_CHUNK = 896


def _expert_kernel(te_ref, nu_ref, x_ref, wg_ref, wu_ref, wd_ref, o_ref):
    i, f = pl.program_id(0), pl.program_id(1)

    @pl.when(f == 0)
    def _():
        o_ref[...] = jnp.zeros_like(o_ref)

    @pl.when(i < nu_ref[0])
    def _():
        x = x_ref[...]
        act = _silu(_dot(x, wg_ref[...])) * _dot(x, wu_ref[...])
        o_ref[...] += _dot(act.astype(BF16), wd_ref[...])


def _experts(xs, tile_expert, n_used, wu_bf16, wd_bf16):
    p, d = xs.shape
    n_tiles = p // MOE_TILE
    nf = EXPERT_DIM // MOE_CHUNK
    fidx = lambda i, f, nu: jnp.where(i < nu[0], f, nf - 1)
    grid_spec = pltpu.PrefetchScalarGridSpec(
        num_scalar_prefetch=2,
        grid=(n_tiles, nf),
        in_specs=[pl.BlockSpec((MOE_TILE, d), lambda i, f, te, nu: (i, 0)),
                  pl.BlockSpec((None, d, MOE_CHUNK), lambda i, f, te, nu: (te[i], 0, fidx(i, f, nu))),
                  pl.BlockSpec((None, d, MOE_CHUNK), lambda i, f, te, nu: (te[i], 0, nf + fidx(i, f, nu))),
                  pl.BlockSpec((None, MOE_CHUNK, d), lambda i, f, te, nu: (te[i], fidx(i, f, nu), 0))],
        out_specs=pl.BlockSpec((MOE_TILE, d), lambda i, f, te, nu: (i, 0)),
    )
    return pl.pallas_call(
        _expert_kernel,
        grid_spec=grid_spec,
        out_shape=jax.ShapeDtypeStruct((p, d), F32),
        compiler_params=_cparams(2),
        name="moe_experts",
    )(tile_expert, n_used, xs, wu_bf16, wu_bf16, wd_bf16)


def _combine_kernel(x_ref, y1_ref, y2_ref, p_ref, g_ref, lw_ref, lb_ref, o_ref):
    p = p_ref[...]
    f = p[:, 0:1] * y1_ref[...] + p[:, 1:2] * y2_ref[...]
    r = DEEPNORM_ALPHA * x_ref[...] + (1.0 + g_ref[...]) * f
    o_ref[...] = _layer_norm(r, lw_ref[...], lb_ref[...])


def _combine(x, y1, y2, probs, gate, ln_w, ln_b):
    s, d = x.shape
    tm = 512
    row = lambda i: (i, 0)
    vec = pl.BlockSpec((1, d), lambda i: (0, 0))
    big = pl.BlockSpec((tm, d), row)
    return pl.pallas_call(
        _combine_kernel,
        grid=(s // tm,),
        in_specs=[big, big, big, pl.BlockSpec((tm, LANES), row), vec, vec, vec],
        out_specs=big,
        out_shape=jax.ShapeDtypeStruct((s, d), F32),
        compiler_params=_cparams(1),
        name="moe_combine_ln",
    )(x, y1, y2, probs, gate, ln_w, ln_b)


def _moe(x, scale, shift, w_router, wu_bf16, wd_bf16, gate, ln_w, ln_b):
    t, d = x.shape
    h_bf16, sel, probs = _router(x, scale, shift, w_router)
    flat_e = sel[:, :TOP_K].reshape(-1)
    n_assign = t * TOP_K
    n_tiles = -(-n_assign // MOE_TILE) + N_EXPERTS
    onehot = (flat_e[:, None] == jnp.arange(N_EXPERTS, dtype=jnp.int32)[None, :]).astype(jnp.int32)
    rank = jnp.sum((jnp.cumsum(onehot, axis=0) - onehot) * onehot, axis=1)
    counts = jnp.sum(onehot, axis=0)
    padded = (counts + MOE_TILE - 1) // MOE_TILE * MOE_TILE
    padded_end = jnp.cumsum(padded)
    slot = (padded_end - padded)[flat_e] + rank
    flat_tok = jnp.repeat(jnp.arange(t, dtype=jnp.int32), TOP_K)
    slot_tok = jnp.full((n_tiles * MOE_TILE,), t, jnp.int32).at[slot].set(flat_tok)
    n_used = (padded_end[-1] // MOE_TILE).astype(jnp.int32).reshape(1)
    tile_ids = jnp.minimum(jnp.arange(n_tiles, dtype=jnp.int32), n_used[0] - 1)
    tile_expert = jnp.minimum(jnp.searchsorted(padded_end, tile_ids * MOE_TILE, side="right"),
                              N_EXPERTS - 1).astype(jnp.int32)
    h_pad = jnp.concatenate([h_bf16, jnp.zeros((1, d), BF16)], axis=0)
    ys = _experts(h_pad[slot_tok], tile_expert, n_used, wu_bf16, wd_bf16)
    slot2 = slot.reshape(t, TOP_K)
    return _combine(x, ys[slot2[:, 0]], ys[slot2[:, 1]], probs, gate, ln_w, ln_b)


def kernel(x, c, positions, w_ada, b_ada, w_in, w_out, a_lower_bound, a_norm_w, b_conv_w, b_conv_b, b_norm_w,
           b_norm_b, d_sink, ln_w, ln_b, ffn_w_up, ffn_w_down, moe_router, moe_w_up, moe_w_down):
    batch, s, d = x.shape
    assert batch == 1 and d == D_MODEL
    x = x.reshape(s, d)
    mod = _adaln(c, w_ada, b_ada)
    rope = _rope_tables(positions)
    lb_cum = jnp.cumsum(jax.nn.softmax(a_lower_bound.astype(F32), axis=0), axis=0)
    lb_all = lb_cum - lb_cum[0]
    row = lambda v: v.reshape(1, -1)
    for layer in range(DEPTH):
        shift1, scale1, gate1, shift2, scale2, gate2 = [mod[layer, :, k * d:(k + 1) * d] for k in range(6)]
        proj = _inproj(x, scale1, shift1, w_in[layer].astype(BF16), rope)
        ya = _hgrn2(proj, row(lb_all[layer]), row(jnp.tile(a_norm_w[layer], GROUP_W // HEAD_DIM)))
        yb = _conv_module(proj, b_conv_w[layer], row(b_conv_b[layer]), row(b_norm_w[layer]), row(b_norm_b[layer]))
        yc = _dilated_attention(proj)
        yd = _window_attention(proj, d_sink[layer])
        x = _outproj((ya, yb, yc, yd), w_out[layer].astype(BF16), x, gate1, row(ln_w[layer, 0]), row(ln_b[layer, 0]))
        lw, lbias = row(ln_w[layer, 1]), row(ln_b[layer, 1])
        if layer % 2 == 0:
            x = _dense_ffn(x, scale2, shift2, ffn_w_up[layer // 2].astype(BF16),
                           ffn_w_down[layer // 2].astype(BF16), gate2, lw, lbias)
        else:
            x = _moe(x, scale2, shift2, moe_router[layer // 2], moe_w_up[layer // 2].astype(BF16),
                     moe_w_down[layer // 2].astype(BF16), gate2, lw, lbias)
    return x.reshape(batch, s, d)
```

```python
import functools

import numpy as np
import jax
import jax.numpy as jnp
from jax import lax
from jax.experimental import pallas as pl
from jax.experimental.pallas import tpu as pltpu

F32 = jnp.float32
BF16 = jnp.bfloat16

D_MODEL = 1024
DEPTH = 2
HEAD_DIM = 64
A_CHUNK = 16
B_KERNEL = 31
GROUP_W = 256
C_PATTERNS = ((128, 1), (512, 4), (2048, 16))
C_SIDE = 64
C_HALO = 1024
D_HALF_WINDOW = 128
ROPE_THETA = 500000.0
ROPE_DIM = HEAD_DIM // 4
FFN_DIM = 2816
N_EXPERTS = 8
TOP_K = 2
EXPERT_DIM = 3584
IN_WIDTH = 3072
DEEPNORM_ALPHA = (2 * DEPTH) ** 0.25
LN_EPS = 1e-5
RMS_EPS = 1e-6
NEG_INF = -1e30

LANES = 128
VMEM_LIMIT = 56 * 1024 * 1024


def _cparams(n_axes=1, vmem=VMEM_LIMIT):
    return pltpu.CompilerParams(dimension_semantics=("arbitrary",) * n_axes, vmem_limit_bytes=vmem)


def _sigmoid(x):
    return 1.0 / (1.0 + jnp.exp(-x))


def _silu(x):
    return x * _sigmoid(x)


def _split_bf16(x):
    hi = x.astype(BF16)
    lo = (x - hi.astype(F32)).astype(BF16)
    return hi, lo


def _layer_norm(r, w, b):
    mu = jnp.mean(r, axis=-1, keepdims=True)
    d = r - mu
    var = jnp.mean(d * d, axis=-1, keepdims=True)
    return d * lax.rsqrt(var + LN_EPS) * w + b


def _dot_nt(a, b):
    return lax.dot_general(a, b, (((1,), (1,)), ((), ())), preferred_element_type=F32)


def _dot_tn(a, b):
    return lax.dot_general(a, b, (((0,), (0,)), ((), ())), preferred_element_type=F32)


def _dot(a, b):
    return jnp.dot(a, b, preferred_element_type=F32)


def _adaln_kernel(c_ref, w_ref, b_ref, o_ref):
    a_hi, a_lo = _split_bf16(_silu(c_ref[...]))
    w_hi, w_lo = _split_bf16(w_ref[...])
    o_ref[...] = _dot(a_hi, w_hi) + _dot(a_hi, w_lo) + _dot(a_lo, w_hi) + b_ref[...]


def _adaln(c, w_ada, b_ada):
    depth, d, n = w_ada.shape
    tn = 1536
    c8 = jnp.broadcast_to(c.reshape(1, d), (8, d))
    out = pl.pallas_call(
        _adaln_kernel,
        grid=(depth, n // tn),
        in_specs=[pl.BlockSpec((8, d), lambda l, j: (0, 0)),
                  pl.BlockSpec((None, d, tn), lambda l, j: (l, 0, j)),
                  pl.BlockSpec((None, 1, tn), lambda l, j: (l, 0, j))],
        out_specs=pl.BlockSpec((None, 8, tn), lambda l, j: (l, 0, j)),
        out_shape=jax.ShapeDtypeStruct((depth, 8, n), F32),
        compiler_params=_cparams(2),
        name="adaln",
    )(c8, w_ada, b_ada.reshape(depth, 1, n))
    return out[:, 0:1, :]


def _rope_kernel(pos_ref, invf_ref, c_ref, s1_ref, s2_ref):
    ang = pos_ref[...].astype(F32) * invf_ref[...]
    j = lax.broadcasted_iota(jnp.int32, ang.shape, 1) & (HEAD_DIM - 1)
    cosv, sinv = jnp.cos(ang), jnp.sin(ang)
    half = ROPE_DIM // 2
    c_ref[...] = jnp.where(j < ROPE_DIM, cosv, 1.0)
    s1_ref[...] = jnp.where(j < half, -sinv, 0.0)
    s2_ref[...] = jnp.where((j >= half) & (j < ROPE_DIM), sinv, 0.0)


def _rope_tables(positions):
    s = positions.shape[-1]
    ts = 2048
    half = ROPE_DIM // 2
    inv_freq = np.float32(ROPE_THETA) ** (-np.arange(half, dtype=np.float32) * np.float32(2.0 / ROPE_DIM))
    invf = jnp.asarray(inv_freq[(np.arange(LANES) % HEAD_DIM) % half].reshape(1, LANES).astype(np.float32))
    spec = pl.BlockSpec((ts, LANES), lambda i: (i, 0))
    return pl.pallas_call(
        _rope_kernel,
        grid=(s // ts,),
        in_specs=[pl.BlockSpec((ts, 1), lambda i: (i, 0)), pl.BlockSpec((1, LANES), lambda i: (0, 0))],
        out_specs=[spec, spec, spec],
        out_shape=[jax.ShapeDtypeStruct((s, LANES), F32)] * 3,
        compiler_params=_cparams(1),
        name="rope_tables",
    )(positions.reshape(s, 1), invf)


_ROPE_COL_CHUNKS = (1792, 1920, 2048, 2176, 2560, 2688, 2816)


def _inproj_kernel(x_ref, sc_ref, sh_ref, w_ref, rc_ref, rs1_ref, rs2_ref, o_ref):
    h = (x_ref[...] * (1.0 + sc_ref[...]) + sh_ref[...]).astype(BF16)
    half = ROPE_DIM // 2
    for c0 in range(0, IN_WIDTH, GROUP_W):
        acc = _dot(h, w_ref[:, c0:c0 + GROUP_W])
        for k in range(GROUP_W // LANES):
            col = c0 + k * LANES
            part = acc[:, k * LANES:(k + 1) * LANES]
            if col in _ROPE_COL_CHUNKS:
                part = (part * rc_ref[...] + pltpu.roll(part, LANES - half, 1) * rs1_ref[...]
                        + pltpu.roll(part, half, 1) * rs2_ref[...])
            o_ref[:, col:col + LANES] = part


def _inproj(x, scale, shift, w_bf16, rope):
    s, d = x.shape
    tm = 512
    row = lambda i: (i, 0)
    const = lambda i: (0, 0)
    return pl.pallas_call(
        _inproj_kernel,
        grid=(s // tm,),
        in_specs=[pl.BlockSpec((tm, d), row), pl.BlockSpec((1, d), const), pl.BlockSpec((1, d), const),
                  pl.BlockSpec((d, IN_WIDTH), const),
                  pl.BlockSpec((tm, LANES), row), pl.BlockSpec((tm, LANES), row), pl.BlockSpec((tm, LANES), row)],
        out_specs=pl.BlockSpec((tm, IN_WIDTH), row),
        out_shape=jax.ShapeDtypeStruct((s, IN_WIDTH), F32),
        compiler_params=_cparams(1),
        name="inproj",
    )(x, scale, shift, w_bf16, *rope)


def _hgrn_core(aq, z, v, lb, st_ref, kbuf, cbuf, vbuf, qt_ref, kt_ref, vt_ref, dec_ref, oi_ref, rev):
    ts = aq.shape[0]
    nsub = ts // A_CHUNK
    q = _silu(aq)
    ls = jnp.minimum(z, 0.0) - jnp.log1p(jnp.exp(-jnp.abs(z)))
    b = jnp.log1p(-lb) + ls
    loglb = jnp.log(lb)
    logf = jnp.maximum(loglb, b) + jnp.log1p(jnp.exp(-jnp.abs(loglb - b)))
    kk = (1.0 - lb) * _sigmoid(-z)

    r = lax.broadcasted_iota(jnp.int32, (2 * ts, ts), 0)
    c = lax.broadcasted_iota(jnp.int32, (2 * ts, ts), 1)
    rr = jnp.where(r >= ts, r - ts, r)
    same = (rr >> 4) == (c >> 4)
    keep = same & ((r >= ts) | ((c >= rr) if rev else (c <= rr)))
    tmat = jnp.where(keep, 1.0, 0.0).astype(BF16)
    lf_hi, lf_lo = _split_bf16(logf)
    res = _dot(tmat, lf_hi) + _dot(tmat, lf_lo)
    cum, tot = res[:ts], res[ts:]

    zpad = jnp.zeros((A_CHUNK, GROUP_W), F32)
    for buf, val in ((kbuf, kk), (cbuf, cum), (vbuf, v)):
        buf[0:A_CHUNK, :] = zpad
        buf[A_CHUNK + ts:, :] = zpad
        buf[A_CHUNK:A_CHUNK + ts, :] = val

    li = lax.broadcasted_iota(jnp.int32, (GROUP_W, GROUP_W), 0)
    lj = lax.broadcasted_iota(jnp.int32, (GROUP_W, GROUP_W), 1)
    bd = (li >> 6) == (lj >> 6)
    ones_bd = jnp.where(bd, 1.0, 0.0).astype(BF16)
    tmod = lax.broadcasted_iota(jnp.int32, (ts, 1), 0) & (A_CHUNK - 1)

    o = jnp.zeros((ts, GROUP_W), F32)
    for d in range(A_CHUNK):
        off = A_CHUNK + d if rev else A_CHUNK - d
        ks = kbuf[pl.ds(off, ts), :]
        cs = cbuf[pl.ds(off, ts), :]
        vs = vbuf[pl.ds(off, ts), :]
        valid = (tmod <= A_CHUNK - 1 - d) if rev else (tmod >= d)
        xx = jnp.where(valid, q * ks * jnp.exp(cum - cs), 0.0)
        o = o + _dot(xx.astype(BF16), ones_bd) * vs

    qt_ref[...] = (q * jnp.exp(cum)).astype(BF16)
    kt_ref[...] = (kk * jnp.exp(tot - cum)).astype(BF16)
    vt_ref[...] = v.astype(BF16)
    dec_ref[...] = jnp.exp(tot)

    def step(i, carry):
        n = (nsub - 1 - i) if rev else i
        r0 = pl.multiple_of(n * A_CHUNK, A_CHUNK)
        rows = pl.ds(r0, A_CHUNK)
        state = st_ref[...]
        oi_ref[rows, :] = _dot_nt(qt_ref[rows, :], state.astype(BF16))
        kv = _dot_tn(vt_ref[rows, :], kt_ref[rows, :])
        st_ref[...] = state * dec_ref[pl.ds(r0, 1), :] + jnp.where(bd, kv, 0.0)
        return carry

    lax.fori_loop(0, nsub, step, 0)
    return o + oi_ref[...]


def _hgrn_fwd_kernel(q_ref, f_ref, i_ref, lb_ref, o_ref, st_ref, *scratch):
    @pl.when(pl.program_id(0) == 0)
    def _():
        st_ref[...] = jnp.zeros_like(st_ref)

    o_ref[...] = _hgrn_core(q_ref[...], f_ref[...], i_ref[...], lb_ref[...], st_ref, *scratch, rev=False)


def _hgrn_bwd_kernel(q_ref, f_ref, i_ref, g_ref, of_ref, lb_ref, nw_ref, o_ref, st_ref, *scratch):
    @pl.when(pl.program_id(0) == 0)
    def _():
        st_ref[...] = jnp.zeros_like(st_ref)

    o = of_ref[...] + _hgrn_core(q_ref[...], f_ref[...], i_ref[...], lb_ref[...], st_ref, *scratch, rev=True)
    li = lax.broadcasted_iota(jnp.int32, (GROUP_W, GROUP_W), 0)
    lj = lax.broadcasted_iota(jnp.int32, (GROUP_W, GROUP_W), 1)
    ones_bd = jnp.where((li >> 6) == (lj >> 6), 1.0, 0.0).astype(BF16)
    sq_hi, sq_lo = _split_bf16(o * o)
    ms = (_dot(sq_hi, ones_bd) + _dot(sq_lo, ones_bd)) * (1.0 / HEAD_DIM)
    o_ref[...] = o * lax.rsqrt(ms + RMS_EPS) * nw_ref[...] * _silu(g_ref[...])


def _hgrn_scratch(ts):
    pad = ts + 2 * A_CHUNK
    return [pltpu.VMEM((GROUP_W, GROUP_W), F32),
            pltpu.VMEM((pad, GROUP_W), F32), pltpu.VMEM((pad, GROUP_W), F32), pltpu.VMEM((pad, GROUP_W), F32),
            pltpu.VMEM((ts, GROUP_W), BF16), pltpu.VMEM((ts, GROUP_W), BF16), pltpu.VMEM((ts, GROUP_W), BF16),
            pltpu.VMEM((ts, GROUP_W), F32), pltpu.VMEM((ts, GROUP_W), F32)]


def _hgrn2(proj, lb, norm_w):
    s = proj.shape[0]
    ts = 256
    n = s // ts
    col = lambda j: pl.BlockSpec((ts, GROUP_W), lambda i: (i, j))
    rcol = lambda j: pl.BlockSpec((ts, GROUP_W), lambda i: (n - 1 - i, j))
    vec = pl.BlockSpec((1, GROUP_W), lambda i: (0, 0))
    o_f = pl.pallas_call(
        _hgrn_fwd_kernel,
        grid=(n,),
        in_specs=[col(0), col(1), col(3), vec],
        out_specs=pl.BlockSpec((ts, GROUP_W), lambda i: (i, 0)),
        out_shape=jax.ShapeDtypeStruct((s, GROUP_W), F32),
        scratch_shapes=_hgrn_scratch(ts),
        compiler_params=_cparams(1),
        name="hgrn_fwd",
    )(proj, proj, proj, lb)
    return pl.pallas_call(
        _hgrn_bwd_kernel,
        grid=(n,),
        in_specs=[rcol(0), rcol(2), rcol(3), rcol(4), pl.BlockSpec((ts, GROUP_W), lambda i: (n - 1 - i, 0)),
                  vec, vec],
        out_specs=pl.BlockSpec((ts, GROUP_W), lambda i: (n - 1 - i, 0)),
        out_shape=jax.ShapeDtypeStruct((s, GROUP_W), F32),
        scratch_shapes=_hgrn_scratch(ts),
        compiler_params=_cparams(1),
        name="hgrn_bwd",
    )(proj, proj, proj, proj, o_f, lb, norm_w)


B_PAD = 16


def _conv_kernel(v_ref, g_ref, vp_ref, gp_ref, vn_ref, gn_ref, w_ref, b_ref, nw_ref, nb_ref, o_ref, ubuf):
    i = pl.program_id(0)
    ts = v_ref.shape[0]
    glu = lambda a, b: a * _sigmoid(b)
    ubuf[0:B_PAD, :] = jnp.where(i > 0, glu(vp_ref[...], gp_ref[...]), 0.0)
    ubuf[B_PAD:B_PAD + ts, :] = glu(v_ref[...], g_ref[...])
    ubuf[B_PAD + ts:, :] = jnp.where(i < pl.num_programs(0) - 1, glu(vn_ref[...], gn_ref[...]), 0.0)
    acc = jnp.zeros((ts, GROUP_W), F32) + b_ref[...]
    for j in range(B_KERNEL):
        acc = acc + ubuf[pl.ds(B_PAD - B_KERNEL // 2 + j, ts), :] * w_ref[j:j + 1, :]
    o_ref[...] = _silu(_layer_norm(acc, nw_ref[...], nb_ref[...]))


def _conv_module(proj, conv_w, conv_b, norm_w, norm_b):
    s = proj.shape[0]
    ts = 512
    n = s // ts
    per = ts // B_PAD
    last = s // B_PAD - 1
    cur = lambda j: pl.BlockSpec((ts, GROUP_W), lambda i: (i, j))
    prev = lambda j: pl.BlockSpec((B_PAD, GROUP_W), lambda i: (jnp.maximum(i * per - 1, 0), j))
    nxt = lambda j: pl.BlockSpec((B_PAD, GROUP_W), lambda i: (jnp.minimum((i + 1) * per, last), j))
    vec = pl.BlockSpec((1, GROUP_W), lambda i: (0, 0))
    wpad = jnp.pad(conv_w, ((0, 1), (0, 0)))
    return pl.pallas_call(
        _conv_kernel,
        grid=(n,),
        in_specs=[cur(5), cur(6), prev(5), prev(6), nxt(5), nxt(6),
                  pl.BlockSpec((B_KERNEL + 1, GROUP_W), lambda i: (0, 0)), vec, vec, vec],
        out_specs=pl.BlockSpec((ts, GROUP_W), lambda i: (i, 0)),
        out_shape=jax.ShapeDtypeStruct((s, GROUP_W), F32),
        scratch_shapes=[pltpu.VMEM((ts + 2 * B_PAD, GROUP_W), F32)],
        compiler_params=_cparams(1),
        name="conv_module",
    )(proj, proj, proj, proj, proj, proj, wpad, conv_b, norm_w, norm_b)


def _dilated_kernel(q_ref, kp_ref, kc_ref, kn_ref, vp_ref, vc_ref, vn_ref, o_ref, qw, kw, vw, ob, mb, lb, *, seq, span):
    i = pl.program_id(0)

    def put(dst, row0, val):
        n = val.shape[0]
        dst[0, row0:row0 + n, :] = val[:, :LANES]
        dst[1, row0:row0 + n, :] = val[:, LANES:]

    put(qw, 0, q_ref[...] * (HEAD_DIM ** -0.5))
    for dst, refs in ((kw, (kp_ref, kc_ref, kn_ref)), (vw, (vp_ref, vc_ref, vn_ref))):
        put(dst, 0, refs[0][...])
        put(dst, C_HALO, refs[1][...])
        put(dst, C_HALO + span, refs[2][...])

    head = lax.broadcasted_iota(jnp.int32, (1, GROUP_W), 1) >> 6
    heads = [head == h for h in range(4)]

    def rows(start, size, dil):
        return pl.ds(start, size) if dil == 1 else pl.ds(start, size, stride=dil)

    def load(src, sel):
        return jnp.concatenate([src[0, sel, :], src[1, sel, :]], axis=1)

    def store(dst, sel, val):
        dst[0, sel, :] = val[:, :LANES]
        dst[1, sel, :] = val[:, LANES:]

    def by_head(x, nq):
        out = jnp.where(heads[0], x[0:nq], 0.0)
        for h in range(1, 4):
            out = out + jnp.where(heads[h], x[h * nq:(h + 1) * nq], 0.0)
        return out

    def attend(first, r, dil, blk, nq):
        nk = nq + 2 * C_SIDE
        qrows = rows(r + dil * blk * nq, nq, dil)
        koff = C_HALO + r + dil * (blk * nq - C_SIDE)
        qs = load(qw, qrows)
        ks = load(kw, rows(koff, nk, dil)).astype(BF16)
        vs = load(vw, rows(koff, nk, dil)).astype(BF16)
        lhs = jnp.concatenate([jnp.where(hm, qs, 0.0) for hm in heads], axis=0).astype(BF16)
        sc = _dot_nt(lhs, ks)
        a = lax.broadcasted_iota(jnp.int32, sc.shape, 0) & (nq - 1)
        b = lax.broadcasted_iota(jnp.int32, sc.shape, 1)
        kpos = i * span + (koff - C_HALO) + dil * b
        mask = (jnp.abs(a - (b - C_SIDE)) <= C_SIDE) & (kpos >= 0) & (kpos < seq)
        sc = jnp.where(mask, sc, NEG_INF)
        m = jnp.max(sc, axis=1, keepdims=True)
        p = jnp.exp(sc - m)
        l = jnp.sum(p, axis=1, keepdims=True)
        o_new = by_head(_dot(p.astype(BF16), vs), nq)
        m_new = by_head(jnp.broadcast_to(m, (4 * nq, GROUP_W)), nq)
        l_new = by_head(jnp.broadcast_to(l, (4 * nq, GROUP_W)), nq)
        if first:
            store(ob, qrows, o_new)
            store(mb, qrows, m_new)
            store(lb, qrows, l_new)
        else:
            m_old = load(mb, qrows)
            m_tot = jnp.maximum(m_old, m_new)
            w_old, w_new = jnp.exp(m_old - m_tot), jnp.exp(m_new - m_tot)
            store(ob, qrows, w_old * load(ob, qrows) + w_new * o_new)
            store(lb, qrows, w_old * load(lb, qrows) + w_new * l_new)
            store(mb, qrows, m_tot)

    for pat, (_, dil) in enumerate(C_PATTERNS):
        sub = span // dil
        nq = min(sub, 256)
        for r in range(dil):
            for blk in range(sub // nq):
                attend(pat == 0, r, dil, blk, nq)
    full = slice(0, span)
    o_ref[...] = load(ob, full) / load(lb, full)


def _dilated_attention(proj):
    s = proj.shape[0]
    span = 1024
    n = s // span
    per = span // C_HALO
    last = s // C_HALO - 1
    cur = lambda j: pl.BlockSpec((span, GROUP_W), lambda i: (i, j))
    prev = lambda j: pl.BlockSpec((C_HALO, GROUP_W), lambda i: (jnp.maximum(i * per - 1, 0), j))
    nxt = lambda j: pl.BlockSpec((C_HALO, GROUP_W), lambda i: (jnp.minimum((i + 1) * per, last), j))
    win = span + 2 * C_HALO
    return pl.pallas_call(
        functools.partial(_dilated_kernel, seq=s, span=span),
        grid=(n,),
        in_specs=[cur(7), prev(8), cur(8), nxt(8), prev(9), cur(9), nxt(9)],
        out_specs=pl.BlockSpec((span, GROUP_W), lambda i: (i, 0)),
        out_shape=jax.ShapeDtypeStruct((s, GROUP_W), F32),
        scratch_shapes=[pltpu.VMEM((2, span, LANES), F32),
                        pltpu.VMEM((2, win, LANES), F32), pltpu.VMEM((2, win, LANES), F32),
                        pltpu.VMEM((2, span, LANES), F32), pltpu.VMEM((2, span, LANES), F32),
                        pltpu.VMEM((2, span, LANES), F32)],
        compiler_params=_cparams(1),
        name="dilated_attention",
    )(proj, proj, proj, proj, proj, proj, proj)


def _window_kernel(sink_ref, q_ref, kp_ref, kc_ref, kn_ref, vp_ref, vc_ref, vn_ref, o_ref, *, seq):
    i = pl.program_id(0)
    tq = q_ref.shape[0]
    hw = D_HALF_WINDOW
    nk = tq + 2 * hw
    kwin = jnp.concatenate([kp_ref[...], kc_ref[...], kn_ref[...]], axis=0).astype(BF16)
    vwin = jnp.concatenate([vp_ref[...], vc_ref[...], vn_ref[...]], axis=0).astype(BF16)
    lo = lax.broadcasted_iota(jnp.int32, (1, LANES), 1) < HEAD_DIM
    q = q_ref[...] * (HEAD_DIM ** -0.5)
    qa, qb = q[:, :LANES], q[:, LANES:]
    swap = lambda x: pltpu.roll(x, HEAD_DIM, 1)
    lhs = (jnp.concatenate([jnp.where(lo, qa, 0.0), jnp.where(lo, swap(qa), 0.0)], axis=0),
           jnp.concatenate([jnp.where(lo, 0.0, swap(qb)), jnp.where(lo, 0.0, qb)], axis=0))
    a = lax.broadcasted_iota(jnp.int32, (2 * tq, nk), 0) & (tq - 1)
    b = lax.broadcasted_iota(jnp.int32, (2 * tq, nk), 1)
    kpos = i * tq - hw + b
    mask = (jnp.abs(b - hw - a) <= hw) & (kpos >= 0) & (kpos < seq)
    second = lax.broadcasted_iota(jnp.int32, (2 * tq, 1), 0) >= tq
    outs = []
    for g in range(2):
        sc = jnp.where(mask, _dot_nt(lhs[g].astype(BF16), kwin), NEG_INF)
        m = jnp.max(sc, axis=1, keepdims=True)
        p = jnp.exp(sc - m)
        l = jnp.sum(p, axis=1, keepdims=True)
        o = _dot(p.astype(BF16), vwin)
        sk = jnp.where(second, sink_ref[2 * g + 1], sink_ref[2 * g])
        m_tot = jnp.maximum(m, sk)
        w = jnp.exp(m - m_tot)
        den = l * w + jnp.exp(sk - m_tot)
        outs.append(o * (w / den))
    o_ref[:, :LANES] = jnp.where(lo, outs[0][:tq], swap(outs[0][tq:]))
    o_ref[:, LANES:] = jnp.where(lo, swap(outs[1][:tq]), outs[1][tq:])


def _window_attention(proj, sink):
    s = proj.shape[0]
    tq = 256
    n = s // tq
    hw = D_HALF_WINDOW
    per = tq // hw
    last = s // hw - 1
    kcol, vcol = 2816 // LANES, 2944 // LANES
    cur = lambda j: pl.BlockSpec((tq, LANES), lambda i: (i, j))
    prev = lambda j: pl.BlockSpec((hw, LANES), lambda i: (jnp.maximum(i * per - 1, 0), j))
    nxt = lambda j: pl.BlockSpec((hw, LANES), lambda i: (jnp.minimum((i + 1) * per, last), j))
    return pl.pallas_call(
        functools.partial(_window_kernel, seq=s),
        grid=(n,),
        in_specs=[pl.BlockSpec(memory_space=pltpu.SMEM),
                  pl.BlockSpec((tq, GROUP_W), lambda i: (i, 10)),
                  prev(kcol), cur(kcol), nxt(kcol), prev(vcol), cur(vcol), nxt(vcol)],
        out_specs=pl.BlockSpec((tq, GROUP_W), lambda i: (i, 0)),
        out_shape=jax.ShapeDtypeStruct((s, GROUP_W), F32),
        compiler_params=_cparams(1),
        name="window_attention",
    )(sink, proj, proj, proj, proj, proj, proj, proj)


def _outproj_kernel(ya_ref, yb_ref, yc_ref, yd_ref, w_ref, x_ref, g_ref, lw_ref, lb_ref, o_ref):
    y = None
    for k, ref in enumerate((ya_ref, yb_ref, yc_ref, yd_ref)):
        part = _dot(ref[...].astype(BF16), w_ref[k * GROUP_W:(k + 1) * GROUP_W, :])
        y = part if y is None else y + part
    r = DEEPNORM_ALPHA * x_ref[...] + (1.0 + g_ref[...]) * y
    o_ref[...] = _layer_norm(r, lw_ref[...], lb_ref[...])


def _outproj(ys, w_bf16, x, gate, ln_w, ln_b):
    s, d = x.shape
    tm = 512
    row = lambda i: (i, 0)
    const = lambda i: (0, 0)
    vec = pl.BlockSpec((1, d), const)
    grp = pl.BlockSpec((tm, GROUP_W), row)
    return pl.pallas_call(
        _outproj_kernel,
        grid=(s // tm,),
        in_specs=[grp, grp, grp, grp, pl.BlockSpec((4 * GROUP_W, d), const), pl.BlockSpec((tm, d), row),
                  vec, vec, vec],
        out_specs=pl.BlockSpec((tm, d), row),
        out_shape=jax.ShapeDtypeStruct((s, d), F32),
        compiler_params=_cparams(1),
        name="outproj_ln",
    )(*ys, w_bf16, x, gate, ln_w, ln_b)


FFN_CHUNK = 256


def _ffn_kernel(x_ref, sc_ref, sh_ref, wu_ref, wd_ref, g_ref, lw_ref, lb_ref, o_ref):
    x = x_ref[...]
    h = (x * (1.0 + sc_ref[...]) + sh_ref[...]).astype(BF16)
    acc = jnp.zeros(x.shape, F32)
    for c0 in range(0, FFN_DIM, FFN_CHUNK):
        gate = _dot(h, wu_ref[:, c0:c0 + FFN_CHUNK])
        up = _dot(h, wu_ref[:, FFN_DIM + c0:FFN_DIM + c0 + FFN_CHUNK])
        acc = acc + _dot((_silu(gate) * up).astype(BF16), wd_ref[c0:c0 + FFN_CHUNK, :])
    r = DEEPNORM_ALPHA * x + (1.0 + g_ref[...]) * acc
    o_ref[...] = _layer_norm(r, lw_ref[...], lb_ref[...])


def _dense_ffn(x, scale, shift, wu_bf16, wd_bf16, gate, ln_w, ln_b):
    s, d = x.shape
    tm = 512
    row = lambda i: (i, 0)
    const = lambda i: (0, 0)
    vec = pl.BlockSpec((1, d), const)
    return pl.pallas_call(
        _ffn_kernel,
        grid=(s // tm,),
        in_specs=[pl.BlockSpec((tm, d), row), vec, vec,
                  pl.BlockSpec((d, 2 * FFN_DIM), const), pl.BlockSpec((FFN_DIM, d), const), vec, vec, vec],
        out_specs=pl.BlockSpec((tm, d), row),
        out_shape=jax.ShapeDtypeStruct((s, d), F32),
        compiler_params=_cparams(1),
        name="dense_ffn_ln",
    )(x, scale, shift, wu_bf16, wd_bf16, gate, ln_w, ln_b)


def _router_kernel(x_ref, sc_ref, sh_ref, w_ref, h_ref, e_ref, p_ref):
    h = x_ref[...] * (1.0 + sc_ref[...]) + sh_ref[...]
    h_ref[...] = h
    h_hi, h_lo = _split_bf16(h)
    w_hi, w_lo = _split_bf16(w_ref[...])
    logits = _dot(h_hi, w_hi) + _dot(h_hi, w_lo) + _dot(h_lo, w_hi)
    lane = lax.broadcasted_iota(jnp.int32, logits.shape, 1)
    logits = jnp.where(lane < N_EXPERTS, logits, NEG_INF)
    m1 = jnp.max(logits, axis=1, keepdims=True)
    e1 = jnp.min(jnp.where(logits == m1, lane, LANES), axis=1, keepdims=True)
    rest = jnp.where(lane == e1, NEG_INF, logits)
    m2 = jnp.max(rest, axis=1, keepdims=True)
    e2 = jnp.min(jnp.where(rest == m2, lane, LANES), axis=1, keepdims=True)
    t = jnp.exp(m2 - m1)
    g1 = 1.0 / (1.0 + t)
    e_ref[...] = jnp.where(lane == 0, e1, jnp.where(lane == 1, e2, 0))
    p_ref[...] = jnp.where(lane == 0, g1, jnp.where(lane == 1, t * g1, 0.0))


def _router(x, scale, shift, w_router):
    s, d = x.shape
    tm = 512
    row = lambda i: (i, 0)
    const = lambda i: (0, 0)
    vec = pl.BlockSpec((1, d), const)
    wpad = jnp.pad(w_router, ((0, 0), (0, LANES - N_EXPERTS)))
    return pl.pallas_call(
        _router_kernel,
        grid=(s // tm,),
        in_specs=[pl.BlockSpec((tm, d), row), vec, vec, pl.BlockSpec((d, LANES), const)],
        out_specs=[pl.BlockSpec((tm, d), row), pl.BlockSpec((tm, LANES), row), pl.BlockSpec((tm, LANES), row)],
        out_shape=[jax.ShapeDtypeStruct((s, d), F32), jax.ShapeDtypeStruct((s, LANES), jnp.int32),
                   jax.ShapeDtypeStruct((s, LANES), F32)],
        compiler_params=_cparams(1),
        name="moe_router",
    )(x, scale, shift, wpad)


MOE_TILE = 512
MOE_CHUNK = 896
ROW_UNROLL = 8


def _expert_kernel(te_ref, nu_ref, tokc_ref, tokn_ref, dst_ref, dstp_ref, h_hbm, wg_ref, wu_ref, wd_ref, y_hbm,
                   xbuf, xb16, acc, gsem, ssem):
    i, f = pl.program_id(0), pl.program_id(1)
    nt, nf = pl.num_programs(0), pl.num_programs(1)
    n_used = nu_ref[0]
    slot = i & 1

    def rows_loop(fn):
        def body(r, carry):
            fn(r)
            return carry
        lax.fori_loop(0, MOE_TILE, body, 0, unroll=ROW_UNROLL)

    def gather(ids_ref, s, r):
        return pltpu.make_async_copy(h_hbm.at[pl.ds(ids_ref[0, r], 1)], xbuf.at[s, pl.ds(r, 1)], gsem.at[s])

    def scatter_rows(ids_ref, s, wait):
        def one(r):
            row = ids_ref[0, r]

            @pl.when(row >= 0)
            def _():
                copy = pltpu.make_async_copy(acc.at[s, pl.ds(r, 1)], y_hbm.at[pl.ds(row, 1)], ssem.at[s])
                copy.wait() if wait else copy.start()
        rows_loop(one)

    @pl.when(f == 0)
    def _():
        @pl.when(i == 0)
        def _():
            rows_loop(lambda r: gather(tokc_ref, 0, r).start())

        @pl.when(i + 1 < n_used)
        def _():
            rows_loop(lambda r: gather(tokn_ref, 1 - slot, r).start())

        @pl.when(i < n_used)
        def _():
            rows_loop(lambda r: gather(tokc_ref, slot, r).wait())
            xb16[...] = xbuf[slot].astype(BF16)

    @pl.when(i < n_used)
    def _():
        x = xb16[...]
        act = _silu(_dot(x, wg_ref[...])) * _dot(x, wu_ref[...])
        part = _dot(act.astype(BF16), wd_ref[...])

        @pl.when(f == 0)
        def _():
            acc[slot] = part

        @pl.when(f > 0)
        def _():
            acc[slot] += part

    @pl.when(f == nf - 1)
    def _():
        @pl.when((i >= 1) & (i - 1 < n_used))
        def _():
            scatter_rows(dstp_ref, 1 - slot, wait=True)

        @pl.when(i < n_used)
        def _():
            scatter_rows(dst_ref, slot, wait=False)

        @pl.when((i == nt - 1) & (i < n_used))
        def _():
            scatter_rows(dst_ref, slot, wait=True)


def _experts(h, slot_tok, slot_dst, tile_expert, n_used, wu_bf16, wd_bf16, n_rows_out):
    t, d = h.shape
    n_tiles = slot_tok.shape[0] // MOE_TILE
    nf = EXPERT_DIM // MOE_CHUNK
    fidx = lambda i, f, nu: jnp.where(i < nu[0], f, nf - 1)
    ids = lambda fn: pl.BlockSpec((None, 1, MOE_TILE), lambda i, f, te, nu: (fn(i), 0, 0), memory_space=pltpu.SMEM)
    grid_spec = pltpu.PrefetchScalarGridSpec(
        num_scalar_prefetch=2,
        grid=(n_tiles, nf),
        in_specs=[ids(lambda i: i), ids(lambda i: jnp.minimum(i + 1, n_tiles - 1)),
                  ids(lambda i: i), ids(lambda i: jnp.maximum(i - 1, 0)),
                  pl.BlockSpec(memory_space=pl.ANY),
                  pl.BlockSpec((None, d, MOE_CHUNK), lambda i, f, te, nu: (te[i], 0, fidx(i, f, nu))),
                  pl.BlockSpec((None, d, MOE_CHUNK), lambda i, f, te, nu: (te[i], 0, nf + fidx(i, f, nu))),
                  pl.BlockSpec((None, MOE_CHUNK, d), lambda i, f, te, nu: (te[i], fidx(i, f, nu), 0))],
        out_specs=pl.BlockSpec(memory_space=pl.ANY),
        scratch_shapes=[pltpu.VMEM((2, MOE_TILE, d), F32), pltpu.VMEM((MOE_TILE, d), BF16),
                        pltpu.VMEM((2, MOE_TILE, d), F32),
                        pltpu.SemaphoreType.DMA((2,)), pltpu.SemaphoreType.DMA((2,))],
    )
    tok3 = slot_tok.reshape(n_tiles, 1, MOE_TILE)
    dst3 = slot_dst.reshape(n_tiles, 1, MOE_TILE)
    return pl.pallas_call(
        _expert_kernel,
        grid_spec=grid_spec,
        out_shape=jax.ShapeDtypeStruct((n_rows_out, d), F32),
        compiler_params=_cparams(2),
        name="moe_experts",
    )(tile_expert, n_used, tok3, tok3, dst3, dst3, h, wu_bf16, wu_bf16, wd_bf16)


def _combine_kernel(x_ref, y1_ref, y2_ref, p_ref, g_ref, lw_ref, lb_ref, o_ref):
    p = p_ref[...]
    f = p[:, 0:1] * y1_ref[...] + p[:, 1:2] * y2_ref[...]
    r = DEEPNORM_ALPHA * x_ref[...] + (1.0 + g_ref[...]) * f
    o_ref[...] = _layer_norm(r, lw_ref[...], lb_ref[...])


def _combine(x, y, probs, gate, ln_w, ln_b):
    s, d = x.shape
    tm = 512
    row = lambda i: (i, 0)
    vec = pl.BlockSpec((1, d), lambda i: (0, 0))
    big = pl.BlockSpec((tm, d), row)
    return pl.pallas_call(
        _combine_kernel,
        grid=(s // tm,),
        in_specs=[big, big, pl.BlockSpec((tm, d), lambda i: (s // tm + i, 0)),
                  pl.BlockSpec((tm, LANES), row), vec, vec, vec],
        out_specs=big,
        out_shape=jax.ShapeDtypeStruct((s, d), F32),
        compiler_params=_cparams(1),
        name="moe_combine_ln",
    )(x, y, y, probs, gate, ln_w, ln_b)


def _moe(x, scale, shift, w_router, wu_bf16, wd_bf16, gate, ln_w, ln_b):
    t, d = x.shape
    h, sel, probs = _router(x, scale, shift, w_router)
    flat_e = sel[:, :TOP_K].reshape(-1)
    n_assign = t * TOP_K
    n_tiles = -(-n_assign // MOE_TILE) + N_EXPERTS
    n_slots = n_tiles * MOE_TILE
    onehot = (flat_e[:, None] == jnp.arange(N_EXPERTS, dtype=jnp.int32)[None, :]).astype(jnp.int32)
    rank = jnp.sum((jnp.cumsum(onehot, axis=0) - onehot) * onehot, axis=1)
    counts = jnp.sum(onehot, axis=0)
    padded = (counts + MOE_TILE - 1) // MOE_TILE * MOE_TILE
    padded_end = jnp.cumsum(padded)
    slot = (padded_end - padded)[flat_e] + rank
    assign = jnp.full((n_slots,), -1, jnp.int32).at[slot].set(jnp.arange(n_assign, dtype=jnp.int32))
    used = assign >= 0
    slot_tok = jnp.where(used, assign >> 1, 0)
    slot_dst = jnp.where(used, (assign & 1) * t + (assign >> 1), -1)
    n_used = (padded_end[-1] // MOE_TILE).astype(jnp.int32).reshape(1)
    tile_ids = jnp.minimum(jnp.arange(n_tiles, dtype=jnp.int32), n_used[0] - 1)
    tile_expert = jnp.minimum(jnp.searchsorted(padded_end, tile_ids * MOE_TILE, side="right"),
                              N_EXPERTS - 1).astype(jnp.int32)
    y = _experts(h, slot_tok, slot_dst, tile_expert, n_used, wu_bf16, wd_bf16, n_assign)
    return _combine(x, y, probs, gate, ln_w, ln_b)


def kernel(x, c, positions, w_ada, b_ada, w_in, w_out, a_lower_bound, a_norm_w, b_conv_w, b_conv_b, b_norm_w,
           b_norm_b, d_sink, ln_w, ln_b, ffn_w_up, ffn_w_down, moe_router, moe_w_up, moe_w_down):
    batch, s, d = x.shape
    assert batch == 1 and d == D_MODEL
    x = x.reshape(s, d)
    mod = _adaln(c, w_ada, b_ada)
    rope = _rope_tables(positions)
    lb_cum = jnp.cumsum(jax.nn.softmax(a_lower_bound.astype(F32), axis=0), axis=0)
    lb_all = lb_cum - lb_cum[0]
    row = lambda v: v.reshape(1, -1)
    for layer in range(DEPTH):
        shift1, scale1, gate1, shift2, scale2, gate2 = [mod[layer, :, k * d:(k + 1) * d] for k in range(6)]
        proj = _inproj(x, scale1, shift1, w_in[layer].astype(BF16), rope)
        ya = _hgrn2(proj, row(lb_all[layer]), row(jnp.tile(a_norm_w[layer], GROUP_W // HEAD_DIM)))
        yb = _conv_module(proj, b_conv_w[layer], row(b_conv_b[layer]), row(b_norm_w[layer]), row(b_norm_b[layer]))
        yc = _dilated_attention(proj)
        yd = _window_attention(proj, d_sink[layer])
        x = _outproj((ya, yb, yc, yd), w_out[layer].astype(BF16), x, gate1, row(ln_w[layer, 0]), row(ln_b[layer, 0]))
        lw, lbias = row(ln_w[layer, 1]), row(ln_b[layer, 1])
        if layer % 2 == 0:
            x = _dense_ffn(x, scale2, shift2, ffn_w_up[layer // 2].astype(BF16),
                           ffn_w_down[layer // 2].astype(BF16), gate2, lw, lbias)
        else:
            x = _moe(x, scale2, shift2, moe_router[layer // 2], moe_w_up[layer // 2].astype(BF16),
                     moe_w_down[layer // 2].astype(BF16), gate2, lw, lbias)
    return x.reshape(batch, s, d)
```

```python
import functools

import numpy as np
import jax
import jax.numpy as jnp
from jax import lax
from jax.experimental import pallas as pl
from jax.experimental.pallas import tpu as pltpu

F32 = jnp.float32
BF16 = jnp.bfloat16

D_MODEL = 1024
DEPTH = 2
HEAD_DIM = 64
A_CHUNK = 16
B_KERNEL = 31
GROUP_W = 256
C_PATTERNS = ((128, 1), (512, 4), (2048, 16))
C_SIDE = 64
C_HALO = 1024
D_HALF_WINDOW = 128
ROPE_THETA = 500000.0
ROPE_DIM = HEAD_DIM // 4
FFN_DIM = 2816
N_EXPERTS = 8
TOP_K = 2
EXPERT_DIM = 3584
IN_WIDTH = 3072
DEEPNORM_ALPHA = (2 * DEPTH) ** 0.25
LN_EPS = 1e-5
RMS_EPS = 1e-6
NEG_INF = -1e30

LANES = 128
VMEM_LIMIT = 56 * 1024 * 1024


def _cparams(n_axes=1, vmem=VMEM_LIMIT):
    return pltpu.CompilerParams(dimension_semantics=("arbitrary",) * n_axes, vmem_limit_bytes=vmem)


def _sigmoid(x):
    return 1.0 / (1.0 + jnp.exp(-x))


def _silu(x):
    return x * _sigmoid(x)


def _split_bf16(x):
    hi = x.astype(BF16)
    lo = (x - hi.astype(F32)).astype(BF16)
    return hi, lo


def _layer_norm(r, w, b):
    mu = jnp.mean(r, axis=-1, keepdims=True)
    d = r - mu
    var = jnp.mean(d * d, axis=-1, keepdims=True)
    return d * lax.rsqrt(var + LN_EPS) * w + b


def _dot_nt(a, b):
    return lax.dot_general(a, b, (((1,), (1,)), ((), ())), preferred_element_type=F32)


def _dot_tn(a, b):
    return lax.dot_general(a, b, (((0,), (0,)), ((), ())), preferred_element_type=F32)


def _dot(a, b):
    return jnp.dot(a, b, preferred_element_type=F32)


def _adaln_kernel(c_ref, w_ref, b_ref, o_ref):
    a_hi, a_lo = _split_bf16(_silu(c_ref[...]))
    w_hi, w_lo = _split_bf16(w_ref[...])
    o_ref[...] = _dot(a_hi, w_hi) + _dot(a_hi, w_lo) + _dot(a_lo, w_hi) + b_ref[...]


def _adaln(c, w_ada, b_ada):
    depth, d, n = w_ada.shape
    tn = 1536
    c8 = jnp.broadcast_to(c.reshape(1, d), (8, d))
    out = pl.pallas_call(
        _adaln_kernel,
        grid=(depth, n // tn),
        in_specs=[pl.BlockSpec((8, d), lambda l, j: (0, 0)),
                  pl.BlockSpec((None, d, tn), lambda l, j: (l, 0, j)),
                  pl.BlockSpec((None, 1, tn), lambda l, j: (l, 0, j))],
        out_specs=pl.BlockSpec((None, 8, tn), lambda l, j: (l, 0, j)),
        out_shape=jax.ShapeDtypeStruct((depth, 8, n), F32),
        compiler_params=_cparams(2),
        name="adaln",
    )(c8, w_ada, b_ada.reshape(depth, 1, n))
    return out[:, 0:1, :]


def _rope_kernel(pos_ref, invf_ref, c_ref, s1_ref, s2_ref):
    ang = pos_ref[...].astype(F32) * invf_ref[...]
    j = lax.broadcasted_iota(jnp.int32, ang.shape, 1) & (HEAD_DIM - 1)
    cosv, sinv = jnp.cos(ang), jnp.sin(ang)
    half = ROPE_DIM // 2
    c_ref[...] = jnp.where(j < ROPE_DIM, cosv, 1.0)
    s1_ref[...] = jnp.where(j < half, -sinv, 0.0)
    s2_ref[...] = jnp.where((j >= half) & (j < ROPE_DIM), sinv, 0.0)


def _rope_tables(positions):
    s = positions.shape[-1]
    ts = 2048
    half = ROPE_DIM // 2
    inv_freq = np.float32(ROPE_THETA) ** (-np.arange(half, dtype=np.float32) * np.float32(2.0 / ROPE_DIM))
    invf = jnp.asarray(inv_freq[(np.arange(LANES) % HEAD_DIM) % half].reshape(1, LANES).astype(np.float32))
    spec = pl.BlockSpec((ts, LANES), lambda i: (i, 0))
    return pl.pallas_call(
        _rope_kernel,
        grid=(s // ts,),
        in_specs=[pl.BlockSpec((ts, 1), lambda i: (i, 0)), pl.BlockSpec((1, LANES), lambda i: (0, 0))],
        out_specs=[spec, spec, spec],
        out_shape=[jax.ShapeDtypeStruct((s, LANES), F32)] * 3,
        compiler_params=_cparams(1),
        name="rope_tables",
    )(positions.reshape(s, 1), invf)


_ROPE_COL_CHUNKS = (1792, 1920, 2048, 2176, 2560, 2688, 2816)


def _inproj_kernel(x_ref, sc_ref, sh_ref, w_ref, rc_ref, rs1_ref, rs2_ref, o_ref):
    h = (x_ref[...] * (1.0 + sc_ref[...]) + sh_ref[...]).astype(BF16)
    half = ROPE_DIM // 2
    for c0 in range(0, IN_WIDTH, GROUP_W):
        acc = _dot(h, w_ref[:, c0:c0 + GROUP_W])
        for k in range(GROUP_W // LANES):
            col = c0 + k * LANES
            part = acc[:, k * LANES:(k + 1) * LANES]
            if col in _ROPE_COL_CHUNKS:
                part = (part * rc_ref[...] + pltpu.roll(part, LANES - half, 1) * rs1_ref[...]
                        + pltpu.roll(part, half, 1) * rs2_ref[...])
            o_ref[:, col:col + LANES] = part


def _inproj(x, scale, shift, w_bf16, rope):
    s, d = x.shape
    tm = 512
    row = lambda i: (i, 0)
    const = lambda i: (0, 0)
    return pl.pallas_call(
        _inproj_kernel,
        grid=(s // tm,),
        in_specs=[pl.BlockSpec((tm, d), row), pl.BlockSpec((1, d), const), pl.BlockSpec((1, d), const),
                  pl.BlockSpec((d, IN_WIDTH), const),
                  pl.BlockSpec((tm, LANES), row), pl.BlockSpec((tm, LANES), row), pl.BlockSpec((tm, LANES), row)],
        out_specs=pl.BlockSpec((tm, IN_WIDTH), row),
        out_shape=jax.ShapeDtypeStruct((s, IN_WIDTH), F32),
        compiler_params=_cparams(1),
        name="inproj",
    )(x, scale, shift, w_bf16, *rope)


HG_TILE = 512
HG_CHUNKS = HG_TILE // A_CHUNK
N_HEADS = GROUP_W // HEAD_DIM


def _same_head_ones():
    li = lax.broadcasted_iota(jnp.int32, (GROUP_W, GROUP_W), 0)
    lj = lax.broadcasted_iota(jnp.int32, (GROUP_W, GROUP_W), 1)
    return jnp.where((li >> 6) == (lj >> 6), 1.0, 0.0).astype(BF16)


def _hgrn_core(q_ref, z_ref, v_ref, lb_ref, st_ref, qs, zs, vs, qt, kt, ot, oi_ref, rev):
    nb = HG_CHUNKS
    lb = lb_ref[...]
    for src, dst in ((q_ref, qs), (z_ref, zs), (v_ref, vs)):
        val = src[...]
        dst[0] = val[:, :LANES]
        dst[1] = val[:, LANES:]

    def by_pos(buf, j):
        sel = pl.ds(j, nb, stride=A_CHUNK)
        return jnp.concatenate([buf[0, sel, :], buf[1, sel, :]], axis=1)

    order = list(range(A_CHUNK))[::-1] if rev else list(range(A_CHUNK))
    q, f, k, v = [], [], [], []
    for j in order:
        aq, z = by_pos(qs, j), by_pos(zs, j)
        q.append(_silu(aq))
        f.append(lb + (1.0 - lb) * _sigmoid(z))
        k.append((1.0 - lb) * _sigmoid(-z))
        v.append(by_pos(vs, j))
    incl = [f[0]]
    for p in range(1, A_CHUNK):
        incl.append(incl[-1] * f[p])
    rest = [None] * A_CHUNK
    for p in range(A_CHUNK - 2, -1, -1):
        rest[p] = f[p + 1] if rest[p + 1] is None else rest[p + 1] * f[p + 1]
    dec = incl[-1]

    head = lax.broadcasted_iota(jnp.int32, (1, GROUP_W), 1) >> 6
    lo = lax.broadcasted_iota(jnp.int32, (1, LANES), 1) < HEAD_DIM
    ones_bd = _same_head_ones()

    def put(dst, sel, val):
        dst[0, sel, :] = val[:, :LANES]
        dst[1, sel, :] = val[:, LANES:]

    for p, j in enumerate(order):
        xs, decay = [], None
        for d in range(p + 1):
            x = q[p] * k[p - d]
            xs.append((x if decay is None else x * decay).astype(BF16))
            decay = f[p - d] if decay is None else decay * f[p - d]
        sc = _dot(jnp.concatenate(xs, axis=0), ones_bd)
        o = sc[0:nb] * v[p]
        for d in range(1, p + 1):
            o = o + sc[d * nb:(d + 1) * nb] * v[p - d]
        sel = pl.ds(j, nb, stride=A_CHUNK)
        put(ot, sel, o)
        put(qt, sel, q[p] * incl[p])
        put(kt, sel, k[p] if rest[p] is None else k[p] * rest[p])

    def per_head(x):
        return jnp.concatenate([jnp.where(head == h, x, 0.0) for h in range(N_HEADS)], axis=0).astype(BF16)

    chunks = range(nb - 1, -1, -1) if rev else range(nb)
    for n in chunks:
        r1 = slice(n * A_CHUNK, (n + 1) * A_CHUNK)
        lq = per_head(jnp.concatenate([qt[0, r1, :], qt[1, r1, :]], axis=1))
        rk = per_head(jnp.concatenate([kt[0, r1, :], kt[1, r1, :]], axis=1))
        va, vb = v_ref[r1, :LANES], v_ref[r1, LANES:]
        lv = jnp.concatenate([va, pltpu.roll(va, HEAD_DIM, 1), vb, pltpu.roll(vb, HEAD_DIM, 1)], axis=0).astype(BF16)
        state = st_ref[...]
        sb = state.astype(BF16)
        res = _dot_nt(lq, jnp.concatenate([sb, sb], axis=0))
        oi_ref[r1, :LANES] = jnp.where(lo, res[0:A_CHUNK], res[A_CHUNK:2 * A_CHUNK])
        oi_ref[r1, LANES:] = jnp.where(lo, res[2 * A_CHUNK:3 * A_CHUNK], res[3 * A_CHUNK:])
        kv = _dot_tn(lv, rk)[:HEAD_DIM]
        st_ref[...] = state * dec[n:n + 1] + kv
    full = slice(0, HG_TILE)
    return jnp.concatenate([ot[0, full, :], ot[1, full, :]], axis=1) + oi_ref[...]


def _hgrn_fwd_kernel(q_ref, f_ref, i_ref, lb_ref, o_ref, st_ref, *scratch):
    @pl.when(pl.program_id(0) == 0)
    def _():
        st_ref[...] = jnp.zeros_like(st_ref)

    o_ref[...] = _hgrn_core(q_ref, f_ref, i_ref, lb_ref, st_ref, *scratch, rev=False)


def _hgrn_bwd_kernel(q_ref, f_ref, i_ref, g_ref, of_ref, lb_ref, nw_ref, o_ref, st_ref, *scratch):
    @pl.when(pl.program_id(0) == 0)
    def _():
        st_ref[...] = jnp.zeros_like(st_ref)

    o = of_ref[...] + _hgrn_core(q_ref, f_ref, i_ref, lb_ref, st_ref, *scratch, rev=True)
    ones_bd = _same_head_ones()
    sq_hi, sq_lo = _split_bf16(o * o)
    ms = (_dot(sq_hi, ones_bd) + _dot(sq_lo, ones_bd)) * (1.0 / HEAD_DIM)
    o_ref[...] = o * lax.rsqrt(ms + RMS_EPS) * nw_ref[...] * _silu(g_ref[...])


def _hgrn_scratch():
    split = lambda rows: pltpu.VMEM((2, rows, LANES), F32)
    return [pltpu.VMEM((HEAD_DIM, GROUP_W), F32),
            split(HG_TILE), split(HG_TILE), split(HG_TILE),
            split(HG_TILE), split(HG_TILE), split(HG_TILE), pltpu.VMEM((HG_TILE, GROUP_W), F32)]


def _hgrn2(proj, lb, norm_w):
    s = proj.shape[0]
    ts = HG_TILE
    n = s // ts
    col = lambda j: pl.BlockSpec((ts, GROUP_W), lambda i: (i, j))
    rcol = lambda j: pl.BlockSpec((ts, GROUP_W), lambda i: (n - 1 - i, j))
    vec = pl.BlockSpec((1, GROUP_W), lambda i: (0, 0))
    o_f = pl.pallas_call(
        _hgrn_fwd_kernel,
        grid=(n,),
        in_specs=[col(0), col(1), col(3), vec],
        out_specs=pl.BlockSpec((ts, GROUP_W), lambda i: (i, 0)),
        out_shape=jax.ShapeDtypeStruct((s, GROUP_W), F32),
        scratch_shapes=_hgrn_scratch(),
        compiler_params=_cparams(1),
        name="hgrn_fwd",
    )(proj, proj, proj, lb)
    return pl.pallas_call(
        _hgrn_bwd_kernel,
        grid=(n,),
        in_specs=[rcol(0), rcol(2), rcol(3), rcol(4), pl.BlockSpec((ts, GROUP_W), lambda i: (n - 1 - i, 0)),
                  vec, vec],
        out_specs=pl.BlockSpec((ts, GROUP_W), lambda i: (n - 1 - i, 0)),
        out_shape=jax.ShapeDtypeStruct((s, GROUP_W), F32),
        scratch_shapes=_hgrn_scratch(),
        compiler_params=_cparams(1),
        name="hgrn_bwd",
    )(proj, proj, proj, proj, o_f, lb, norm_w)


B_PAD = 16


def _conv_kernel(v_ref, g_ref, vp_ref, gp_ref, vn_ref, gn_ref, w_ref, b_ref, nw_ref, nb_ref, o_ref, ubuf):
    i = pl.program_id(0)
    ts = v_ref.shape[0]
    glu = lambda a, b: a * _sigmoid(b)
    ubuf[0:B_PAD, :] = jnp.where(i > 0, glu(vp_ref[...], gp_ref[...]), 0.0)
    ubuf[B_PAD:B_PAD + ts, :] = glu(v_ref[...], g_ref[...])
    ubuf[B_PAD + ts:, :] = jnp.where(i < pl.num_programs(0) - 1, glu(vn_ref[...], gn_ref[...]), 0.0)
    acc = jnp.zeros((ts, GROUP_W), F32) + b_ref[...]
    for j in range(B_KERNEL):
        acc = acc + ubuf[pl.ds(B_PAD - B_KERNEL // 2 + j, ts), :] * w_ref[j:j + 1, :]
    o_ref[...] = _silu(_layer_norm(acc, nw_ref[...], nb_ref[...]))


def _conv_module(proj, conv_w, conv_b, norm_w, norm_b):
    s = proj.shape[0]
    ts = 512
    n = s // ts
    per = ts // B_PAD
    last = s // B_PAD - 1
    cur = lambda j: pl.BlockSpec((ts, GROUP_W), lambda i: (i, j))
    prev = lambda j: pl.BlockSpec((B_PAD, GROUP_W), lambda i: (jnp.maximum(i * per - 1, 0), j))
    nxt = lambda j: pl.BlockSpec((B_PAD, GROUP_W), lambda i: (jnp.minimum((i + 1) * per, last), j))
    vec = pl.BlockSpec((1, GROUP_W), lambda i: (0, 0))
    wpad = jnp.pad(conv_w, ((0, 1), (0, 0)))
    return pl.pallas_call(
        _conv_kernel,
        grid=(n,),
        in_specs=[cur(5), cur(6), prev(5), prev(6), nxt(5), nxt(6),
                  pl.BlockSpec((B_KERNEL + 1, GROUP_W), lambda i: (0, 0)), vec, vec, vec],
        out_specs=pl.BlockSpec((ts, GROUP_W), lambda i: (i, 0)),
        out_shape=jax.ShapeDtypeStruct((s, GROUP_W), F32),
        scratch_shapes=[pltpu.VMEM((ts + 2 * B_PAD, GROUP_W), F32)],
        compiler_params=_cparams(1),
        name="conv_module",
    )(proj, proj, proj, proj, proj, proj, wpad, conv_b, norm_w, norm_b)


def _dilated_kernel(q_ref, kp_ref, kc_ref, kn_ref, vp_ref, vc_ref, vn_ref, o_ref, qw, kw, vw, ob, mb, lb, *, seq, span):
    i = pl.program_id(0)

    def put(dst, row0, val):
        n = val.shape[0]
        dst[0, row0:row0 + n, :] = val[:, :LANES]
        dst[1, row0:row0 + n, :] = val[:, LANES:]

    put(qw, 0, q_ref[...] * (HEAD_DIM ** -0.5))
    for dst, refs in ((kw, (kp_ref, kc_ref, kn_ref)), (vw, (vp_ref, vc_ref, vn_ref))):
        put(dst, 0, refs[0][...])
        put(dst, C_HALO, refs[1][...])
        put(dst, C_HALO + span, refs[2][...])

    head = lax.broadcasted_iota(jnp.int32, (1, GROUP_W), 1) >> 6
    heads = [head == h for h in range(4)]

    def rows(start, size, dil):
        return pl.ds(start, size) if dil == 1 else pl.ds(start, size, stride=dil)

    def load(src, sel):
        return jnp.concatenate([src[0, sel, :], src[1, sel, :]], axis=1)

    def store(dst, sel, val):
        dst[0, sel, :] = val[:, :LANES]
        dst[1, sel, :] = val[:, LANES:]

    def by_head(x, nq):
        out = jnp.where(heads[0], x[0:nq], 0.0)
        for h in range(1, 4):
            out = out + jnp.where(heads[h], x[h * nq:(h + 1) * nq], 0.0)
        return out

    def attend(first, r, dil, blk, nq):
        nk = nq + 2 * C_SIDE
        qrows = rows(r + dil * blk * nq, nq, dil)
        koff = C_HALO + r + dil * (blk * nq - C_SIDE)
        qs = load(qw, qrows)
        ks = load(kw, rows(koff, nk, dil)).astype(BF16)
        vs = load(vw, rows(koff, nk, dil)).astype(BF16)
        lhs = jnp.concatenate([jnp.where(hm, qs, 0.0) for hm in heads], axis=0).astype(BF16)
        sc = _dot_nt(lhs, ks)
        a = lax.broadcasted_iota(jnp.int32, sc.shape, 0) & (nq - 1)
        b = lax.broadcasted_iota(jnp.int32, sc.shape, 1)
        kpos = i * span + (koff - C_HALO) + dil * b
        mask = (jnp.abs(a - (b - C_SIDE)) <= C_SIDE) & (kpos >= 0) & (kpos < seq)
        sc = jnp.where(mask, sc, NEG_INF)
        m = jnp.max(sc, axis=1, keepdims=True)
        p = jnp.exp(sc - m)
        l = jnp.sum(p, axis=1, keepdims=True)
        o_new = by_head(_dot(p.astype(BF16), vs), nq)
        m_new = by_head(jnp.broadcast_to(m, (4 * nq, GROUP_W)), nq)
        l_new = by_head(jnp.broadcast_to(l, (4 * nq, GROUP_W)), nq)
        if first:
            store(ob, qrows, o_new)
            store(mb, qrows, m_new)
            store(lb, qrows, l_new)
        else:
            m_old = load(mb, qrows)
            m_tot = jnp.maximum(m_old, m_new)
            w_old, w_new = jnp.exp(m_old - m_tot), jnp.exp(m_new - m_tot)
            store(ob, qrows, w_old * load(ob, qrows) + w_new * o_new)
            store(lb, qrows, w_old * load(lb, qrows) + w_new * l_new)
            store(mb, qrows, m_tot)

    for pat, (_, dil) in enumerate(C_PATTERNS):
        sub = span // dil
        nq = min(sub, 256)
        for r in range(dil):
            for blk in range(sub // nq):
                attend(pat == 0, r, dil, blk, nq)
    full = slice(0, span)
    o_ref[...] = load(ob, full) / load(lb, full)


def _dilated_attention(proj):
    s = proj.shape[0]
    span = 1024
    n = s // span
    per = span // C_HALO
    last = s // C_HALO - 1
    cur = lambda j: pl.BlockSpec((span, GROUP_W), lambda i: (i, j))
    prev = lambda j: pl.BlockSpec((C_HALO, GROUP_W), lambda i: (jnp.maximum(i * per - 1, 0), j))
    nxt = lambda j: pl.BlockSpec((C_HALO, GROUP_W), lambda i: (jnp.minimum((i + 1) * per, last), j))
    win = span + 2 * C_HALO
    return pl.pallas_call(
        functools.partial(_dilated_kernel, seq=s, span=span),
        grid=(n,),
        in_specs=[cur(7), prev(8), cur(8), nxt(8), prev(9), cur(9), nxt(9)],
        out_specs=pl.BlockSpec((span, GROUP_W), lambda i: (i, 0)),
        out_shape=jax.ShapeDtypeStruct((s, GROUP_W), F32),
        scratch_shapes=[pltpu.VMEM((2, span, LANES), F32),
                        pltpu.VMEM((2, win, LANES), F32), pltpu.VMEM((2, win, LANES), F32),
                        pltpu.VMEM((2, span, LANES), F32), pltpu.VMEM((2, span, LANES), F32),
                        pltpu.VMEM((2, span, LANES), F32)],
        compiler_params=_cparams(1),
        name="dilated_attention",
    )(proj, proj, proj, proj, proj, proj, proj)


def _window_kernel(sink_ref, q_ref, kp_ref, kc_ref, kn_ref, vp_ref, vc_ref, vn_ref, o_ref, *, seq):
    i = pl.program_id(0)
    tq = q_ref.shape[0]
    hw = D_HALF_WINDOW
    nk = tq + 2 * hw
    kwin = jnp.concatenate([kp_ref[...], kc_ref[...], kn_ref[...]], axis=0).astype(BF16)
    vwin = jnp.concatenate([vp_ref[...], vc_ref[...], vn_ref[...]], axis=0).astype(BF16)
    lo = lax.broadcasted_iota(jnp.int32, (1, LANES), 1) < HEAD_DIM
    q = q_ref[...] * (HEAD_DIM ** -0.5)
    qa, qb = q[:, :LANES], q[:, LANES:]
    swap = lambda x: pltpu.roll(x, HEAD_DIM, 1)
    lhs = (jnp.concatenate([jnp.where(lo, qa, 0.0), jnp.where(lo, swap(qa), 0.0)], axis=0),
           jnp.concatenate([jnp.where(lo, 0.0, swap(qb)), jnp.where(lo, 0.0, qb)], axis=0))
    a = lax.broadcasted_iota(jnp.int32, (2 * tq, nk), 0) & (tq - 1)
    b = lax.broadcasted_iota(jnp.int32, (2 * tq, nk), 1)
    kpos = i * tq - hw + b
    mask = (jnp.abs(b - hw - a) <= hw) & (kpos >= 0) & (kpos < seq)
    second = lax.broadcasted_iota(jnp.int32, (2 * tq, 1), 0) >= tq
    outs = []
    for g in range(2):
        sc = jnp.where(mask, _dot_nt(lhs[g].astype(BF16), kwin), NEG_INF)
        m = jnp.max(sc, axis=1, keepdims=True)
        p = jnp.exp(sc - m)
        l = jnp.sum(p, axis=1, keepdims=True)
        o = _dot(p.astype(BF16), vwin)
        sk = jnp.where(second, sink_ref[2 * g + 1], sink_ref[2 * g])
        m_tot = jnp.maximum(m, sk)
        w = jnp.exp(m - m_tot)
        den = l * w + jnp.exp(sk - m_tot)
        outs.append(o * (w / den))
    o_ref[:, :LANES] = jnp.where(lo, outs[0][:tq], swap(outs[0][tq:]))
    o_ref[:, LANES:] = jnp.where(lo, swap(outs[1][:tq]), outs[1][tq:])


def _window_attention(proj, sink):
    s = proj.shape[0]
    tq = 256
    n = s // tq
    hw = D_HALF_WINDOW
    per = tq // hw
    last = s // hw - 1
    kcol, vcol = 2816 // LANES, 2944 // LANES
    cur = lambda j: pl.BlockSpec((tq, LANES), lambda i: (i, j))
    prev = lambda j: pl.BlockSpec((hw, LANES), lambda i: (jnp.maximum(i * per - 1, 0), j))
    nxt = lambda j: pl.BlockSpec((hw, LANES), lambda i: (jnp.minimum((i + 1) * per, last), j))
    return pl.pallas_call(
        functools.partial(_window_kernel, seq=s),
        grid=(n,),
        in_specs=[pl.BlockSpec(memory_space=pltpu.SMEM),
                  pl.BlockSpec((tq, GROUP_W), lambda i: (i, 10)),
                  prev(kcol), cur(kcol), nxt(kcol), prev(vcol), cur(vcol), nxt(vcol)],
        out_specs=pl.BlockSpec((tq, GROUP_W), lambda i: (i, 0)),
        out_shape=jax.ShapeDtypeStruct((s, GROUP_W), F32),
        compiler_params=_cparams(1),
        name="window_attention",
    )(sink, proj, proj, proj, proj, proj, proj, proj)


def _outproj_kernel(ya_ref, yb_ref, yc_ref, yd_ref, w_ref, x_ref, g_ref, lw_ref, lb_ref, o_ref):
    y = None
    for k, ref in enumerate((ya_ref, yb_ref, yc_ref, yd_ref)):
        part = _dot(ref[...].astype(BF16), w_ref[k * GROUP_W:(k + 1) * GROUP_W, :])
        y = part if y is None else y + part
    r = DEEPNORM_ALPHA * x_ref[...] + (1.0 + g_ref[...]) * y
    o_ref[...] = _layer_norm(r, lw_ref[...], lb_ref[...])


def _outproj(ys, w_bf16, x, gate, ln_w, ln_b):
    s, d = x.shape
    tm = 512
    row = lambda i: (i, 0)
    const = lambda i: (0, 0)
    vec = pl.BlockSpec((1, d), const)
    grp = pl.BlockSpec((tm, GROUP_W), row)
    return pl.pallas_call(
        _outproj_kernel,
        grid=(s // tm,),
        in_specs=[grp, grp, grp, grp, pl.BlockSpec((4 * GROUP_W, d), const), pl.BlockSpec((tm, d), row),
                  vec, vec, vec],
        out_specs=pl.BlockSpec((tm, d), row),
        out_shape=jax.ShapeDtypeStruct((s, d), F32),
        compiler_params=_cparams(1),
        name="outproj_ln",
    )(*ys, w_bf16, x, gate, ln_w, ln_b)


FFN_CHUNK = 256


def _ffn_kernel(x_ref, sc_ref, sh_ref, wu_ref, wd_ref, g_ref, lw_ref, lb_ref, o_ref):
    x = x_ref[...]
    h = (x * (1.0 + sc_ref[...]) + sh_ref[...]).astype(BF16)
    acc = jnp.zeros(x.shape, F32)
    for c0 in range(0, FFN_DIM, FFN_CHUNK):
        gate = _dot(h, wu_ref[:, c0:c0 + FFN_CHUNK])
        up = _dot(h, wu_ref[:, FFN_DIM + c0:FFN_DIM + c0 + FFN_CHUNK])
        acc = acc + _dot((_silu(gate) * up).astype(BF16), wd_ref[c0:c0 + FFN_CHUNK, :])
    r = DEEPNORM_ALPHA * x + (1.0 + g_ref[...]) * acc
    o_ref[...] = _layer_norm(r, lw_ref[...], lb_ref[...])


def _dense_ffn(x, scale, shift, wu_bf16, wd_bf16, gate, ln_w, ln_b):
    s, d = x.shape
    tm = 512
    row = lambda i: (i, 0)
    const = lambda i: (0, 0)
    vec = pl.BlockSpec((1, d), const)
    return pl.pallas_call(
        _ffn_kernel,
        grid=(s // tm,),
        in_specs=[pl.BlockSpec((tm, d), row), vec, vec,
                  pl.BlockSpec((d, 2 * FFN_DIM), const), pl.BlockSpec((FFN_DIM, d), const), vec, vec, vec],
        out_specs=pl.BlockSpec((tm, d), row),
        out_shape=jax.ShapeDtypeStruct((s, d), F32),
        compiler_params=_cparams(1),
        name="dense_ffn_ln",
    )(x, scale, shift, wu_bf16, wd_bf16, gate, ln_w, ln_b)


def _router_kernel(x_ref, sc_ref, sh_ref, w_ref, h_ref, e_ref, p_ref):
    h = x_ref[...] * (1.0 + sc_ref[...]) + sh_ref[...]
    h_ref[...] = h
    h_hi, h_lo = _split_bf16(h)
    w_hi, w_lo = _split_bf16(w_ref[...])
    logits = _dot(h_hi, w_hi) + _dot(h_hi, w_lo) + _dot(h_lo, w_hi)
    lane = lax.broadcasted_iota(jnp.int32, logits.shape, 1)
    logits = jnp.where(lane < N_EXPERTS, logits, NEG_INF)
    m1 = jnp.max(logits, axis=1, keepdims=True)
    e1 = jnp.min(jnp.where(logits == m1, lane, LANES), axis=1, keepdims=True)
    rest = jnp.where(lane == e1, NEG_INF, logits)
    m2 = jnp.max(rest, axis=1, keepdims=True)
    e2 = jnp.min(jnp.where(rest == m2, lane, LANES), axis=1, keepdims=True)
    t = jnp.exp(m2 - m1)
    g1 = 1.0 / (1.0 + t)
    e_ref[...] = jnp.where(lane == 0, e1, jnp.where(lane == 1, e2, 0))
    p_ref[...] = jnp.where(lane == 0, g1, jnp.where(lane == 1, t * g1, 0.0))


def _router(x, scale, shift, w_router):
    s, d = x.shape
    tm = 512
    row = lambda i: (i, 0)
    const = lambda i: (0, 0)
    vec = pl.BlockSpec((1, d), const)
    wpad = jnp.pad(w_router, ((0, 0), (0, LANES - N_EXPERTS)))
    return pl.pallas_call(
        _router_kernel,
        grid=(s // tm,),
        in_specs=[pl.BlockSpec((tm, d), row), vec, vec, pl.BlockSpec((d, LANES), const)],
        out_specs=[pl.BlockSpec((tm, d), row), pl.BlockSpec((tm, LANES), row), pl.BlockSpec((tm, LANES), row)],
        out_shape=[jax.ShapeDtypeStruct((s, d), F32), jax.ShapeDtypeStruct((s, LANES), jnp.int32),
                   jax.ShapeDtypeStruct((s, LANES), F32)],
        compiler_params=_cparams(1),
        name="moe_router",
    )(x, scale, shift, wpad)


MOE_TILE = 512
MOE_CHUNK = 896
ROW_UNROLL = 8


def _expert_kernel(te_ref, nu_ref, tokc_ref, tokn_ref, dst_ref, h_hbm, wg_ref, wu_ref, wd_ref, y_hbm,
                   xbuf, xb16, acc, gsem, ssem):
    i, f = pl.program_id(0), pl.program_id(1)
    nt, nf = pl.num_programs(0), pl.num_programs(1)
    n_used = nu_ref[0]
    slot = i & 1
    tail = y_hbm.shape[0] - MOE_TILE

    def rows_loop(fn):
        def body(g, carry):
            base = pl.multiple_of(g * ROW_UNROLL, ROW_UNROLL)
            for j in range(ROW_UNROLL):
                fn(base + j)
            return carry
        lax.fori_loop(0, MOE_TILE // ROW_UNROLL, body, 0)

    def gather_rows(ids_ref, s):
        rows_loop(lambda r: pltpu.make_async_copy(h_hbm.at[pl.ds(ids_ref[0, r], 1)], xbuf.at[s, pl.ds(r, 1)],
                                                  gsem.at[s]).start())

    def gather_wait(s):
        pltpu.make_async_copy(h_hbm.at[pl.ds(0, MOE_TILE)], xbuf.at[s], gsem.at[s]).wait()

    def scatter_rows(s):
        rows_loop(lambda r: pltpu.make_async_copy(acc.at[s, pl.ds(r, 1)], y_hbm.at[pl.ds(dst_ref[0, r], 1)],
                                                  ssem.at[s]).start())

    def scatter_wait(s):
        pltpu.make_async_copy(acc.at[s], y_hbm.at[pl.ds(0, MOE_TILE)], ssem.at[s]).wait()

    @pl.when(f == 0)
    def _():
        @pl.when(i == 0)
        def _():
            acc[1] = jnp.zeros(acc.shape[1:], F32)
            zero_tail = pltpu.make_async_copy(acc.at[1], y_hbm.at[pl.ds(tail, MOE_TILE)], ssem.at[1])
            zero_tail.start()
            zero_tail.wait()
            gather_rows(tokc_ref, 0)

        @pl.when(i + 1 < n_used)
        def _():
            gather_rows(tokn_ref, 1 - slot)

        @pl.when(i < n_used)
        def _():
            gather_wait(slot)
            xb16[...] = xbuf[slot].astype(BF16)

    @pl.when(i < n_used)
    def _():
        x = xb16[...]
        act = _silu(_dot(x, wg_ref[...])) * _dot(x, wu_ref[...])
        part = _dot(act.astype(BF16), wd_ref[...])

        @pl.when(f == 0)
        def _():
            acc[slot] = part

        @pl.when(f > 0)
        def _():
            acc[slot] += part

    @pl.when(f == nf - 1)
    def _():
        @pl.when((i >= 1) & (i - 1 < n_used))
        def _():
            scatter_wait(1 - slot)

        @pl.when(i < n_used)
        def _():
            scatter_rows(slot)

        @pl.when((i == nt - 1) & (i < n_used))
        def _():
            scatter_wait(slot)


def _experts(h, slot_tok, slot_dst, tile_expert, n_used, wu_bf16, wd_bf16, n_rows_out):
    t, d = h.shape
    n_tiles = slot_tok.shape[0] // MOE_TILE
    nf = EXPERT_DIM // MOE_CHUNK
    fidx = lambda i, f, nu: jnp.where(i < nu[0], f, nf - 1)
    ids = lambda fn: pl.BlockSpec((None, 1, MOE_TILE), lambda i, f, te, nu: (fn(i), 0, 0), memory_space=pltpu.SMEM)
    grid_spec = pltpu.PrefetchScalarGridSpec(
        num_scalar_prefetch=2,
        grid=(n_tiles, nf),
        in_specs=[ids(lambda i: i), ids(lambda i: jnp.minimum(i + 1, n_tiles - 1)), ids(lambda i: i),
                  pl.BlockSpec(memory_space=pl.ANY),
                  pl.BlockSpec((None, d, MOE_CHUNK), lambda i, f, te, nu: (te[i], 0, fidx(i, f, nu))),
                  pl.BlockSpec((None, d, MOE_CHUNK), lambda i, f, te, nu: (te[i], 0, nf + fidx(i, f, nu))),
                  pl.BlockSpec((None, MOE_CHUNK, d), lambda i, f, te, nu: (te[i], fidx(i, f, nu), 0))],
        out_specs=pl.BlockSpec(memory_space=pl.ANY),
        scratch_shapes=[pltpu.VMEM((2, MOE_TILE, d), F32), pltpu.VMEM((MOE_TILE, d), BF16),
                        pltpu.VMEM((2, MOE_TILE, d), F32),
                        pltpu.SemaphoreType.DMA((2,)), pltpu.SemaphoreType.DMA((2,))],
    )
    tok3 = slot_tok.reshape(n_tiles, 1, MOE_TILE)
    dst3 = slot_dst.reshape(n_tiles, 1, MOE_TILE)
    return pl.pallas_call(
        _expert_kernel,
        grid_spec=grid_spec,
        out_shape=jax.ShapeDtypeStruct((n_rows_out, d), F32),
        compiler_params=_cparams(2),
        name="moe_experts",
    )(tile_expert, n_used, tok3, tok3, dst3, h, wu_bf16, wu_bf16, wd_bf16)


def _combine_kernel(x_ref, y1_ref, y2_ref, p_ref, g_ref, lw_ref, lb_ref, o_ref):
    p = p_ref[...]
    f = p[:, 0:1] * y1_ref[...] + p[:, 1:2] * y2_ref[...]
    r = DEEPNORM_ALPHA * x_ref[...] + (1.0 + g_ref[...]) * f
    o_ref[...] = _layer_norm(r, lw_ref[...], lb_ref[...])


def _combine(x, y, probs, gate, ln_w, ln_b):
    s, d = x.shape
    tm = 512
    row = lambda i: (i, 0)
    vec = pl.BlockSpec((1, d), lambda i: (0, 0))
    big = pl.BlockSpec((tm, d), row)
    return pl.pallas_call(
        _combine_kernel,
        grid=(s // tm,),
        in_specs=[big, big, pl.BlockSpec((tm, d), lambda i: (s // tm + i, 0)),
                  pl.BlockSpec((tm, LANES), row), vec, vec, vec],
        out_specs=big,
        out_shape=jax.ShapeDtypeStruct((s, d), F32),
        compiler_params=_cparams(1),
        name="moe_combine_ln",
    )(x, y, y, probs, gate, ln_w, ln_b)


def _moe(x, scale, shift, w_router, wu_bf16, wd_bf16, gate, ln_w, ln_b):
    t, d = x.shape
    h, sel, probs = _router(x, scale, shift, w_router)
    flat_e = sel[:, :TOP_K].reshape(-1)
    n_assign = t * TOP_K
    n_tiles = -(-n_assign // MOE_TILE) + N_EXPERTS
    n_slots = n_tiles * MOE_TILE
    onehot = (flat_e[:, None] == jnp.arange(N_EXPERTS, dtype=jnp.int32)[None, :]).astype(jnp.int32)
    rank = jnp.sum((jnp.cumsum(onehot, axis=0) - onehot) * onehot, axis=1)
    counts = jnp.sum(onehot, axis=0)
    padded = (counts + MOE_TILE - 1) // MOE_TILE * MOE_TILE
    padded_end = jnp.cumsum(padded)
    slot = (padded_end - padded)[flat_e] + rank
    assign = jnp.full((n_slots,), -1, jnp.int32).at[slot].set(jnp.arange(n_assign, dtype=jnp.int32))
    used = assign >= 0
    slot_tok = jnp.where(used, assign >> 1, 0)
    slot_dst = jnp.where(used, (assign & 1) * t + (assign >> 1),
                         n_assign + (jnp.arange(n_slots, dtype=jnp.int32) & (MOE_TILE - 1)))
    n_used = (padded_end[-1] // MOE_TILE).astype(jnp.int32).reshape(1)
    tile_ids = jnp.minimum(jnp.arange(n_tiles, dtype=jnp.int32), n_used[0] - 1)
    tile_expert = jnp.minimum(jnp.searchsorted(padded_end, tile_ids * MOE_TILE, side="right"),
                              N_EXPERTS - 1).astype(jnp.int32)
    y = _experts(h, slot_tok, slot_dst, tile_expert, n_used, wu_bf16, wd_bf16, n_assign + MOE_TILE)
    return _combine(x, y, probs, gate, ln_w, ln_b)


def kernel(x, c, positions, w_ada, b_ada, w_in, w_out, a_lower_bound, a_norm_w, b_conv_w, b_conv_b, b_norm_w,
           b_norm_b, d_sink, ln_w, ln_b, ffn_w_up, ffn_w_down, moe_router, moe_w_up, moe_w_down):
    batch, s, d = x.shape
    assert batch == 1 and d == D_MODEL
    x = x.reshape(s, d)
    mod = _adaln(c, w_ada, b_ada)
    rope = _rope_tables(positions)
    lb_cum = jnp.cumsum(jax.nn.softmax(a_lower_bound.astype(F32), axis=0), axis=0)
    lb_all = lb_cum - lb_cum[0]
    row = lambda v: v.reshape(1, -1)
    for layer in range(DEPTH):
        shift1, scale1, gate1, shift2, scale2, gate2 = [mod[layer, :, k * d:(k + 1) * d] for k in range(6)]
        proj = _inproj(x, scale1, shift1, w_in[layer].astype(BF16), rope)
        ya = _hgrn2(proj, row(lb_all[layer]), row(jnp.tile(a_norm_w[layer], GROUP_W // HEAD_DIM)))
        yb = _conv_module(proj, b_conv_w[layer], row(b_conv_b[layer]), row(b_norm_w[layer]), row(b_norm_b[layer]))
        yc = _dilated_attention(proj)
        yd = _window_attention(proj, d_sink[layer])
        x = _outproj((ya, yb, yc, yd), w_out[layer].astype(BF16), x, gate1, row(ln_w[layer, 0]), row(ln_b[layer, 0]))
        lw, lbias = row(ln_w[layer, 1]), row(ln_b[layer, 1])
        if layer % 2 == 0:
            x = _dense_ffn(x, scale2, shift2, ffn_w_up[layer // 2].astype(BF16),
                           ffn_w_down[layer // 2].astype(BF16), gate2, lw, lbias)
        else:
            x = _moe(x, scale2, shift2, moe_router[layer // 2], moe_w_up[layer // 2].astype(BF16),
                     moe_w_down[layer // 2].astype(BF16), gate2, lw, lbias)
    return x.reshape(batch, s, d)
```

```python
import functools

import numpy as np
import jax
import jax.numpy as jnp
from jax import lax
from jax.experimental import pallas as pl
from jax.experimental.pallas import tpu as pltpu

F32 = jnp.float32
BF16 = jnp.bfloat16

D_MODEL = 1024
DEPTH = 2
HEAD_DIM = 64
A_CHUNK = 16
B_KERNEL = 31
GROUP_W = 256
C_PATTERNS = ((128, 1), (512, 4), (2048, 16))
C_SIDE = 64
C_HALO = 1024
D_HALF_WINDOW = 128
ROPE_THETA = 500000.0
ROPE_DIM = HEAD_DIM // 4
FFN_DIM = 2816
N_EXPERTS = 8
TOP_K = 2
EXPERT_DIM = 3584
IN_WIDTH = 3072
DEEPNORM_ALPHA = (2 * DEPTH) ** 0.25
LN_EPS = 1e-5
RMS_EPS = 1e-6
NEG_INF = -1e30

LANES = 128
VMEM_LIMIT = 56 * 1024 * 1024


def _cparams(n_axes=1, vmem=VMEM_LIMIT):
    return pltpu.CompilerParams(dimension_semantics=("arbitrary",) * n_axes, vmem_limit_bytes=vmem)


def _sigmoid(x):
    return 1.0 / (1.0 + jnp.exp(-x))


def _silu(x):
    return x * _sigmoid(x)


def _split_bf16(x):
    hi = x.astype(BF16)
    lo = (x - hi.astype(F32)).astype(BF16)
    return hi, lo


def _layer_norm(r, w, b):
    mu = jnp.mean(r, axis=-1, keepdims=True)
    d = r - mu
    var = jnp.mean(d * d, axis=-1, keepdims=True)
    return d * lax.rsqrt(var + LN_EPS) * w + b


def _dot_nt(a, b):
    return lax.dot_general(a, b, (((1,), (1,)), ((), ())), preferred_element_type=F32)


def _dot_tn(a, b):
    return lax.dot_general(a, b, (((0,), (0,)), ((), ())), preferred_element_type=F32)


def _dot(a, b):
    return jnp.dot(a, b, preferred_element_type=F32)


def _adaln_kernel(c_ref, w_ref, b_ref, o_ref):
    a_hi, a_lo = _split_bf16(_silu(c_ref[...]))
    w_hi, w_lo = _split_bf16(w_ref[...])
    o_ref[...] = _dot(a_hi, w_hi) + _dot(a_hi, w_lo) + _dot(a_lo, w_hi) + b_ref[...]


def _adaln(c, w_ada, b_ada):
    depth, d, n = w_ada.shape
    tn = 1536
    c8 = jnp.broadcast_to(c.reshape(1, d), (8, d))
    out = pl.pallas_call(
        _adaln_kernel,
        grid=(depth, n // tn),
        in_specs=[pl.BlockSpec((8, d), lambda l, j: (0, 0)),
                  pl.BlockSpec((None, d, tn), lambda l, j: (l, 0, j)),
                  pl.BlockSpec((None, 1, tn), lambda l, j: (l, 0, j))],
        out_specs=pl.BlockSpec((None, 8, tn), lambda l, j: (l, 0, j)),
        out_shape=jax.ShapeDtypeStruct((depth, 8, n), F32),
        compiler_params=_cparams(2),
        name="adaln",
    )(c8, w_ada, b_ada.reshape(depth, 1, n))
    return out[:, 0:1, :]


def _rope_kernel(pos_ref, invf_ref, c_ref, s1_ref, s2_ref):
    ang = pos_ref[...].astype(F32) * invf_ref[...]
    j = lax.broadcasted_iota(jnp.int32, ang.shape, 1) & (HEAD_DIM - 1)
    cosv, sinv = jnp.cos(ang), jnp.sin(ang)
    half = ROPE_DIM // 2
    c_ref[...] = jnp.where(j < ROPE_DIM, cosv, 1.0)
    s1_ref[...] = jnp.where(j < half, -sinv, 0.0)
    s2_ref[...] = jnp.where((j >= half) & (j < ROPE_DIM), sinv, 0.0)


def _rope_tables(positions):
    s = positions.shape[-1]
    ts = 2048
    half = ROPE_DIM // 2
    inv_freq = np.float32(ROPE_THETA) ** (-np.arange(half, dtype=np.float32) * np.float32(2.0 / ROPE_DIM))
    invf = jnp.asarray(inv_freq[(np.arange(LANES) % HEAD_DIM) % half].reshape(1, LANES).astype(np.float32))
    spec = pl.BlockSpec((ts, LANES), lambda i: (i, 0))
    return pl.pallas_call(
        _rope_kernel,
        grid=(s // ts,),
        in_specs=[pl.BlockSpec((ts, 1), lambda i: (i, 0)), pl.BlockSpec((1, LANES), lambda i: (0, 0))],
        out_specs=[spec, spec, spec],
        out_shape=[jax.ShapeDtypeStruct((s, LANES), F32)] * 3,
        compiler_params=_cparams(1),
        name="rope_tables",
    )(positions.reshape(s, 1), invf)


_ROPE_COL_CHUNKS = (1792, 1920, 2048, 2176, 2560, 2688, 2816)


def _inproj_kernel(x_ref, sc_ref, sh_ref, w_ref, rc_ref, rs1_ref, rs2_ref, o_ref):
    h = (x_ref[...] * (1.0 + sc_ref[...]) + sh_ref[...]).astype(BF16)
    half = ROPE_DIM // 2
    for c0 in range(0, IN_WIDTH, GROUP_W):
        acc = _dot(h, w_ref[:, c0:c0 + GROUP_W])
        for k in range(GROUP_W // LANES):
            col = c0 + k * LANES
            part = acc[:, k * LANES:(k + 1) * LANES]
            if col in _ROPE_COL_CHUNKS:
                part = (part * rc_ref[...] + pltpu.roll(part, LANES - half, 1) * rs1_ref[...]
                        + pltpu.roll(part, half, 1) * rs2_ref[...])
            o_ref[:, col:col + LANES] = part


def _inproj(x, scale, shift, w_bf16, rope):
    s, d = x.shape
    tm = 512
    row = lambda i: (i, 0)
    const = lambda i: (0, 0)
    return pl.pallas_call(
        _inproj_kernel,
        grid=(s // tm,),
        in_specs=[pl.BlockSpec((tm, d), row), pl.BlockSpec((1, d), const), pl.BlockSpec((1, d), const),
                  pl.BlockSpec((d, IN_WIDTH), const),
                  pl.BlockSpec((tm, LANES), row), pl.BlockSpec((tm, LANES), row), pl.BlockSpec((tm, LANES), row)],
        out_specs=pl.BlockSpec((tm, IN_WIDTH), row),
        out_shape=jax.ShapeDtypeStruct((s, IN_WIDTH), F32),
        compiler_params=_cparams(1),
        name="inproj",
    )(x, scale, shift, w_bf16, *rope)


HG_TILE = 512
HG_CHUNKS = HG_TILE // A_CHUNK
N_HEADS = GROUP_W // HEAD_DIM


def _same_head_ones():
    li = lax.broadcasted_iota(jnp.int32, (GROUP_W, GROUP_W), 0)
    lj = lax.broadcasted_iota(jnp.int32, (GROUP_W, GROUP_W), 1)
    return jnp.where((li >> 6) == (lj >> 6), 1.0, 0.0).astype(BF16)


def _hgrn_core(q_ref, z_ref, v_ref, lb_ref, st_ref, qs, zs, vs, qt, kt, ot, oi_ref, rev):
    nb = HG_CHUNKS
    lb = lb_ref[...]
    for src, dst in ((q_ref, qs), (z_ref, zs), (v_ref, vs)):
        val = src[...]
        dst[0] = val[:, :LANES]
        dst[1] = val[:, LANES:]

    def by_pos(buf, j):
        sel = pl.ds(j, nb, stride=A_CHUNK)
        return jnp.concatenate([buf[0, sel, :], buf[1, sel, :]], axis=1)

    order = list(range(A_CHUNK))[::-1] if rev else list(range(A_CHUNK))
    q, f, k, v = [], [], [], []
    for j in order:
        aq, z = by_pos(qs, j), by_pos(zs, j)
        q.append(_silu(aq))
        f.append(lb + (1.0 - lb) * _sigmoid(z))
        k.append((1.0 - lb) * _sigmoid(-z))
        v.append(by_pos(vs, j))
    incl = [f[0]]
    for p in range(1, A_CHUNK):
        incl.append(incl[-1] * f[p])
    rest = [None] * A_CHUNK
    for p in range(A_CHUNK - 2, -1, -1):
        rest[p] = f[p + 1] if rest[p + 1] is None else rest[p + 1] * f[p + 1]
    dec = incl[-1]

    head = lax.broadcasted_iota(jnp.int32, (1, GROUP_W), 1) >> 6
    lo = lax.broadcasted_iota(jnp.int32, (1, LANES), 1) < HEAD_DIM
    ones_bd = _same_head_ones()

    def put(dst, sel, val):
        dst[0, sel, :] = val[:, :LANES]
        dst[1, sel, :] = val[:, LANES:]

    for p, j in enumerate(order):
        xs, decay = [], None
        for d in range(p + 1):
            x = q[p] * k[p - d]
            xs.append((x if decay is None else x * decay).astype(BF16))
            decay = f[p - d] if decay is None else decay * f[p - d]
        sc = _dot(jnp.concatenate(xs, axis=0), ones_bd)
        o = sc[0:nb] * v[p]
        for d in range(1, p + 1):
            o = o + sc[d * nb:(d + 1) * nb] * v[p - d]
        sel = pl.ds(j, nb, stride=A_CHUNK)
        put(ot, sel, o)
        put(qt, sel, q[p] * incl[p])
        put(kt, sel, k[p] if rest[p] is None else k[p] * rest[p])

    def per_head(x):
        return jnp.concatenate([jnp.where(head == h, x, 0.0) for h in range(N_HEADS)], axis=0).astype(BF16)

    chunks = range(nb - 1, -1, -1) if rev else range(nb)
    for n in chunks:
        r1 = slice(n * A_CHUNK, (n + 1) * A_CHUNK)
        lq = per_head(jnp.concatenate([qt[0, r1, :], qt[1, r1, :]], axis=1))
        rk = per_head(jnp.concatenate([kt[0, r1, :], kt[1, r1, :]], axis=1))
        va, vb = v_ref[r1, :LANES], v_ref[r1, LANES:]
        lv = jnp.concatenate([va, pltpu.roll(va, HEAD_DIM, 1), vb, pltpu.roll(vb, HEAD_DIM, 1)], axis=0).astype(BF16)
        state = st_ref[...]
        sb = state.astype(BF16)
        res = _dot_nt(lq, jnp.concatenate([sb, sb], axis=0))
        oi_ref[r1, :LANES] = jnp.where(lo, res[0:A_CHUNK], res[A_CHUNK:2 * A_CHUNK])
        oi_ref[r1, LANES:] = jnp.where(lo, res[2 * A_CHUNK:3 * A_CHUNK], res[3 * A_CHUNK:])
        kv = _dot_tn(lv, rk)[:HEAD_DIM]
        st_ref[...] = state * dec[n:n + 1] + kv
    full = slice(0, HG_TILE)
    return jnp.concatenate([ot[0, full, :], ot[1, full, :]], axis=1) + oi_ref[...]


def _hgrn_fwd_kernel(q_ref, f_ref, i_ref, lb_ref, o_ref, st_ref, *scratch):
    @pl.when(pl.program_id(0) == 0)
    def _():
        st_ref[...] = jnp.zeros_like(st_ref)

    o_ref[...] = _hgrn_core(q_ref, f_ref, i_ref, lb_ref, st_ref, *scratch, rev=False)


def _hgrn_bwd_kernel(q_ref, f_ref, i_ref, g_ref, of_ref, lb_ref, nw_ref, o_ref, st_ref, *scratch):
    @pl.when(pl.program_id(0) == 0)
    def _():
        st_ref[...] = jnp.zeros_like(st_ref)

    o = of_ref[...] + _hgrn_core(q_ref, f_ref, i_ref, lb_ref, st_ref, *scratch, rev=True)
    ones_bd = _same_head_ones()
    sq_hi, sq_lo = _split_bf16(o * o)
    ms = (_dot(sq_hi, ones_bd) + _dot(sq_lo, ones_bd)) * (1.0 / HEAD_DIM)
    o_ref[...] = o * lax.rsqrt(ms + RMS_EPS) * nw_ref[...] * _silu(g_ref[...])


def _hgrn_scratch():
    split = lambda rows: pltpu.VMEM((2, rows, LANES), F32)
    return [pltpu.VMEM((HEAD_DIM, GROUP_W), F32),
            split(HG_TILE), split(HG_TILE), split(HG_TILE),
            split(HG_TILE), split(HG_TILE), split(HG_TILE), pltpu.VMEM((HG_TILE, GROUP_W), F32)]


def _hgrn2(proj, lb, norm_w):
    s = proj.shape[0]
    ts = HG_TILE
    n = s // ts
    col = lambda j: pl.BlockSpec((ts, GROUP_W), lambda i: (i, j))
    rcol = lambda j: pl.BlockSpec((ts, GROUP_W), lambda i: (n - 1 - i, j))
    vec = pl.BlockSpec((1, GROUP_W), lambda i: (0, 0))
    o_f = pl.pallas_call(
        _hgrn_fwd_kernel,
        grid=(n,),
        in_specs=[col(0), col(1), col(3), vec],
        out_specs=pl.BlockSpec((ts, GROUP_W), lambda i: (i, 0)),
        out_shape=jax.ShapeDtypeStruct((s, GROUP_W), F32),
        scratch_shapes=_hgrn_scratch(),
        compiler_params=_cparams(1),
        name="hgrn_fwd",
    )(proj, proj, proj, lb)
    return pl.pallas_call(
        _hgrn_bwd_kernel,
        grid=(n,),
        in_specs=[rcol(0), rcol(2), rcol(3), rcol(4), pl.BlockSpec((ts, GROUP_W), lambda i: (n - 1 - i, 0)),
                  vec, vec],
        out_specs=pl.BlockSpec((ts, GROUP_W), lambda i: (n - 1 - i, 0)),
        out_shape=jax.ShapeDtypeStruct((s, GROUP_W), F32),
        scratch_shapes=_hgrn_scratch(),
        compiler_params=_cparams(1),
        name="hgrn_bwd",
    )(proj, proj, proj, proj, o_f, lb, norm_w)


B_PAD = 16


def _conv_kernel(v_ref, g_ref, vp_ref, gp_ref, vn_ref, gn_ref, w_ref, b_ref, nw_ref, nb_ref, o_ref, ubuf):
    i = pl.program_id(0)
    ts = v_ref.shape[0]
    glu = lambda a, b: a * _sigmoid(b)
    ubuf[0:B_PAD, :] = jnp.where(i > 0, glu(vp_ref[...], gp_ref[...]), 0.0)
    ubuf[B_PAD:B_PAD + ts, :] = glu(v_ref[...], g_ref[...])
    ubuf[B_PAD + ts:, :] = jnp.where(i < pl.num_programs(0) - 1, glu(vn_ref[...], gn_ref[...]), 0.0)
    acc = jnp.zeros((ts, GROUP_W), F32) + b_ref[...]
    for j in range(B_KERNEL):
        acc = acc + ubuf[pl.ds(B_PAD - B_KERNEL // 2 + j, ts), :] * w_ref[j:j + 1, :]
    o_ref[...] = _silu(_layer_norm(acc, nw_ref[...], nb_ref[...]))


def _conv_module(proj, conv_w, conv_b, norm_w, norm_b):
    s = proj.shape[0]
    ts = 512
    n = s // ts
    per = ts // B_PAD
    last = s // B_PAD - 1
    cur = lambda j: pl.BlockSpec((ts, GROUP_W), lambda i: (i, j))
    prev = lambda j: pl.BlockSpec((B_PAD, GROUP_W), lambda i: (jnp.maximum(i * per - 1, 0), j))
    nxt = lambda j: pl.BlockSpec((B_PAD, GROUP_W), lambda i: (jnp.minimum((i + 1) * per, last), j))
    vec = pl.BlockSpec((1, GROUP_W), lambda i: (0, 0))
    wpad = jnp.pad(conv_w, ((0, 1), (0, 0)))
    return pl.pallas_call(
        _conv_kernel,
        grid=(n,),
        in_specs=[cur(5), cur(6), prev(5), prev(6), nxt(5), nxt(6),
                  pl.BlockSpec((B_KERNEL + 1, GROUP_W), lambda i: (0, 0)), vec, vec, vec],
        out_specs=pl.BlockSpec((ts, GROUP_W), lambda i: (i, 0)),
        out_shape=jax.ShapeDtypeStruct((s, GROUP_W), F32),
        scratch_shapes=[pltpu.VMEM((ts + 2 * B_PAD, GROUP_W), F32)],
        compiler_params=_cparams(1),
        name="conv_module",
    )(proj, proj, proj, proj, proj, proj, wpad, conv_b, norm_w, norm_b)


C_MAX_BLOCK = 256


def _dilated_block_sizes(span):
    return sorted({min(span // dil, C_MAX_BLOCK) for _, dil in C_PATTERNS})


def _dilated_kernel(q_ref, kp_ref, kc_ref, kn_ref, vp_ref, vc_ref, vn_ref, o_ref, qw, kw, vw, ob, mb, lb, *band_refs,
                    seq, span):
    i = pl.program_id(0)
    band = dict(zip(_dilated_block_sizes(span), band_refs))

    @pl.when(i == 0)
    def _():
        for nq, ref in band.items():
            a = lax.broadcasted_iota(jnp.int32, ref.shape, 0) & (nq - 1)
            b = lax.broadcasted_iota(jnp.int32, ref.shape, 1)
            ref[...] = jnp.where(jnp.abs(a - (b - C_SIDE)) <= C_SIDE, 0.0, NEG_INF)

    def put(dst, row0, val):
        n = val.shape[0]
        dst[0, row0:row0 + n, :] = val[:, :LANES]
        dst[1, row0:row0 + n, :] = val[:, LANES:]

    put(qw, 0, q_ref[...] * (HEAD_DIM ** -0.5))
    for dst, refs in ((kw, (kp_ref, kc_ref, kn_ref)), (vw, (vp_ref, vc_ref, vn_ref))):
        put(dst, 0, refs[0][...])
        put(dst, C_HALO, refs[1][...])
        put(dst, C_HALO + span, refs[2][...])

    head = lax.broadcasted_iota(jnp.int32, (1, GROUP_W), 1) >> 6
    heads = [head == h for h in range(4)]

    def rows(start, size, dil):
        return pl.ds(start, size) if dil == 1 else pl.ds(start, size, stride=dil)

    def load(src, sel):
        return jnp.concatenate([src[0, sel, :], src[1, sel, :]], axis=1)

    def store(dst, sel, val):
        dst[0, sel, :] = val[:, :LANES]
        dst[1, sel, :] = val[:, LANES:]

    def by_head(x, nq):
        out = jnp.where(heads[0], x[0:nq], 0.0)
        for h in range(1, 4):
            out = out + jnp.where(heads[h], x[h * nq:(h + 1) * nq], 0.0)
        return out

    def attend(first, r, dil, blk, nq):
        nk = nq + 2 * C_SIDE
        qrows = rows(r + dil * blk * nq, nq, dil)
        koff = C_HALO + r + dil * (blk * nq - C_SIDE)
        qs = load(qw, qrows)
        ks = load(kw, rows(koff, nk, dil)).astype(BF16)
        vs = load(vw, rows(koff, nk, dil)).astype(BF16)
        lhs = jnp.concatenate([jnp.where(hm, qs, 0.0) for hm in heads], axis=0).astype(BF16)
        kpos = i * span + (koff - C_HALO) + dil * lax.broadcasted_iota(jnp.int32, (1, nk), 1)
        edge = jnp.where((kpos >= 0) & (kpos < seq), 0.0, NEG_INF)
        sc = _dot_nt(lhs, ks) + band[nq][...] + edge
        m = jnp.max(sc, axis=1, keepdims=True)
        p = jnp.exp(sc - m)
        l = jnp.sum(p, axis=1, keepdims=True)
        o_new = by_head(_dot(p.astype(BF16), vs), nq)
        m_new = by_head(jnp.broadcast_to(m, (4 * nq, GROUP_W)), nq)
        l_new = by_head(jnp.broadcast_to(l, (4 * nq, GROUP_W)), nq)
        if first:
            store(ob, qrows, o_new)
            store(mb, qrows, m_new)
            store(lb, qrows, l_new)
        else:
            m_old = load(mb, qrows)
            m_tot = jnp.maximum(m_old, m_new)
            w_old, w_new = jnp.exp(m_old - m_tot), jnp.exp(m_new - m_tot)
            store(ob, qrows, w_old * load(ob, qrows) + w_new * o_new)
            store(lb, qrows, w_old * load(lb, qrows) + w_new * l_new)
            store(mb, qrows, m_tot)

    for pat, (_, dil) in enumerate(C_PATTERNS):
        sub = span // dil
        nq = min(sub, C_MAX_BLOCK)
        for r in range(dil):
            for blk in range(sub // nq):
                attend(pat == 0, r, dil, blk, nq)
    full = slice(0, span)
    o_ref[...] = load(ob, full) / load(lb, full)


def _dilated_attention(proj):
    s = proj.shape[0]
    span = 1024
    n = s // span
    per = span // C_HALO
    last = s // C_HALO - 1
    cur = lambda j: pl.BlockSpec((span, GROUP_W), lambda i: (i, j))
    prev = lambda j: pl.BlockSpec((C_HALO, GROUP_W), lambda i: (jnp.maximum(i * per - 1, 0), j))
    nxt = lambda j: pl.BlockSpec((C_HALO, GROUP_W), lambda i: (jnp.minimum((i + 1) * per, last), j))
    win = span + 2 * C_HALO
    return pl.pallas_call(
        functools.partial(_dilated_kernel, seq=s, span=span),
        grid=(n,),
        in_specs=[cur(7), prev(8), cur(8), nxt(8), prev(9), cur(9), nxt(9)],
        out_specs=pl.BlockSpec((span, GROUP_W), lambda i: (i, 0)),
        out_shape=jax.ShapeDtypeStruct((s, GROUP_W), F32),
        scratch_shapes=[pltpu.VMEM((2, span, LANES), F32),
                        pltpu.VMEM((2, win, LANES), F32), pltpu.VMEM((2, win, LANES), F32),
                        pltpu.VMEM((2, span, LANES), F32), pltpu.VMEM((2, span, LANES), F32),
                        pltpu.VMEM((2, span, LANES), F32)]
        + [pltpu.VMEM((N_HEADS * nq, nq + 2 * C_SIDE), F32) for nq in _dilated_block_sizes(span)],
        compiler_params=_cparams(1),
        name="dilated_attention",
    )(proj, proj, proj, proj, proj, proj, proj)


def _window_kernel(sink_ref, q_ref, kp_ref, kc_ref, kn_ref, vp_ref, vc_ref, vn_ref, o_ref, *, seq):
    i = pl.program_id(0)
    tq = q_ref.shape[0]
    hw = D_HALF_WINDOW
    nk = tq + 2 * hw
    kwin = jnp.concatenate([kp_ref[...], kc_ref[...], kn_ref[...]], axis=0).astype(BF16)
    vwin = jnp.concatenate([vp_ref[...], vc_ref[...], vn_ref[...]], axis=0).astype(BF16)
    lo = lax.broadcasted_iota(jnp.int32, (1, LANES), 1) < HEAD_DIM
    q = q_ref[...] * (HEAD_DIM ** -0.5)
    qa, qb = q[:, :LANES], q[:, LANES:]
    swap = lambda x: pltpu.roll(x, HEAD_DIM, 1)
    lhs = (jnp.concatenate([jnp.where(lo, qa, 0.0), jnp.where(lo, swap(qa), 0.0)], axis=0),
           jnp.concatenate([jnp.where(lo, 0.0, swap(qb)), jnp.where(lo, 0.0, qb)], axis=0))
    a = lax.broadcasted_iota(jnp.int32, (2 * tq, nk), 0) & (tq - 1)
    b = lax.broadcasted_iota(jnp.int32, (2 * tq, nk), 1)
    kpos = i * tq - hw + b
    mask = (jnp.abs(b - hw - a) <= hw) & (kpos >= 0) & (kpos < seq)
    second = lax.broadcasted_iota(jnp.int32, (2 * tq, 1), 0) >= tq
    outs = []
    for g in range(2):
        sc = jnp.where(mask, _dot_nt(lhs[g].astype(BF16), kwin), NEG_INF)
        m = jnp.max(sc, axis=1, keepdims=True)
        p = jnp.exp(sc - m)
        l = jnp.sum(p, axis=1, keepdims=True)
        o = _dot(p.astype(BF16), vwin)
        sk = jnp.where(second, sink_ref[2 * g + 1], sink_ref[2 * g])
        m_tot = jnp.maximum(m, sk)
        w = jnp.exp(m - m_tot)
        den = l * w + jnp.exp(sk - m_tot)
        outs.append(o * (w / den))
    o_ref[:, :LANES] = jnp.where(lo, outs[0][:tq], swap(outs[0][tq:]))
    o_ref[:, LANES:] = jnp.where(lo, swap(outs[1][:tq]), outs[1][tq:])


def _window_attention(proj, sink):
    s = proj.shape[0]
    tq = 256
    n = s // tq
    hw = D_HALF_WINDOW
    per = tq // hw
    last = s // hw - 1
    kcol, vcol = 2816 // LANES, 2944 // LANES
    cur = lambda j: pl.BlockSpec((tq, LANES), lambda i: (i, j))
    prev = lambda j: pl.BlockSpec((hw, LANES), lambda i: (jnp.maximum(i * per - 1, 0), j))
    nxt = lambda j: pl.BlockSpec((hw, LANES), lambda i: (jnp.minimum((i + 1) * per, last), j))
    return pl.pallas_call(
        functools.partial(_window_kernel, seq=s),
        grid=(n,),
        in_specs=[pl.BlockSpec(memory_space=pltpu.SMEM),
                  pl.BlockSpec((tq, GROUP_W), lambda i: (i, 10)),
                  prev(kcol), cur(kcol), nxt(kcol), prev(vcol), cur(vcol), nxt(vcol)],
        out_specs=pl.BlockSpec((tq, GROUP_W), lambda i: (i, 0)),
        out_shape=jax.ShapeDtypeStruct((s, GROUP_W), F32),
        compiler_params=_cparams(1),
        name="window_attention",
    )(sink, proj, proj, proj, proj, proj, proj, proj)


def _outproj_kernel(ya_ref, yb_ref, yc_ref, yd_ref, w_ref, x_ref, g_ref, lw_ref, lb_ref, o_ref):
    y = None
    for k, ref in enumerate((ya_ref, yb_ref, yc_ref, yd_ref)):
        part = _dot(ref[...].astype(BF16), w_ref[k * GROUP_W:(k + 1) * GROUP_W, :])
        y = part if y is None else y + part
    r = DEEPNORM_ALPHA * x_ref[...] + (1.0 + g_ref[...]) * y
    o_ref[...] = _layer_norm(r, lw_ref[...], lb_ref[...])


def _outproj(ys, w_bf16, x, gate, ln_w, ln_b):
    s, d = x.shape
    tm = 512
    row = lambda i: (i, 0)
    const = lambda i: (0, 0)
    vec = pl.BlockSpec((1, d), const)
    grp = pl.BlockSpec((tm, GROUP_W), row)
    return pl.pallas_call(
        _outproj_kernel,
        grid=(s // tm,),
        in_specs=[grp, grp, grp, grp, pl.BlockSpec((4 * GROUP_W, d), const), pl.BlockSpec((tm, d), row),
                  vec, vec, vec],
        out_specs=pl.BlockSpec((tm, d), row),
        out_shape=jax.ShapeDtypeStruct((s, d), F32),
        compiler_params=_cparams(1),
        name="outproj_ln",
    )(*ys, w_bf16, x, gate, ln_w, ln_b)


FFN_CHUNK = 256


def _ffn_kernel(x_ref, sc_ref, sh_ref, wu_ref, wd_ref, g_ref, lw_ref, lb_ref, o_ref):
    x = x_ref[...]
    h = (x * (1.0 + sc_ref[...]) + sh_ref[...]).astype(BF16)
    acc = jnp.zeros(x.shape, F32)
    for c0 in range(0, FFN_DIM, FFN_CHUNK):
        gate = _dot(h, wu_ref[:, c0:c0 + FFN_CHUNK])
        up = _dot(h, wu_ref[:, FFN_DIM + c0:FFN_DIM + c0 + FFN_CHUNK])
        acc = acc + _dot((_silu(gate) * up).astype(BF16), wd_ref[c0:c0 + FFN_CHUNK, :])
    r = DEEPNORM_ALPHA * x + (1.0 + g_ref[...]) * acc
    o_ref[...] = _layer_norm(r, lw_ref[...], lb_ref[...])


def _dense_ffn(x, scale, shift, wu_bf16, wd_bf16, gate, ln_w, ln_b):
    s, d = x.shape
    tm = 512
    row = lambda i: (i, 0)
    const = lambda i: (0, 0)
    vec = pl.BlockSpec((1, d), const)
    return pl.pallas_call(
        _ffn_kernel,
        grid=(s // tm,),
        in_specs=[pl.BlockSpec((tm, d), row), vec, vec,
                  pl.BlockSpec((d, 2 * FFN_DIM), const), pl.BlockSpec((FFN_DIM, d), const), vec, vec, vec],
        out_specs=pl.BlockSpec((tm, d), row),
        out_shape=jax.ShapeDtypeStruct((s, d), F32),
        compiler_params=_cparams(1),
        name="dense_ffn_ln",
    )(x, scale, shift, wu_bf16, wd_bf16, gate, ln_w, ln_b)


ROW_TILE = 8


def _to_token_tiles(ref, val):
    n = val.shape[0]
    for c in range(ROW_TILE):
        ref[pl.ds(c, n, stride=ROW_TILE), :] = val[:, c * LANES:(c + 1) * LANES]


def _from_token_tiles(ref, n):
    return jnp.concatenate([ref[pl.ds(c, n, stride=ROW_TILE), :] for c in range(ROW_TILE)], axis=1)


def _router_kernel(x_ref, sc_ref, sh_ref, w_ref, h_ref, e_ref, p_ref):
    h = x_ref[...] * (1.0 + sc_ref[...]) + sh_ref[...]
    _to_token_tiles(h_ref, h)
    h_hi, h_lo = _split_bf16(h)
    w_hi, w_lo = _split_bf16(w_ref[...])
    logits = _dot(h_hi, w_hi) + _dot(h_hi, w_lo) + _dot(h_lo, w_hi)
    lane = lax.broadcasted_iota(jnp.int32, logits.shape, 1)
    logits = jnp.where(lane < N_EXPERTS, logits, NEG_INF)
    m1 = jnp.max(logits, axis=1, keepdims=True)
    e1 = jnp.min(jnp.where(logits == m1, lane, LANES), axis=1, keepdims=True)
    rest = jnp.where(lane == e1, NEG_INF, logits)
    m2 = jnp.max(rest, axis=1, keepdims=True)
    e2 = jnp.min(jnp.where(rest == m2, lane, LANES), axis=1, keepdims=True)
    t = jnp.exp(m2 - m1)
    g1 = 1.0 / (1.0 + t)
    e_ref[...] = jnp.where(lane == 0, e1, jnp.where(lane == 1, e2, 0))
    p_ref[...] = jnp.where(lane == 0, g1, jnp.where(lane == 1, t * g1, 0.0))


def _router(x, scale, shift, w_router):
    s, d = x.shape
    tm = 512
    row = lambda i: (i, 0)
    const = lambda i: (0, 0)
    vec = pl.BlockSpec((1, d), const)
    wpad = jnp.pad(w_router, ((0, 0), (0, LANES - N_EXPERTS)))
    return pl.pallas_call(
        _router_kernel,
        grid=(s // tm,),
        in_specs=[pl.BlockSpec((tm, d), row), vec, vec, pl.BlockSpec((d, LANES), const)],
        out_specs=[pl.BlockSpec((tm * ROW_TILE, LANES), row), pl.BlockSpec((tm, LANES), row),
                   pl.BlockSpec((tm, LANES), row)],
        out_shape=[jax.ShapeDtypeStruct((s * ROW_TILE, LANES), F32), jax.ShapeDtypeStruct((s, LANES), jnp.int32),
                   jax.ShapeDtypeStruct((s, LANES), F32)],
        compiler_params=_cparams(1),
        name="moe_router",
    )(x, scale, shift, wpad)


MOE_TILE = 512
MOE_CHUNK = 1792
ROW_UNROLL = 8
TILE_ROWS = MOE_TILE * ROW_TILE


def _expert_kernel(te_ref, nu_ref, tokc_ref, tokn_ref, dst_ref, h_hbm, wg_ref, wu_ref, wd_ref, y_hbm,
                   xbuf, xb16, acc, ybuf, gsem, ssem):
    i, f = pl.program_id(0), pl.program_id(1)
    nt, nf = pl.num_programs(0), pl.num_programs(1)
    n_used = nu_ref[0]
    slot = i & 1
    tail = y_hbm.shape[0] - TILE_ROWS

    def rows_loop(fn):
        def body(g, carry):
            base = pl.multiple_of(g * ROW_UNROLL, ROW_UNROLL)
            for j in range(ROW_UNROLL):
                fn(base + j)
            return carry
        lax.fori_loop(0, MOE_TILE // ROW_UNROLL, body, 0)

    def token(ref, r):
        return ref.at[pl.ds(pl.multiple_of(r, ROW_TILE), ROW_TILE)]

    def gather_rows(ids_ref, s):
        rows_loop(lambda r: pltpu.make_async_copy(token(h_hbm, ids_ref[0, r]), token(xbuf.at[s], r * ROW_TILE),
                                                  gsem.at[s]).start())

    def gather_wait(s):
        pltpu.make_async_copy(h_hbm.at[pl.ds(0, TILE_ROWS)], xbuf.at[s], gsem.at[s]).wait()

    def scatter_rows(s):
        rows_loop(lambda r: pltpu.make_async_copy(token(ybuf.at[s], r * ROW_TILE), token(y_hbm, dst_ref[0, r]),
                                                  ssem.at[s]).start())

    def scatter_wait(s):
        pltpu.make_async_copy(ybuf.at[s], y_hbm.at[pl.ds(0, TILE_ROWS)], ssem.at[s]).wait()

    @pl.when(f == 0)
    def _():
        @pl.when(i == 0)
        def _():
            ybuf[1] = jnp.zeros(ybuf.shape[1:], F32)
            zero_tail = pltpu.make_async_copy(ybuf.at[1], y_hbm.at[pl.ds(tail, TILE_ROWS)], ssem.at[1])
            zero_tail.start()
            zero_tail.wait()
            gather_rows(tokc_ref, 0)

        @pl.when(i + 1 < n_used)
        def _():
            gather_rows(tokn_ref, 1 - slot)

        @pl.when(i < n_used)
        def _():
            gather_wait(slot)
            xb16[...] = _from_token_tiles(xbuf.at[slot], MOE_TILE).astype(BF16)

    @pl.when(i < n_used)
    def _():
        x = xb16[...]
        act = _silu(_dot(x, wg_ref[...])) * _dot(x, wu_ref[...])
        part = _dot(act.astype(BF16), wd_ref[...])

        @pl.when(f == 0)
        def _():
            acc[...] = part

        @pl.when((f > 0) & (f < nf - 1))
        def _():
            acc[...] += part

        @pl.when(f == nf - 1)
        def _():
            _to_token_tiles(ybuf.at[slot], acc[...] + part)

    @pl.when(f == nf - 1)
    def _():
        @pl.when((i >= 1) & (i - 1 < n_used))
        def _():
            scatter_wait(1 - slot)

        @pl.when(i < n_used)
        def _():
            scatter_rows(slot)

        @pl.when((i == nt - 1) & (i < n_used))
        def _():
            scatter_wait(slot)


def _experts(h, slot_tok, slot_dst, tile_expert, n_used, wu_bf16, wd_bf16, n_rows_out):
    d = D_MODEL
    n_tiles = slot_tok.shape[0] // MOE_TILE
    nf = EXPERT_DIM // MOE_CHUNK
    assert nf >= 2
    fidx = lambda i, f, nu: jnp.where(i < nu[0], f, nf - 1)
    ids = lambda fn: pl.BlockSpec((None, 1, MOE_TILE), lambda i, f, te, nu: (fn(i), 0, 0), memory_space=pltpu.SMEM)
    grid_spec = pltpu.PrefetchScalarGridSpec(
        num_scalar_prefetch=2,
        grid=(n_tiles, nf),
        in_specs=[ids(lambda i: i), ids(lambda i: jnp.minimum(i + 1, n_tiles - 1)), ids(lambda i: i),
                  pl.BlockSpec(memory_space=pl.ANY),
                  pl.BlockSpec((None, d, MOE_CHUNK), lambda i, f, te, nu: (te[i], 0, fidx(i, f, nu))),
                  pl.BlockSpec((None, d, MOE_CHUNK), lambda i, f, te, nu: (te[i], 0, nf + fidx(i, f, nu))),
                  pl.BlockSpec((None, MOE_CHUNK, d), lambda i, f, te, nu: (te[i], fidx(i, f, nu), 0))],
        out_specs=pl.BlockSpec(memory_space=pl.ANY),
        scratch_shapes=[pltpu.VMEM((2, TILE_ROWS, LANES), F32), pltpu.VMEM((MOE_TILE, d), BF16),
                        pltpu.VMEM((MOE_TILE, d), F32), pltpu.VMEM((2, TILE_ROWS, LANES), F32),
                        pltpu.SemaphoreType.DMA((2,)), pltpu.SemaphoreType.DMA((2,))],
    )
    tok3 = (slot_tok * ROW_TILE).reshape(n_tiles, 1, MOE_TILE)
    dst3 = (slot_dst * ROW_TILE).reshape(n_tiles, 1, MOE_TILE)
    return pl.pallas_call(
        _expert_kernel,
        grid_spec=grid_spec,
        out_shape=jax.ShapeDtypeStruct((n_rows_out * ROW_TILE, LANES), F32),
        compiler_params=_cparams(2),
        name="moe_experts",
    )(tile_expert, n_used, tok3, tok3, dst3, h, wu_bf16, wu_bf16, wd_bf16)


def _combine_kernel(x_ref, y1_ref, y2_ref, p_ref, g_ref, lw_ref, lb_ref, o_ref):
    p = p_ref[...]
    n = x_ref.shape[0]
    f = p[:, 0:1] * _from_token_tiles(y1_ref, n) + p[:, 1:2] * _from_token_tiles(y2_ref, n)
    r = DEEPNORM_ALPHA * x_ref[...] + (1.0 + g_ref[...]) * f
    o_ref[...] = _layer_norm(r, lw_ref[...], lb_ref[...])


def _combine(x, y, probs, gate, ln_w, ln_b):
    s, d = x.shape
    tm = 512
    row = lambda i: (i, 0)
    vec = pl.BlockSpec((1, d), lambda i: (0, 0))
    big = pl.BlockSpec((tm, d), row)
    return pl.pallas_call(
        _combine_kernel,
        grid=(s // tm,),
        in_specs=[big, pl.BlockSpec((tm * ROW_TILE, LANES), row),
                  pl.BlockSpec((tm * ROW_TILE, LANES), lambda i: (s // tm + i, 0)),
                  pl.BlockSpec((tm, LANES), row), vec, vec, vec],
        out_specs=big,
        out_shape=jax.ShapeDtypeStruct((s, d), F32),
        compiler_params=_cparams(1),
        name="moe_combine_ln",
    )(x, y, y, probs, gate, ln_w, ln_b)


def _moe(x, scale, shift, w_router, wu_bf16, wd_bf16, gate, ln_w, ln_b):
    t, d = x.shape
    h, sel, probs = _router(x, scale, shift, w_router)
    flat_e = sel[:, :TOP_K].reshape(-1)
    n_assign = t * TOP_K
    n_tiles = -(-n_assign // MOE_TILE) + N_EXPERTS
    n_slots = n_tiles * MOE_TILE
    onehot = (flat_e[:, None] == jnp.arange(N_EXPERTS, dtype=jnp.int32)[None, :]).astype(jnp.int32)
    rank = jnp.sum((jnp.cumsum(onehot, axis=0) - onehot) * onehot, axis=1)
    counts = jnp.sum(onehot, axis=0)
    padded = (counts + MOE_TILE - 1) // MOE_TILE * MOE_TILE
    padded_end = jnp.cumsum(padded)
    slot = (padded_end - padded)[flat_e] + rank
    assign = jnp.full((n_slots,), -1, jnp.int32).at[slot].set(jnp.arange(n_assign, dtype=jnp.int32))
    used = assign >= 0
    slot_tok = jnp.where(used, assign >> 1, 0)
    slot_dst = jnp.where(used, (assign & 1) * t + (assign >> 1),
                         n_assign + (jnp.arange(n_slots, dtype=jnp.int32) & (MOE_TILE - 1)))
    n_used = (padded_end[-1] // MOE_TILE).astype(jnp.int32).reshape(1)
    tile_ids = jnp.minimum(jnp.arange(n_tiles, dtype=jnp.int32), n_used[0] - 1)
    tile_expert = jnp.minimum(jnp.searchsorted(padded_end, tile_ids * MOE_TILE, side="right"),
                              N_EXPERTS - 1).astype(jnp.int32)
    y = _experts(h, slot_tok, slot_dst, tile_expert, n_used, wu_bf16, wd_bf16, n_assign + MOE_TILE)
    return _combine(x, y, probs, gate, ln_w, ln_b)


def kernel(x, c, positions, w_ada, b_ada, w_in, w_out, a_lower_bound, a_norm_w, b_conv_w, b_conv_b, b_norm_w,
           b_norm_b, d_sink, ln_w, ln_b, ffn_w_up, ffn_w_down, moe_router, moe_w_up, moe_w_down):
    batch, s, d = x.shape
    assert batch == 1 and d == D_MODEL
    x = x.reshape(s, d)
    mod = _adaln(c, w_ada, b_ada)
    rope = _rope_tables(positions)
    lb_cum = jnp.cumsum(jax.nn.softmax(a_lower_bound.astype(F32), axis=0), axis=0)
    lb_all = lb_cum - lb_cum[0]
    row = lambda v: v.reshape(1, -1)
    for layer in range(DEPTH):
        shift1, scale1, gate1, shift2, scale2, gate2 = [mod[layer, :, k * d:(k + 1) * d] for k in range(6)]
        proj = _inproj(x, scale1, shift1, w_in[layer].astype(BF16), rope)
        ya = _hgrn2(proj, row(lb_all[layer]), row(jnp.tile(a_norm_w[layer], GROUP_W // HEAD_DIM)))
        yb = _conv_module(proj, b_conv_w[layer], row(b_conv_b[layer]), row(b_norm_w[layer]), row(b_norm_b[layer]))
        yc = _dilated_attention(proj)
        yd = _window_attention(proj, d_sink[layer])
        x = _outproj((ya, yb, yc, yd), w_out[layer].astype(BF16), x, gate1, row(ln_w[layer, 0]), row(ln_b[layer, 0]))
        lw, lbias = row(ln_w[layer, 1]), row(ln_b[layer, 1])
        if layer % 2 == 0:
            x = _dense_ffn(x, scale2, shift2, ffn_w_up[layer // 2].astype(BF16),
                           ffn_w_down[layer // 2].astype(BF16), gate2, lw, lbias)
        else:
            x = _moe(x, scale2, shift2, moe_router[layer // 2], moe_w_up[layer // 2].astype(BF16),
                     moe_w_down[layer // 2].astype(BF16), gate2, lw, lbias)
    return x.reshape(batch, s, d)
```

```python
import functools

import numpy as np
import jax
import jax.numpy as jnp
from jax import lax
from jax.experimental import pallas as pl
from jax.experimental.pallas import tpu as pltpu

F32 = jnp.float32
BF16 = jnp.bfloat16

D_MODEL = 1024
DEPTH = 2
HEAD_DIM = 64
A_CHUNK = 16
B_KERNEL = 31
GROUP_W = 256
C_PATTERNS = ((128, 1), (512, 4), (2048, 16))
C_SIDE = 64
C_HALO = 1024
D_HALF_WINDOW = 128
ROPE_THETA = 500000.0
ROPE_DIM = HEAD_DIM // 4
FFN_DIM = 2816
N_EXPERTS = 8
TOP_K = 2
EXPERT_DIM = 3584
IN_WIDTH = 3072
DEEPNORM_ALPHA = (2 * DEPTH) ** 0.25
LN_EPS = 1e-5
RMS_EPS = 1e-6
NEG_INF = -1e30

LANES = 128
VMEM_LIMIT = 56 * 1024 * 1024


def _cparams(n_axes=1, vmem=VMEM_LIMIT):
    return pltpu.CompilerParams(dimension_semantics=("arbitrary",) * n_axes, vmem_limit_bytes=vmem)


def _sigmoid(x):
    return 1.0 / (1.0 + jnp.exp(-x))


def _silu(x):
    return x * _sigmoid(x)


def _split_bf16(x):
    hi = x.astype(BF16)
    lo = (x - hi.astype(F32)).astype(BF16)
    return hi, lo


def _layer_norm(r, w, b):
    mu = jnp.mean(r, axis=-1, keepdims=True)
    d = r - mu
    var = jnp.mean(d * d, axis=-1, keepdims=True)
    return d * lax.rsqrt(var + LN_EPS) * w + b


def _dot_nt(a, b):
    return lax.dot_general(a, b, (((1,), (1,)), ((), ())), preferred_element_type=F32)


def _dot_tn(a, b):
    return lax.dot_general(a, b, (((0,), (0,)), ((), ())), preferred_element_type=F32)


def _dot(a, b):
    return jnp.dot(a, b, preferred_element_type=F32)


def _adaln_kernel(c_ref, w_ref, b_ref, o_ref):
    a_hi, a_lo = _split_bf16(_silu(c_ref[...]))
    w_hi, w_lo = _split_bf16(w_ref[...])
    o_ref[...] = _dot(a_hi, w_hi) + _dot(a_hi, w_lo) + _dot(a_lo, w_hi) + b_ref[...]


def _adaln(c, w_ada, b_ada):
    depth, d, n = w_ada.shape
    tn = 1536
    c8 = jnp.broadcast_to(c.reshape(1, d), (8, d))
    out = pl.pallas_call(
        _adaln_kernel,
        grid=(depth, n // tn),
        in_specs=[pl.BlockSpec((8, d), lambda l, j: (0, 0)),
                  pl.BlockSpec((None, d, tn), lambda l, j: (l, 0, j)),
                  pl.BlockSpec((None, 1, tn), lambda l, j: (l, 0, j))],
        out_specs=pl.BlockSpec((None, 8, tn), lambda l, j: (l, 0, j)),
        out_shape=jax.ShapeDtypeStruct((depth, 8, n), F32),
        compiler_params=_cparams(2),
        name="adaln",
    )(c8, w_ada, b_ada.reshape(depth, 1, n))
    return out[:, 0:1, :]


def _rope_kernel(pos_ref, invf_ref, c_ref, s1_ref, s2_ref):
    ang = pos_ref[...].astype(F32) * invf_ref[...]
    j = lax.broadcasted_iota(jnp.int32, ang.shape, 1) & (HEAD_DIM - 1)
    cosv, sinv = jnp.cos(ang), jnp.sin(ang)
    half = ROPE_DIM // 2
    c_ref[...] = jnp.where(j < ROPE_DIM, cosv, 1.0)
    s1_ref[...] = jnp.where(j < half, -sinv, 0.0)
    s2_ref[...] = jnp.where((j >= half) & (j < ROPE_DIM), sinv, 0.0)


def _rope_tables(positions):
    s = positions.shape[-1]
    ts = 2048
    half = ROPE_DIM // 2
    inv_freq = np.float32(ROPE_THETA) ** (-np.arange(half, dtype=np.float32) * np.float32(2.0 / ROPE_DIM))
    invf = jnp.asarray(inv_freq[(np.arange(LANES) % HEAD_DIM) % half].reshape(1, LANES).astype(np.float32))
    spec = pl.BlockSpec((ts, LANES), lambda i: (i, 0))
    return pl.pallas_call(
        _rope_kernel,
        grid=(s // ts,),
        in_specs=[pl.BlockSpec((ts, 1), lambda i: (i, 0)), pl.BlockSpec((1, LANES), lambda i: (0, 0))],
        out_specs=[spec, spec, spec],
        out_shape=[jax.ShapeDtypeStruct((s, LANES), F32)] * 3,
        compiler_params=_cparams(1),
        name="rope_tables",
    )(positions.reshape(s, 1), invf)


_ROPE_COL_CHUNKS = (1792, 1920, 2048, 2176, 2560, 2688, 2816)


def _inproj_kernel(x_ref, sc_ref, sh_ref, w_ref, rc_ref, rs1_ref, rs2_ref, o_ref):
    h = (x_ref[...] * (1.0 + sc_ref[...]) + sh_ref[...]).astype(BF16)
    half = ROPE_DIM // 2
    for c0 in range(0, IN_WIDTH, GROUP_W):
        acc = _dot(h, w_ref[:, c0:c0 + GROUP_W])
        for k in range(GROUP_W // LANES):
            col = c0 + k * LANES
            part = acc[:, k * LANES:(k + 1) * LANES]
            if col in _ROPE_COL_CHUNKS:
                part = (part * rc_ref[...] + pltpu.roll(part, LANES - half, 1) * rs1_ref[...]
                        + pltpu.roll(part, half, 1) * rs2_ref[...])
            o_ref[:, col:col + LANES] = part


def _inproj(x, scale, shift, w_bf16, rope):
    s, d = x.shape
    tm = 512
    row = lambda i: (i, 0)
    const = lambda i: (0, 0)
    return pl.pallas_call(
        _inproj_kernel,
        grid=(s // tm,),
        in_specs=[pl.BlockSpec((tm, d), row), pl.BlockSpec((1, d), const), pl.BlockSpec((1, d), const),
                  pl.BlockSpec((d, IN_WIDTH), const),
                  pl.BlockSpec((tm, LANES), row), pl.BlockSpec((tm, LANES), row), pl.BlockSpec((tm, LANES), row)],
        out_specs=pl.BlockSpec((tm, IN_WIDTH), row),
        out_shape=jax.ShapeDtypeStruct((s, IN_WIDTH), F32),
        compiler_params=_cparams(1),
        name="inproj",
    )(x, scale, shift, w_bf16, *rope)


HG_TILE = 512
HG_CHUNKS = HG_TILE // A_CHUNK
N_HEADS = GROUP_W // HEAD_DIM


def _same_head_ones():
    li = lax.broadcasted_iota(jnp.int32, (GROUP_W, GROUP_W), 0)
    lj = lax.broadcasted_iota(jnp.int32, (GROUP_W, GROUP_W), 1)
    return jnp.where((li >> 6) == (lj >> 6), 1.0, 0.0).astype(BF16)


def _hgrn_core(q_ref, z_ref, v_ref, lb_ref, st_ref, qs, zs, vs, qt, kt, ot, oi_ref, rev):
    nb = HG_CHUNKS
    lb = lb_ref[...]
    for src, dst in ((q_ref, qs), (z_ref, zs), (v_ref, vs)):
        val = src[...]
        dst[0] = val[:, :LANES]
        dst[1] = val[:, LANES:]

    def by_pos(buf, j):
        sel = pl.ds(j, nb, stride=A_CHUNK)
        return jnp.concatenate([buf[0, sel, :], buf[1, sel, :]], axis=1)

    order = list(range(A_CHUNK))[::-1] if rev else list(range(A_CHUNK))
    q, f, k, v = [], [], [], []
    for j in order:
        aq, z = by_pos(qs, j), by_pos(zs, j)
        q.append(_silu(aq))
        f.append(lb + (1.0 - lb) * _sigmoid(z))
        k.append((1.0 - lb) * _sigmoid(-z))
        v.append(by_pos(vs, j))
    incl = [f[0]]
    for p in range(1, A_CHUNK):
        incl.append(incl[-1] * f[p])
    rest = [None] * A_CHUNK
    for p in range(A_CHUNK - 2, -1, -1):
        rest[p] = f[p + 1] if rest[p + 1] is None else rest[p + 1] * f[p + 1]
    dec = incl[-1]

    head = lax.broadcasted_iota(jnp.int32, (1, GROUP_W), 1) >> 6
    lo = lax.broadcasted_iota(jnp.int32, (1, LANES), 1) < HEAD_DIM
    ones_bd = _same_head_ones()

    def put(dst, sel, val):
        dst[0, sel, :] = val[:, :LANES]
        dst[1, sel, :] = val[:, LANES:]

    for p, j in enumerate(order):
        xs, decay = [], None
        for d in range(p + 1):
            x = q[p] * k[p - d]
            xs.append((x if decay is None else x * decay).astype(BF16))
            decay = f[p - d] if decay is None else decay * f[p - d]
        sc = _dot(jnp.concatenate(xs, axis=0), ones_bd)
        o = sc[0:nb] * v[p]
        for d in range(1, p + 1):
            o = o + sc[d * nb:(d + 1) * nb] * v[p - d]
        sel = pl.ds(j, nb, stride=A_CHUNK)
        put(ot, sel, o)
        put(qt, sel, q[p] * incl[p])
        put(kt, sel, k[p] if rest[p] is None else k[p] * rest[p])

    def per_head(x):
        return jnp.concatenate([jnp.where(head == h, x, 0.0) for h in range(N_HEADS)], axis=0).astype(BF16)

    chunks = range(nb - 1, -1, -1) if rev else range(nb)
    state = st_ref[...]
    for n in chunks:
        r1 = slice(n * A_CHUNK, (n + 1) * A_CHUNK)
        lq = per_head(jnp.concatenate([qt[0, r1, :], qt[1, r1, :]], axis=1))
        rk = per_head(jnp.concatenate([kt[0, r1, :], kt[1, r1, :]], axis=1))
        va, vb = v_ref[r1, :LANES], v_ref[r1, LANES:]
        lv = jnp.concatenate([va, pltpu.roll(va, HEAD_DIM, 1), vb, pltpu.roll(vb, HEAD_DIM, 1)], axis=0).astype(BF16)
        sb = state.astype(BF16)
        res = _dot_nt(lq, jnp.concatenate([sb, sb], axis=0))
        oi_ref[r1, :LANES] = jnp.where(lo, res[0:A_CHUNK], res[A_CHUNK:2 * A_CHUNK])
        oi_ref[r1, LANES:] = jnp.where(lo, res[2 * A_CHUNK:3 * A_CHUNK], res[3 * A_CHUNK:])
        kv = _dot_tn(lv, rk)[:HEAD_DIM]
        state = state * dec[n:n + 1] + kv
    st_ref[...] = state
    full = slice(0, HG_TILE)
    return jnp.concatenate([ot[0, full, :], ot[1, full, :]], axis=1) + oi_ref[...]


def _hgrn_fwd_kernel(q_ref, f_ref, i_ref, lb_ref, o_ref, st_ref, *scratch):
    @pl.when(pl.program_id(0) == 0)
    def _():
        st_ref[...] = jnp.zeros_like(st_ref)

    o_ref[...] = _hgrn_core(q_ref, f_ref, i_ref, lb_ref, st_ref, *scratch, rev=False)


def _hgrn_bwd_kernel(q_ref, f_ref, i_ref, g_ref, of_ref, lb_ref, nw_ref, o_ref, st_ref, *scratch):
    @pl.when(pl.program_id(0) == 0)
    def _():
        st_ref[...] = jnp.zeros_like(st_ref)

    o = of_ref[...] + _hgrn_core(q_ref, f_ref, i_ref, lb_ref, st_ref, *scratch, rev=True)
    ones_bd = _same_head_ones()
    sq_hi, sq_lo = _split_bf16(o * o)
    ms = (_dot(sq_hi, ones_bd) + _dot(sq_lo, ones_bd)) * (1.0 / HEAD_DIM)
    o_ref[...] = o * lax.rsqrt(ms + RMS_EPS) * nw_ref[...] * _silu(g_ref[...])


def _hgrn_scratch():
    split = lambda rows: pltpu.VMEM((2, rows, LANES), F32)
    return [pltpu.VMEM((HEAD_DIM, GROUP_W), F32),
            split(HG_TILE), split(HG_TILE), split(HG_TILE),
            split(HG_TILE), split(HG_TILE), split(HG_TILE), pltpu.VMEM((HG_TILE, GROUP_W), F32)]


def _hgrn2(proj, lb, norm_w):
    s = proj.shape[0]
    ts = HG_TILE
    n = s // ts
    col = lambda j: pl.BlockSpec((ts, GROUP_W), lambda i: (i, j))
    rcol = lambda j: pl.BlockSpec((ts, GROUP_W), lambda i: (n - 1 - i, j))
    vec = pl.BlockSpec((1, GROUP_W), lambda i: (0, 0))
    o_f = pl.pallas_call(
        _hgrn_fwd_kernel,
        grid=(n,),
        in_specs=[col(0), col(1), col(3), vec],
        out_specs=pl.BlockSpec((ts, GROUP_W), lambda i: (i, 0)),
        out_shape=jax.ShapeDtypeStruct((s, GROUP_W), F32),
        scratch_shapes=_hgrn_scratch(),
        compiler_params=_cparams(1),
        name="hgrn_fwd",
    )(proj, proj, proj, lb)
    return pl.pallas_call(
        _hgrn_bwd_kernel,
        grid=(n,),
        in_specs=[rcol(0), rcol(2), rcol(3), rcol(4), pl.BlockSpec((ts, GROUP_W), lambda i: (n - 1 - i, 0)),
                  vec, vec],
        out_specs=pl.BlockSpec((ts, GROUP_W), lambda i: (n - 1 - i, 0)),
        out_shape=jax.ShapeDtypeStruct((s, GROUP_W), F32),
        scratch_shapes=_hgrn_scratch(),
        compiler_params=_cparams(1),
        name="hgrn_bwd",
    )(proj, proj, proj, proj, o_f, lb, norm_w)


B_PAD = 16


def _conv_kernel(v_ref, g_ref, vp_ref, gp_ref, vn_ref, gn_ref, w_ref, b_ref, nw_ref, nb_ref, o_ref, ubuf):
    i = pl.program_id(0)
    ts = v_ref.shape[0]
    glu = lambda a, b: a * _sigmoid(b)
    ubuf[0:B_PAD, :] = jnp.where(i > 0, glu(vp_ref[...], gp_ref[...]), 0.0)
    ubuf[B_PAD:B_PAD + ts, :] = glu(v_ref[...], g_ref[...])
    ubuf[B_PAD + ts:, :] = jnp.where(i < pl.num_programs(0) - 1, glu(vn_ref[...], gn_ref[...]), 0.0)
    acc = jnp.zeros((ts, GROUP_W), F32) + b_ref[...]
    for j in range(B_KERNEL):
        acc = acc + ubuf[pl.ds(B_PAD - B_KERNEL // 2 + j, ts), :] * w_ref[j:j + 1, :]
    o_ref[...] = _silu(_layer_norm(acc, nw_ref[...], nb_ref[...]))


def _conv_module(proj, conv_w, conv_b, norm_w, norm_b):
    s = proj.shape[0]
    ts = 512
    n = s // ts
    per = ts // B_PAD
    last = s // B_PAD - 1
    cur = lambda j: pl.BlockSpec((ts, GROUP_W), lambda i: (i, j))
    prev = lambda j: pl.BlockSpec((B_PAD, GROUP_W), lambda i: (jnp.maximum(i * per - 1, 0), j))
    nxt = lambda j: pl.BlockSpec((B_PAD, GROUP_W), lambda i: (jnp.minimum((i + 1) * per, last), j))
    vec = pl.BlockSpec((1, GROUP_W), lambda i: (0, 0))
    wpad = jnp.pad(conv_w, ((0, 1), (0, 0)))
    return pl.pallas_call(
        _conv_kernel,
        grid=(n,),
        in_specs=[cur(5), cur(6), prev(5), prev(6), nxt(5), nxt(6),
                  pl.BlockSpec((B_KERNEL + 1, GROUP_W), lambda i: (0, 0)), vec, vec, vec],
        out_specs=pl.BlockSpec((ts, GROUP_W), lambda i: (i, 0)),
        out_shape=jax.ShapeDtypeStruct((s, GROUP_W), F32),
        scratch_shapes=[pltpu.VMEM((ts + 2 * B_PAD, GROUP_W), F32)],
        compiler_params=_cparams(1),
        name="conv_module",
    )(proj, proj, proj, proj, proj, proj, wpad, conv_b, norm_w, norm_b)


C_MAX_BLOCK = 128


def _dilated_block_sizes(span):
    return sorted({min(span // dil, C_MAX_BLOCK) for _, dil in C_PATTERNS})


def _dilated_kernel(q_ref, kp_ref, kc_ref, kn_ref, vp_ref, vc_ref, vn_ref, o_ref, qw, kw, vw, ob, mb, lb, *band_refs,
                    seq, span):
    i = pl.program_id(0)
    band = dict(zip(_dilated_block_sizes(span), band_refs))

    @pl.when(i == 0)
    def _():
        for nq, ref in band.items():
            a = lax.broadcasted_iota(jnp.int32, ref.shape, 0) & (nq - 1)
            b = lax.broadcasted_iota(jnp.int32, ref.shape, 1)
            ref[...] = jnp.where(jnp.abs(a - (b - C_SIDE)) <= C_SIDE, 0.0, NEG_INF)

    def put(dst, row0, val):
        n = val.shape[0]
        dst[0, row0:row0 + n, :] = val[:, :LANES]
        dst[1, row0:row0 + n, :] = val[:, LANES:]

    put(qw, 0, q_ref[...] * (HEAD_DIM ** -0.5))
    for dst, refs in ((kw, (kp_ref, kc_ref, kn_ref)), (vw, (vp_ref, vc_ref, vn_ref))):
        put(dst, 0, refs[0][...])
        put(dst, C_HALO, refs[1][...])
        put(dst, C_HALO + span, refs[2][...])

    head = lax.broadcasted_iota(jnp.int32, (1, GROUP_W), 1) >> 6
    heads = [head == h for h in range(4)]

    def rows(start, size, dil):
        return pl.ds(start, size) if dil == 1 else pl.ds(start, size, stride=dil)

    def load(src, sel):
        return jnp.concatenate([src[0, sel, :], src[1, sel, :]], axis=1)

    def store(dst, sel, val):
        dst[0, sel, :] = val[:, :LANES]
        dst[1, sel, :] = val[:, LANES:]

    def by_head(x, nq):
        out = jnp.where(heads[0], x[0:nq], 0.0)
        for h in range(1, 4):
            out = out + jnp.where(heads[h], x[h * nq:(h + 1) * nq], 0.0)
        return out

    def attend(first, r, dil, blk, nq):
        nk = nq + 2 * C_SIDE
        qrows = rows(r + dil * blk * nq, nq, dil)
        koff = C_HALO + r + dil * (blk * nq - C_SIDE)
        qs = load(qw, qrows)
        ks = load(kw, rows(koff, nk, dil)).astype(BF16)
        vs = load(vw, rows(koff, nk, dil)).astype(BF16)
        lhs = jnp.concatenate([jnp.where(hm, qs, 0.0) for hm in heads], axis=0).astype(BF16)
        kpos = i * span + (koff - C_HALO) + dil * lax.broadcasted_iota(jnp.int32, (1, nk), 1)
        edge = jnp.where((kpos >= 0) & (kpos < seq), 0.0, NEG_INF)
        sc = _dot_nt(lhs, ks) + band[nq][...] + edge
        m = jnp.max(sc, axis=1, keepdims=True)
        p = jnp.exp(sc - m)
        l = jnp.sum(p, axis=1, keepdims=True)
        o_new = by_head(_dot(p.astype(BF16), vs), nq)
        m_new = by_head(jnp.broadcast_to(m, (4 * nq, GROUP_W)), nq)
        l_new = by_head(jnp.broadcast_to(l, (4 * nq, GROUP_W)), nq)
        if first:
            store(ob, qrows, o_new)
            store(mb, qrows, m_new)
            store(lb, qrows, l_new)
        else:
            m_old = load(mb, qrows)
            m_tot = jnp.maximum(m_old, m_new)
            w_old, w_new = jnp.exp(m_old - m_tot), jnp.exp(m_new - m_tot)
            store(ob, qrows, w_old * load(ob, qrows) + w_new * o_new)
            store(lb, qrows, w_old * load(lb, qrows) + w_new * l_new)
            store(mb, qrows, m_tot)

    for pat, (_, dil) in enumerate(C_PATTERNS):
        sub = span // dil
        nq = min(sub, C_MAX_BLOCK)
        for r in range(dil):
            for blk in range(sub // nq):
                attend(pat == 0, r, dil, blk, nq)
    full = slice(0, span)
    o_ref[...] = load(ob, full) / load(lb, full)


def _dilated_attention(proj):
    s = proj.shape[0]
    span = 1024
    n = s // span
    per = span // C_HALO
    last = s // C_HALO - 1
    cur = lambda j: pl.BlockSpec((span, GROUP_W), lambda i: (i, j))
    prev = lambda j: pl.BlockSpec((C_HALO, GROUP_W), lambda i: (jnp.maximum(i * per - 1, 0), j))
    nxt = lambda j: pl.BlockSpec((C_HALO, GROUP_W), lambda i: (jnp.minimum((i + 1) * per, last), j))
    win = span + 2 * C_HALO
    return pl.pallas_call(
        functools.partial(_dilated_kernel, seq=s, span=span),
        grid=(n,),
        in_specs=[cur(7), prev(8), cur(8), nxt(8), prev(9), cur(9), nxt(9)],
        out_specs=pl.BlockSpec((span, GROUP_W), lambda i: (i, 0)),
        out_shape=jax.ShapeDtypeStruct((s, GROUP_W), F32),
        scratch_shapes=[pltpu.VMEM((2, span, LANES), F32),
                        pltpu.VMEM((2, win, LANES), F32), pltpu.VMEM((2, win, LANES), F32),
                        pltpu.VMEM((2, span, LANES), F32), pltpu.VMEM((2, span, LANES), F32),
                        pltpu.VMEM((2, span, LANES), F32)]
        + [pltpu.VMEM((N_HEADS * nq, nq + 2 * C_SIDE), F32) for nq in _dilated_block_sizes(span)],
        compiler_params=_cparams(1),
        name="dilated_attention",
    )(proj, proj, proj, proj, proj, proj, proj)


def _window_kernel(sink_ref, q_ref, kp_ref, kc_ref, kn_ref, vp_ref, vc_ref, vn_ref, o_ref, band_ref, *, seq):
    i = pl.program_id(0)
    tq = q_ref.shape[0]
    hw = D_HALF_WINDOW
    nk = tq + 2 * hw
    kwin = jnp.concatenate([kp_ref[...], kc_ref[...], kn_ref[...]], axis=0).astype(BF16)
    vwin = jnp.concatenate([vp_ref[...], vc_ref[...], vn_ref[...]], axis=0).astype(BF16)
    lo = lax.broadcasted_iota(jnp.int32, (1, LANES), 1) < HEAD_DIM
    q = q_ref[...] * (HEAD_DIM ** -0.5)
    qa, qb = q[:, :LANES], q[:, LANES:]
    swap = lambda x: pltpu.roll(x, HEAD_DIM, 1)
    lhs = (jnp.concatenate([jnp.where(lo, qa, 0.0), jnp.where(lo, swap(qa), 0.0)], axis=0),
           jnp.concatenate([jnp.where(lo, 0.0, swap(qb)), jnp.where(lo, 0.0, qb)], axis=0))
    @pl.when(i == 0)
    def _():
        a = lax.broadcasted_iota(jnp.int32, (2 * tq, nk), 0) & (tq - 1)
        b = lax.broadcasted_iota(jnp.int32, (2 * tq, nk), 1)
        band_ref[...] = jnp.where(jnp.abs(b - hw - a) <= hw, 0.0, NEG_INF)

    kpos = i * tq - hw + lax.broadcasted_iota(jnp.int32, (1, nk), 1)
    bias = band_ref[...] + jnp.where((kpos >= 0) & (kpos < seq), 0.0, NEG_INF)
    second = lax.broadcasted_iota(jnp.int32, (2 * tq, 1), 0) >= tq
    outs = []
    for g in range(2):
        sc = _dot_nt(lhs[g].astype(BF16), kwin) + bias
        m = jnp.max(sc, axis=1, keepdims=True)
        p = jnp.exp(sc - m)
        l = jnp.sum(p, axis=1, keepdims=True)
        o = _dot(p.astype(BF16), vwin)
        sk = jnp.where(second, sink_ref[2 * g + 1], sink_ref[2 * g])
        m_tot = jnp.maximum(m, sk)
        w = jnp.exp(m - m_tot)
        den = l * w + jnp.exp(sk - m_tot)
        outs.append(o * (w / den))
    o_ref[:, :LANES] = jnp.where(lo, outs[0][:tq], swap(outs[0][tq:]))
    o_ref[:, LANES:] = jnp.where(lo, swap(outs[1][:tq]), outs[1][tq:])


def _window_attention(proj, sink):
    s = proj.shape[0]
    tq = 256
    n = s // tq
    hw = D_HALF_WINDOW
    per = tq // hw
    last = s // hw - 1
    kcol, vcol = 2816 // LANES, 2944 // LANES
    cur = lambda j: pl.BlockSpec((tq, LANES), lambda i: (i, j))
    prev = lambda j: pl.BlockSpec((hw, LANES), lambda i: (jnp.maximum(i * per - 1, 0), j))
    nxt = lambda j: pl.BlockSpec((hw, LANES), lambda i: (jnp.minimum((i + 1) * per, last), j))
    return pl.pallas_call(
        functools.partial(_window_kernel, seq=s),
        grid=(n,),
        in_specs=[pl.BlockSpec(memory_space=pltpu.SMEM),
                  pl.BlockSpec((tq, GROUP_W), lambda i: (i, 10)),
                  prev(kcol), cur(kcol), nxt(kcol), prev(vcol), cur(vcol), nxt(vcol)],
        out_specs=pl.BlockSpec((tq, GROUP_W), lambda i: (i, 0)),
        out_shape=jax.ShapeDtypeStruct((s, GROUP_W), F32),
        scratch_shapes=[pltpu.VMEM((2 * tq, tq + 2 * hw), F32)],
        compiler_params=_cparams(1),
        name="window_attention",
    )(sink, proj, proj, proj, proj, proj, proj, proj)


def _outproj_kernel(ya_ref, yb_ref, yc_ref, yd_ref, w_ref, x_ref, g_ref, lw_ref, lb_ref, o_ref):
    y = None
    for k, ref in enumerate((ya_ref, yb_ref, yc_ref, yd_ref)):
        part = _dot(ref[...].astype(BF16), w_ref[k * GROUP_W:(k + 1) * GROUP_W, :])
        y = part if y is None else y + part
    r = DEEPNORM_ALPHA * x_ref[...] + (1.0 + g_ref[...]) * y
    o_ref[...] = _layer_norm(r, lw_ref[...], lb_ref[...])


def _outproj(ys, w_bf16, x, gate, ln_w, ln_b):
    s, d = x.shape
    tm = 512
    row = lambda i: (i, 0)
    const = lambda i: (0, 0)
    vec = pl.BlockSpec((1, d), const)
    grp = pl.BlockSpec((tm, GROUP_W), row)
    return pl.pallas_call(
        _outproj_kernel,
        grid=(s // tm,),
        in_specs=[grp, grp, grp, grp, pl.BlockSpec((4 * GROUP_W, d), const), pl.BlockSpec((tm, d), row),
                  vec, vec, vec],
        out_specs=pl.BlockSpec((tm, d), row),
        out_shape=jax.ShapeDtypeStruct((s, d), F32),
        compiler_params=_cparams(1),
        name="outproj_ln",
    )(*ys, w_bf16, x, gate, ln_w, ln_b)


FFN_CHUNK = 256


def _ffn_kernel(x_ref, sc_ref, sh_ref, wu_ref, wd_ref, g_ref, lw_ref, lb_ref, o_ref):
    x = x_ref[...]
    h = (x * (1.0 + sc_ref[...]) + sh_ref[...]).astype(BF16)
    acc = jnp.zeros(x.shape, F32)
    for c0 in range(0, FFN_DIM, FFN_CHUNK):
        gate = _dot(h, wu_ref[:, c0:c0 + FFN_CHUNK])
        up = _dot(h, wu_ref[:, FFN_DIM + c0:FFN_DIM + c0 + FFN_CHUNK])
        acc = acc + _dot((_silu(gate) * up).astype(BF16), wd_ref[c0:c0 + FFN_CHUNK, :])
    r = DEEPNORM_ALPHA * x + (1.0 + g_ref[...]) * acc
    o_ref[...] = _layer_norm(r, lw_ref[...], lb_ref[...])


def _dense_ffn(x, scale, shift, wu_bf16, wd_bf16, gate, ln_w, ln_b):
    s, d = x.shape
    tm = 512
    row = lambda i: (i, 0)
    const = lambda i: (0, 0)
    vec = pl.BlockSpec((1, d), const)
    return pl.pallas_call(
        _ffn_kernel,
        grid=(s // tm,),
        in_specs=[pl.BlockSpec((tm, d), row), vec, vec,
                  pl.BlockSpec((d, 2 * FFN_DIM), const), pl.BlockSpec((FFN_DIM, d), const), vec, vec, vec],
        out_specs=pl.BlockSpec((tm, d), row),
        out_shape=jax.ShapeDtypeStruct((s, d), F32),
        compiler_params=_cparams(1),
        name="dense_ffn_ln",
    )(x, scale, shift, wu_bf16, wd_bf16, gate, ln_w, ln_b)


ROW_TILE = 8


def _to_token_tiles(ref, val):
    n = val.shape[0]
    for c in range(ROW_TILE):
        ref[pl.ds(c, n, stride=ROW_TILE), :] = val[:, c * LANES:(c + 1) * LANES]


def _from_token_tiles(ref, n):
    return jnp.concatenate([ref[pl.ds(c, n, stride=ROW_TILE), :] for c in range(ROW_TILE)], axis=1)


def _router_kernel(x_ref, sc_ref, sh_ref, w_ref, h_ref, e_ref, p_ref, c_ref, cnt_ref):
    h = x_ref[...] * (1.0 + sc_ref[...]) + sh_ref[...]
    _to_token_tiles(h_ref, h)
    h_hi, h_lo = _split_bf16(h)
    w_hi, w_lo = _split_bf16(w_ref[...])
    logits = _dot(h_hi, w_hi) + _dot(h_hi, w_lo) + _dot(h_lo, w_hi)
    lane = lax.broadcasted_iota(jnp.int32, logits.shape, 1)
    logits = jnp.where(lane < N_EXPERTS, logits, NEG_INF)
    m1 = jnp.max(logits, axis=1, keepdims=True)
    e1 = jnp.min(jnp.where(logits == m1, lane, LANES), axis=1, keepdims=True)
    rest = jnp.where(lane == e1, NEG_INF, logits)
    m2 = jnp.max(rest, axis=1, keepdims=True)
    e2 = jnp.min(jnp.where(rest == m2, lane, LANES), axis=1, keepdims=True)
    t = jnp.exp(m2 - m1)
    g1 = 1.0 / (1.0 + t)
    p_ref[...] = jnp.where(lane == 0, g1, jnp.where(lane == 1, t * g1, 0.0))

    @pl.when(pl.program_id(0) == 0)
    def _():
        cnt_ref[...] = jnp.zeros_like(cnt_ref)

    tm = logits.shape[0]
    picks = jnp.where((lane == e1) | (lane == e2), 1.0, 0.0)
    rr = lax.broadcasted_iota(jnp.int32, (tm, tm), 0)
    cc = lax.broadcasted_iota(jnp.int32, (tm, tm), 1)
    before = _dot(jnp.where(cc < rr, 1.0, 0.0).astype(BF16), picks.astype(BF16)) + cnt_ref[...]
    r1 = jnp.sum(jnp.where(lane == e1, before, 0.0), axis=1, keepdims=True).astype(jnp.int32)
    r2 = jnp.sum(jnp.where(lane == e2, before, 0.0), axis=1, keepdims=True).astype(jnp.int32)
    cnt_ref[...] += jnp.sum(picks, axis=0, keepdims=True)
    c_ref[...] = jnp.broadcast_to(cnt_ref[...], c_ref.shape)
    e_ref[...] = jnp.where(lane == 0, e1, jnp.where(lane == 1, e2, jnp.where(lane == 2, r1, jnp.where(lane == 3, r2, 0))))


def _router(x, scale, shift, w_router):
    s, d = x.shape
    tm = 512
    row = lambda i: (i, 0)
    const = lambda i: (0, 0)
    vec = pl.BlockSpec((1, d), const)
    wpad = jnp.pad(w_router, ((0, 0), (0, LANES - N_EXPERTS)))
    return pl.pallas_call(
        _router_kernel,
        grid=(s // tm,),
        in_specs=[pl.BlockSpec((tm, d), row), vec, vec, pl.BlockSpec((d, LANES), const)],
        out_specs=[pl.BlockSpec((tm * ROW_TILE, LANES), row), pl.BlockSpec((tm, LANES), row),
                   pl.BlockSpec((tm, LANES), row), pl.BlockSpec((8, LANES), const)],
        out_shape=[jax.ShapeDtypeStruct((s * ROW_TILE, LANES), F32), jax.ShapeDtypeStruct((s, LANES), jnp.int32),
                   jax.ShapeDtypeStruct((s, LANES), F32), jax.ShapeDtypeStruct((8, LANES), F32)],
        scratch_shapes=[pltpu.VMEM((1, LANES), F32)],
        compiler_params=_cparams(1),
        name="moe_router",
    )(x, scale, shift, wpad)


MOE_TILE = 512
MOE_CHUNK = 1792
ROW_UNROLL = 8
TILE_ROWS = MOE_TILE * ROW_TILE


def _expert_kernel(te_ref, nu_ref, tokc_ref, tokn_ref, dst_ref, h_hbm, wg_ref, wu_ref, wd_ref, y_hbm,
                   xbuf, xb16, acc, ybuf, gsem, ssem):
    i, f = pl.program_id(0), pl.program_id(1)
    nt, nf = pl.num_programs(0), pl.num_programs(1)
    n_used = nu_ref[0]
    slot = i & 1
    tail = y_hbm.shape[0] - TILE_ROWS

    def rows_loop(fn):
        def body(g, carry):
            base = pl.multiple_of(g * ROW_UNROLL, ROW_UNROLL)
            for j in range(ROW_UNROLL):
                fn(base + j)
            return carry
        lax.fori_loop(0, MOE_TILE // ROW_UNROLL, body, 0)

    def token(ref, r):
        return ref.at[pl.ds(pl.multiple_of(r, ROW_TILE), ROW_TILE)]

    def gather_rows(ids_ref, s):
        rows_loop(lambda r: pltpu.make_async_copy(token(h_hbm, ids_ref[0, r]), token(xbuf.at[s], r * ROW_TILE),
                                                  gsem.at[s]).start())

    def gather_wait(s):
        pltpu.make_async_copy(h_hbm.at[pl.ds(0, TILE_ROWS)], xbuf.at[s], gsem.at[s]).wait()

    def scatter_rows(s):
        rows_loop(lambda r: pltpu.make_async_copy(token(ybuf.at[s], r * ROW_TILE), token(y_hbm, dst_ref[0, r]),
                                                  ssem.at[s]).start())

    def scatter_wait(s):
        pltpu.make_async_copy(ybuf.at[s], y_hbm.at[pl.ds(0, TILE_ROWS)], ssem.at[s]).wait()

    @pl.when(f == 0)
    def _():
        @pl.when(i == 0)
        def _():
            ybuf[1] = jnp.zeros(ybuf.shape[1:], F32)
            zero_tail = pltpu.make_async_copy(ybuf.at[1], y_hbm.at[pl.ds(tail, TILE_ROWS)], ssem.at[1])
            zero_tail.start()
            zero_tail.wait()
            gather_rows(tokc_ref, 0)

        @pl.when(i + 1 < n_used)
        def _():
            gather_rows(tokn_ref, 1 - slot)

        @pl.when(i < n_used)
        def _():
            gather_wait(slot)
            xb16[...] = _from_token_tiles(xbuf.at[slot], MOE_TILE).astype(BF16)

    @pl.when(i < n_used)
    def _():
        x = xb16[...]
        act = _silu(_dot(x, wg_ref[...])) * _dot(x, wu_ref[...])
        part = _dot(act.astype(BF16), wd_ref[...])

        @pl.when(f == 0)
        def _():
            acc[...] = part

        @pl.when((f > 0) & (f < nf - 1))
        def _():
            acc[...] += part

        @pl.when(f == nf - 1)
        def _():
            _to_token_tiles(ybuf.at[slot], acc[...] + part)

    @pl.when(f == nf - 1)
    def _():
        @pl.when((i >= 1) & (i - 1 < n_used))
        def _():
            scatter_wait(1 - slot)

        @pl.when(i < n_used)
        def _():
            scatter_rows(slot)

        @pl.when((i == nt - 1) & (i < n_used))
        def _():
            scatter_wait(slot)


def _experts(h, slot_tok, slot_dst, tile_expert, n_used, wu_bf16, wd_bf16, n_rows_out):
    d = D_MODEL
    n_tiles = slot_tok.shape[0] // MOE_TILE
    nf = EXPERT_DIM // MOE_CHUNK
    assert nf >= 2
    fidx = lambda i, f, nu: jnp.where(i < nu[0], f, nf - 1)
    ids = lambda fn: pl.BlockSpec((None, 1, MOE_TILE), lambda i, f, te, nu: (fn(i), 0, 0), memory_space=pltpu.SMEM)
    grid_spec = pltpu.PrefetchScalarGridSpec(
        num_scalar_prefetch=2,
        grid=(n_tiles, nf),
        in_specs=[ids(lambda i: i), ids(lambda i: jnp.minimum(i + 1, n_tiles - 1)), ids(lambda i: i),
                  pl.BlockSpec(memory_space=pl.ANY),
                  pl.BlockSpec((None, d, MOE_CHUNK), lambda i, f, te, nu: (te[i], 0, fidx(i, f, nu))),
                  pl.BlockSpec((None, d, MOE_CHUNK), lambda i, f, te, nu: (te[i], 0, nf + fidx(i, f, nu))),
                  pl.BlockSpec((None, MOE_CHUNK, d), lambda i, f, te, nu: (te[i], fidx(i, f, nu), 0))],
        out_specs=pl.BlockSpec(memory_space=pl.ANY),
        scratch_shapes=[pltpu.VMEM((2, TILE_ROWS, LANES), F32), pltpu.VMEM((MOE_TILE, d), BF16),
                        pltpu.VMEM((MOE_TILE, d), F32), pltpu.VMEM((2, TILE_ROWS, LANES), F32),
                        pltpu.SemaphoreType.DMA((2,)), pltpu.SemaphoreType.DMA((2,))],
    )
    tok3 = (slot_tok * ROW_TILE).reshape(n_tiles, 1, MOE_TILE)
    dst3 = (slot_dst * ROW_TILE).reshape(n_tiles, 1, MOE_TILE)
    return pl.pallas_call(
        _expert_kernel,
        grid_spec=grid_spec,
        out_shape=jax.ShapeDtypeStruct((n_rows_out * ROW_TILE, LANES), F32),
        compiler_params=_cparams(2),
        name="moe_experts",
    )(tile_expert, n_used, tok3, tok3, dst3, h, wu_bf16, wu_bf16, wd_bf16)


def _combine_kernel(x_ref, y1_ref, y2_ref, p_ref, g_ref, lw_ref, lb_ref, o_ref):
    p = p_ref[...]
    n = x_ref.shape[0]
    f = p[:, 0:1] * _from_token_tiles(y1_ref, n) + p[:, 1:2] * _from_token_tiles(y2_ref, n)
    r = DEEPNORM_ALPHA * x_ref[...] + (1.0 + g_ref[...]) * f
    o_ref[...] = _layer_norm(r, lw_ref[...], lb_ref[...])


def _combine(x, y, probs, gate, ln_w, ln_b):
    s, d = x.shape
    tm = 512
    row = lambda i: (i, 0)
    vec = pl.BlockSpec((1, d), lambda i: (0, 0))
    big = pl.BlockSpec((tm, d), row)
    return pl.pallas_call(
        _combine_kernel,
        grid=(s // tm,),
        in_specs=[big, pl.BlockSpec((tm * ROW_TILE, LANES), row),
                  pl.BlockSpec((tm * ROW_TILE, LANES), lambda i: (s // tm + i, 0)),
                  pl.BlockSpec((tm, LANES), row), vec, vec, vec],
        out_specs=big,
        out_shape=jax.ShapeDtypeStruct((s, d), F32),
        compiler_params=_cparams(1),
        name="moe_combine_ln",
    )(x, y, y, probs, gate, ln_w, ln_b)


def _moe(x, scale, shift, w_router, wu_bf16, wd_bf16, gate, ln_w, ln_b):
    t, d = x.shape
    h, sel, probs, cnt = _router(x, scale, shift, w_router)
    flat_e = sel[:, :TOP_K].reshape(-1)
    rank = sel[:, TOP_K:2 * TOP_K].reshape(-1)
    counts = cnt[0, :N_EXPERTS].astype(jnp.int32)
    n_assign = t * TOP_K
    n_tiles = -(-n_assign // MOE_TILE) + N_EXPERTS
    n_slots = n_tiles * MOE_TILE
    padded = (counts + MOE_TILE - 1) // MOE_TILE * MOE_TILE
    padded_end = jnp.cumsum(padded)
    slot = (padded_end - padded)[flat_e] + rank
    assign = jnp.full((n_slots,), -1, jnp.int32).at[slot].set(jnp.arange(n_assign, dtype=jnp.int32))
    used = assign >= 0
    slot_tok = jnp.where(used, assign >> 1, 0)
    slot_dst = jnp.where(used, (assign & 1) * t + (assign >> 1),
                         n_assign + (jnp.arange(n_slots, dtype=jnp.int32) & (MOE_TILE - 1)))
    n_used = (padded_end[-1] // MOE_TILE).astype(jnp.int32).reshape(1)
    tile_ids = jnp.minimum(jnp.arange(n_tiles, dtype=jnp.int32), n_used[0] - 1)
    tile_expert = jnp.minimum(jnp.searchsorted(padded_end, tile_ids * MOE_TILE, side="right"),
                              N_EXPERTS - 1).astype(jnp.int32)
    y = _experts(h, slot_tok, slot_dst, tile_expert, n_used, wu_bf16, wd_bf16, n_assign + MOE_TILE)
    return _combine(x, y, probs, gate, ln_w, ln_b)


def kernel(x, c, positions, w_ada, b_ada, w_in, w_out, a_lower_bound, a_norm_w, b_conv_w, b_conv_b, b_norm_w,
           b_norm_b, d_sink, ln_w, ln_b, ffn_w_up, ffn_w_down, moe_router, moe_w_up, moe_w_down):
    batch, s, d = x.shape
    assert batch == 1 and d == D_MODEL
    x = x.reshape(s, d)
    mod = _adaln(c, w_ada, b_ada)
    rope = _rope_tables(positions)
    lb_cum = jnp.cumsum(jax.nn.softmax(a_lower_bound.astype(F32), axis=0), axis=0)
    lb_all = lb_cum - lb_cum[0]
    row = lambda v: v.reshape(1, -1)
    for layer in range(DEPTH):
        shift1, scale1, gate1, shift2, scale2, gate2 = [mod[layer, :, k * d:(k + 1) * d] for k in range(6)]
        proj = _inproj(x, scale1, shift1, w_in[layer].astype(BF16), rope)
        ya = _hgrn2(proj, row(lb_all[layer]), row(jnp.tile(a_norm_w[layer], GROUP_W // HEAD_DIM)))
        yb = _conv_module(proj, b_conv_w[layer], row(b_conv_b[layer]), row(b_norm_w[layer]), row(b_norm_b[layer]))
        yc = _dilated_attention(proj)
        yd = _window_attention(proj, d_sink[layer])
        x = _outproj((ya, yb, yc, yd), w_out[layer].astype(BF16), x, gate1, row(ln_w[layer, 0]), row(ln_b[layer, 0]))
        lw, lbias = row(ln_w[layer, 1]), row(ln_b[layer, 1])
        if layer % 2 == 0:
            x = _dense_ffn(x, scale2, shift2, ffn_w_up[layer // 2].astype(BF16),
                           ffn_w_down[layer // 2].astype(BF16), gate2, lw, lbias)
        else:
            x = _moe(x, scale2, shift2, moe_router[layer // 2], moe_w_up[layer // 2].astype(BF16),
                     moe_w_down[layer // 2].astype(BF16), gate2, lw, lbias)
    return x.reshape(batch, s, d)
```

```python
import functools

import numpy as np
import jax
import jax.numpy as jnp
from jax import lax
from jax.experimental import pallas as pl
from jax.experimental.pallas import tpu as pltpu

F32 = jnp.float32
BF16 = jnp.bfloat16

D_MODEL = 1024
DEPTH = 2
HEAD_DIM = 64
A_CHUNK = 16
B_KERNEL = 31
GROUP_W = 256
C_PATTERNS = ((128, 1), (512, 4), (2048, 16))
C_SIDE = 64
C_HALO = 1024
D_HALF_WINDOW = 128
ROPE_THETA = 500000.0
ROPE_DIM = HEAD_DIM // 4
FFN_DIM = 2816
N_EXPERTS = 8
TOP_K = 2
EXPERT_DIM = 3584
IN_WIDTH = 3072
DEEPNORM_ALPHA = (2 * DEPTH) ** 0.25
LN_EPS = 1e-5
RMS_EPS = 1e-6
NEG_INF = -1e30

LANES = 128
SUBLANES = 8
VMEM_LIMIT = 56 * 1024 * 1024


def _cparams(n_axes=1, vmem=VMEM_LIMIT):
    return pltpu.CompilerParams(dimension_semantics=("arbitrary",) * n_axes, vmem_limit_bytes=vmem)


def _sigmoid(x):
    return 1.0 / (1.0 + jnp.exp(-x))


def _silu(x):
    return x * _sigmoid(x)


def _split_bf16(x):
    hi = x.astype(BF16)
    lo = (x - hi.astype(F32)).astype(BF16)
    return hi, lo


def _layer_norm(r, w, b):
    mu = jnp.mean(r, axis=-1, keepdims=True)
    d = r - mu
    var = jnp.mean(d * d, axis=-1, keepdims=True)
    return d * lax.rsqrt(var + LN_EPS) * w + b


def _dot_nt(a, b):
    return lax.dot_general(a, b, (((1,), (1,)), ((), ())), preferred_element_type=F32)


def _dot_tn(a, b):
    return lax.dot_general(a, b, (((0,), (0,)), ((), ())), preferred_element_type=F32)


def _dot(a, b):
    return jnp.dot(a, b, preferred_element_type=F32)


def _adaln_kernel(c_ref, w_ref, b_ref, o_ref):
    a_hi, a_lo = _split_bf16(_silu(c_ref[...]))
    w_hi, w_lo = _split_bf16(w_ref[...])
    o_ref[...] = _dot(a_hi, w_hi) + _dot(a_hi, w_lo) + _dot(a_lo, w_hi) + b_ref[...]


def _adaln(c, w_ada, b_ada):
    depth, d, n = w_ada.shape
    tn = 1536
    c8 = jnp.broadcast_to(c.reshape(1, d), (8, d))
    out = pl.pallas_call(
        _adaln_kernel,
        grid=(depth, n // tn),
        in_specs=[pl.BlockSpec((8, d), lambda l, j: (0, 0)),
                  pl.BlockSpec((None, d, tn), lambda l, j: (l, 0, j)),
                  pl.BlockSpec((None, 1, tn), lambda l, j: (l, 0, j))],
        out_specs=pl.BlockSpec((None, 8, tn), lambda l, j: (l, 0, j)),
        out_shape=jax.ShapeDtypeStruct((depth, 8, n), F32),
        compiler_params=_cparams(2),
        name="adaln",
    )(c8, w_ada, b_ada.reshape(depth, 1, n))
    return out[:, 0:1, :]


def _rope_kernel(pos_ref, invf_ref, c_ref, s1_ref, s2_ref):
    ang = pos_ref[...].astype(F32) * invf_ref[...]
    j = lax.broadcasted_iota(jnp.int32, ang.shape, 1) & (HEAD_DIM - 1)
    cosv, sinv = jnp.cos(ang), jnp.sin(ang)
    half = ROPE_DIM // 2
    c_ref[...] = jnp.where(j < ROPE_DIM, cosv, 1.0)
    s1_ref[...] = jnp.where(j < half, -sinv, 0.0)
    s2_ref[...] = jnp.where((j >= half) & (j < ROPE_DIM), sinv, 0.0)


def _rope_tables(positions):
    s = positions.shape[-1]
    ts = 2048
    half = ROPE_DIM // 2
    inv_freq = np.float32(ROPE_THETA) ** (-np.arange(half, dtype=np.float32) * np.float32(2.0 / ROPE_DIM))
    invf = jnp.asarray(inv_freq[(np.arange(LANES) % HEAD_DIM) % half].reshape(1, LANES).astype(np.float32))
    spec = pl.BlockSpec((ts, LANES), lambda i: (i, 0))
    return pl.pallas_call(
        _rope_kernel,
        grid=(s // ts,),
        in_specs=[pl.BlockSpec((ts, 1), lambda i: (i, 0)), pl.BlockSpec((1, LANES), lambda i: (0, 0))],
        out_specs=[spec, spec, spec],
        out_shape=[jax.ShapeDtypeStruct((s, LANES), F32)] * 3,
        compiler_params=_cparams(1),
        name="rope_tables",
    )(positions.reshape(s, 1), invf)


_ROPE_COL_CHUNKS = (1792, 1920, 2048, 2176, 2560, 2688, 2816)


def _inproj_kernel(x_ref, sc_ref, sh_ref, w_ref, rc_ref, rs1_ref, rs2_ref, o_ref):
    h = (x_ref[...] * (1.0 + sc_ref[...]) + sh_ref[...]).astype(BF16)
    half = ROPE_DIM // 2
    for c0 in range(0, IN_WIDTH, GROUP_W):
        acc = _dot(h, w_ref[:, c0:c0 + GROUP_W])
        for k in range(GROUP_W // LANES):
            col = c0 + k * LANES
            part = acc[:, k * LANES:(k + 1) * LANES]
            if col in _ROPE_COL_CHUNKS:
                part = (part * rc_ref[...] + pltpu.roll(part, LANES - half, 1) * rs1_ref[...]
                        + pltpu.roll(part, half, 1) * rs2_ref[...])
            o_ref[:, col:col + LANES] = part


def _inproj(x, scale, shift, w_bf16, rope):
    s, d = x.shape
    tm = 512
    row = lambda i: (i, 0)
    const = lambda i: (0, 0)
    return pl.pallas_call(
        _inproj_kernel,
        grid=(s // tm,),
        in_specs=[pl.BlockSpec((tm, d), row), pl.BlockSpec((1, d), const), pl.BlockSpec((1, d), const),
                  pl.BlockSpec((d, IN_WIDTH), const),
                  pl.BlockSpec((tm, LANES), row), pl.BlockSpec((tm, LANES), row), pl.BlockSpec((tm, LANES), row)],
        out_specs=pl.BlockSpec((tm, IN_WIDTH), row),
        out_shape=jax.ShapeDtypeStruct((s, IN_WIDTH), F32),
        compiler_params=_cparams(1),
        name="inproj",
    )(x, scale, shift, w_bf16, *rope)


HG_TILE = 512
HG_CHUNKS = HG_TILE // A_CHUNK
N_HEADS = GROUP_W // HEAD_DIM


def _same_head_ones():
    li = lax.broadcasted_iota(jnp.int32, (GROUP_W, GROUP_W), 0)
    lj = lax.broadcasted_iota(jnp.int32, (GROUP_W, GROUP_W), 1)
    return jnp.where((li >> 6) == (lj >> 6), 1.0, 0.0).astype(BF16)


def _hgrn_core(q_ref, z_ref, v_ref, lb_ref, st_ref, qs, zs, vs, qt, kt, ot, oi_ref, rev):
    nb = HG_CHUNKS
    lb = lb_ref[...]
    for src, dst in ((q_ref, qs), (z_ref, zs), (v_ref, vs)):
        val = src[...]
        dst[0] = val[:, :LANES]
        dst[1] = val[:, LANES:]

    def by_pos(buf, j):
        sel = pl.ds(j, nb, stride=A_CHUNK)
        return jnp.concatenate([buf[0, sel, :], buf[1, sel, :]], axis=1)

    order = list(range(A_CHUNK))[::-1] if rev else list(range(A_CHUNK))
    q, f, k, v = [], [], [], []
    for j in order:
        aq, z = by_pos(qs, j), by_pos(zs, j)
        q.append(_silu(aq))
        f.append(lb + (1.0 - lb) * _sigmoid(z))
        k.append((1.0 - lb) * _sigmoid(-z))
        v.append(by_pos(vs, j))
    incl = [f[0]]
    for p in range(1, A_CHUNK):
        incl.append(incl[-1] * f[p])
    rest = [None] * A_CHUNK
    for p in range(A_CHUNK - 2, -1, -1):
        rest[p] = f[p + 1] if rest[p + 1] is None else rest[p + 1] * f[p + 1]
    dec = incl[-1]

    head = lax.broadcasted_iota(jnp.int32, (1, GROUP_W), 1) >> 6
    lo = lax.broadcasted_iota(jnp.int32, (1, LANES), 1) < HEAD_DIM
    ones_bd = _same_head_ones()

    def put(dst, sel, val):
        dst[0, sel, :] = val[:, :LANES]
        dst[1, sel, :] = val[:, LANES:]

    for p, j in enumerate(order):
        xs, decay = [], None
        for d in range(p + 1):
            x = q[p] * k[p - d]
            xs.append((x if decay is None else x * decay).astype(BF16))
            decay = f[p - d] if decay is None else decay * f[p - d]
        sc = _dot(jnp.concatenate(xs, axis=0), ones_bd)
        o = sc[0:nb] * v[p]
        for d in range(1, p + 1):
            o = o + sc[d * nb:(d + 1) * nb] * v[p - d]
        sel = pl.ds(j, nb, stride=A_CHUNK)
        put(ot, sel, o)
        put(qt, sel, q[p] * incl[p])
        put(kt, sel, k[p] if rest[p] is None else k[p] * rest[p])

    def per_head(x):
        return jnp.concatenate([jnp.where(head == h, x, 0.0) for h in range(N_HEADS)], axis=0).astype(BF16)

    chunks = range(nb - 1, -1, -1) if rev else range(nb)
    state = st_ref[...]
    for n in chunks:
        r1 = slice(n * A_CHUNK, (n + 1) * A_CHUNK)
        lq = per_head(jnp.concatenate([qt[0, r1, :], qt[1, r1, :]], axis=1))
        rk = per_head(jnp.concatenate([kt[0, r1, :], kt[1, r1, :]], axis=1))
        va, vb = v_ref[r1, :LANES], v_ref[r1, LANES:]
        lv = jnp.concatenate([va, pltpu.roll(va, HEAD_DIM, 1), vb, pltpu.roll(vb, HEAD_DIM, 1)], axis=0).astype(BF16)
        sb = state.astype(BF16)
        res = _dot_nt(lq, jnp.concatenate([sb, sb], axis=0))
        oi_ref[r1, :LANES] = jnp.where(lo, res[0:A_CHUNK], res[A_CHUNK:2 * A_CHUNK])
        oi_ref[r1, LANES:] = jnp.where(lo, res[2 * A_CHUNK:3 * A_CHUNK], res[3 * A_CHUNK:])
        kv = _dot_tn(lv, rk)[:HEAD_DIM]
        state = state * dec[n:n + 1] + kv
    st_ref[...] = state
    full = slice(0, HG_TILE)
    return jnp.concatenate([ot[0, full, :], ot[1, full, :]], axis=1) + oi_ref[...]


def _hgrn_fwd_kernel(q_ref, f_ref, i_ref, lb_ref, o_ref, st_ref, *scratch):
    @pl.when(pl.program_id(0) == 0)
    def _():
        st_ref[...] = jnp.zeros_like(st_ref)

    o_ref[...] = _hgrn_core(q_ref, f_ref, i_ref, lb_ref, st_ref, *scratch, rev=False)


def _hgrn_bwd_kernel(q_ref, f_ref, i_ref, g_ref, of_ref, lb_ref, nw_ref, o_ref, st_ref, *scratch):
    @pl.when(pl.program_id(0) == 0)
    def _():
        st_ref[...] = jnp.zeros_like(st_ref)

    o = of_ref[...] + _hgrn_core(q_ref, f_ref, i_ref, lb_ref, st_ref, *scratch, rev=True)
    ones_bd = _same_head_ones()
    sq_hi, sq_lo = _split_bf16(o * o)
    ms = (_dot(sq_hi, ones_bd) + _dot(sq_lo, ones_bd)) * (1.0 / HEAD_DIM)
    o_ref[...] = o * lax.rsqrt(ms + RMS_EPS) * nw_ref[...] * _silu(g_ref[...])


def _hgrn_scratch():
    split = lambda rows: pltpu.VMEM((2, rows, LANES), F32)
    return [pltpu.VMEM((HEAD_DIM, GROUP_W), F32),
            split(HG_TILE), split(HG_TILE), split(HG_TILE),
            split(HG_TILE), split(HG_TILE), split(HG_TILE), pltpu.VMEM((HG_TILE, GROUP_W), F32)]


def _hgrn2(proj, lb, norm_w):
    s = proj.shape[0]
    ts = HG_TILE
    n = s // ts
    col = lambda j: pl.BlockSpec((ts, GROUP_W), lambda i: (i, j))
    rcol = lambda j: pl.BlockSpec((ts, GROUP_W), lambda i: (n - 1 - i, j))
    vec = pl.BlockSpec((1, GROUP_W), lambda i: (0, 0))
    o_f = pl.pallas_call(
        _hgrn_fwd_kernel,
        grid=(n,),
        in_specs=[col(0), col(1), col(3), vec],
        out_specs=pl.BlockSpec((ts, GROUP_W), lambda i: (i, 0)),
        out_shape=jax.ShapeDtypeStruct((s, GROUP_W), F32),
        scratch_shapes=_hgrn_scratch(),
        compiler_params=_cparams(1),
        name="hgrn_fwd",
    )(proj, proj, proj, lb)
    return pl.pallas_call(
        _hgrn_bwd_kernel,
        grid=(n,),
        in_specs=[rcol(0), rcol(2), rcol(3), rcol(4), pl.BlockSpec((ts, GROUP_W), lambda i: (n - 1 - i, 0)),
                  vec, vec],
        out_specs=pl.BlockSpec((ts, GROUP_W), lambda i: (n - 1 - i, 0)),
        out_shape=jax.ShapeDtypeStruct((s, GROUP_W), F32),
        scratch_shapes=_hgrn_scratch(),
        compiler_params=_cparams(1),
        name="hgrn_bwd",
    )(proj, proj, proj, proj, o_f, lb, norm_w)


B_PAD = 16


def _conv_kernel(v_ref, g_ref, vp_ref, gp_ref, vn_ref, gn_ref, w_ref, b_ref, nw_ref, nb_ref, o_ref, ubuf):
    i = pl.program_id(0)
    ts = v_ref.shape[0]
    glu = lambda a, b: a * _sigmoid(b)
    ubuf[0:B_PAD, :] = jnp.where(i > 0, glu(vp_ref[...], gp_ref[...]), 0.0)
    ubuf[B_PAD:B_PAD + ts, :] = glu(v_ref[...], g_ref[...])
    ubuf[B_PAD + ts:, :] = jnp.where(i < pl.num_programs(0) - 1, glu(vn_ref[...], gn_ref[...]), 0.0)
    acc = jnp.zeros((ts, GROUP_W), F32) + b_ref[...]
    span = ts + 2 * B_PAD - SUBLANES
    shifted = [ubuf[pl.ds(r, span), :] for r in range(SUBLANES)]
    for j in range(B_KERNEL):
        off = B_PAD - B_KERNEL // 2 + j
        base = off - off % SUBLANES
        acc = acc + shifted[off % SUBLANES][base:base + ts] * w_ref[j:j + 1, :]
    o_ref[...] = _silu(_layer_norm(acc, nw_ref[...], nb_ref[...]))


def _conv_module(proj, conv_w, conv_b, norm_w, norm_b):
    s = proj.shape[0]
    ts = 512
    n = s // ts
    per = ts // B_PAD
    last = s // B_PAD - 1
    cur = lambda j: pl.BlockSpec((ts, GROUP_W), lambda i: (i, j))
    prev = lambda j: pl.BlockSpec((B_PAD, GROUP_W), lambda i: (jnp.maximum(i * per - 1, 0), j))
    nxt = lambda j: pl.BlockSpec((B_PAD, GROUP_W), lambda i: (jnp.minimum((i + 1) * per, last), j))
    vec = pl.BlockSpec((1, GROUP_W), lambda i: (0, 0))
    wpad = jnp.pad(conv_w, ((0, 1), (0, 0)))
    return pl.pallas_call(
        _conv_kernel,
        grid=(n,),
        in_specs=[cur(5), cur(6), prev(5), prev(6), nxt(5), nxt(6),
                  pl.BlockSpec((B_KERNEL + 1, GROUP_W), lambda i: (0, 0)), vec, vec, vec],
        out_specs=pl.BlockSpec((ts, GROUP_W), lambda i: (i, 0)),
        out_shape=jax.ShapeDtypeStruct((s, GROUP_W), F32),
        scratch_shapes=[pltpu.VMEM((ts + 2 * B_PAD, GROUP_W), F32)],
        compiler_params=_cparams(1),
        name="conv_module",
    )(proj, proj, proj, proj, proj, proj, wpad, conv_b, norm_w, norm_b)


C_MAX_BLOCK = 128


def _dilated_block_sizes(span):
    return sorted({min(span // dil, C_MAX_BLOCK) for _, dil in C_PATTERNS})


def _dilated_kernel(q_ref, kp_ref, kc_ref, kn_ref, vp_ref, vc_ref, vn_ref, o_ref, qw, kw, vw, ob, mb, lb, *band_refs,
                    seq, span):
    i = pl.program_id(0)
    band = dict(zip(_dilated_block_sizes(span), band_refs))

    @pl.when(i == 0)
    def _():
        for nq, ref in band.items():
            a = lax.broadcasted_iota(jnp.int32, ref.shape, 0) & (nq - 1)
            b = lax.broadcasted_iota(jnp.int32, ref.shape, 1)
            ref[...] = jnp.where(jnp.abs(a - (b - C_SIDE)) <= C_SIDE, 0.0, NEG_INF)

    def put(dst, row0, val):
        n = val.shape[0]
        dst[0, row0:row0 + n, :] = val[:, :LANES]
        dst[1, row0:row0 + n, :] = val[:, LANES:]

    put(qw, 0, q_ref[...] * (HEAD_DIM ** -0.5))
    for dst, refs in ((kw, (kp_ref, kc_ref, kn_ref)), (vw, (vp_ref, vc_ref, vn_ref))):
        put(dst, 0, refs[0][...])
        put(dst, C_HALO, refs[1][...])
        put(dst, C_HALO + span, refs[2][...])

    head = lax.broadcasted_iota(jnp.int32, (1, GROUP_W), 1) >> 6
    heads = [head == h for h in range(4)]

    def rows(start, size, dil):
        return pl.ds(start, size) if dil == 1 else pl.ds(start, size, stride=dil)

    def load(src, sel):
        return jnp.concatenate([src[0, sel, :], src[1, sel, :]], axis=1)

    def store(dst, sel, val):
        dst[0, sel, :] = val[:, :LANES]
        dst[1, sel, :] = val[:, LANES:]

    def by_head(x, nq):
        out = jnp.where(heads[0], x[0:nq], 0.0)
        for h in range(1, 4):
            out = out + jnp.where(heads[h], x[h * nq:(h + 1) * nq], 0.0)
        return out

    def attend(first, r, dil, blk, nq):
        nk = nq + 2 * C_SIDE
        qrows = rows(r + dil * blk * nq, nq, dil)
        koff = C_HALO + r + dil * (blk * nq - C_SIDE)
        qs = load(qw, qrows)
        ks = load(kw, rows(koff, nk, dil)).astype(BF16)
        vs = load(vw, rows(koff, nk, dil)).astype(BF16)
        lhs = jnp.concatenate([jnp.where(hm, qs, 0.0) for hm in heads], axis=0).astype(BF16)
        kpos = i * span + (koff - C_HALO) + dil * lax.broadcasted_iota(jnp.int32, (1, nk), 1)
        edge = jnp.where((kpos >= 0) & (kpos < seq), 0.0, NEG_INF)
        sc = _dot_nt(lhs, ks) + band[nq][...] + edge
        m = jnp.max(sc, axis=1, keepdims=True)
        p = jnp.exp(sc - m)
        l = jnp.sum(p, axis=1, keepdims=True)
        o_new = by_head(_dot(p.astype(BF16), vs), nq)
        m_new = by_head(jnp.broadcast_to(m, (4 * nq, GROUP_W)), nq)
        l_new = by_head(jnp.broadcast_to(l, (4 * nq, GROUP_W)), nq)
        if first:
            store(ob, qrows, o_new)
            store(mb, qrows, m_new)
            store(lb, qrows, l_new)
        else:
            m_old = load(mb, qrows)
            m_tot = jnp.maximum(m_old, m_new)
            w_old, w_new = jnp.exp(m_old - m_tot), jnp.exp(m_new - m_tot)
            store(ob, qrows, w_old * load(ob, qrows) + w_new * o_new)
            store(lb, qrows, w_old * load(lb, qrows) + w_new * l_new)
            store(mb, qrows, m_tot)

    for pat, (_, dil) in enumerate(C_PATTERNS):
        sub = span // dil
        nq = min(sub, C_MAX_BLOCK)
        for r in range(dil):
            for blk in range(sub // nq):
                attend(pat == 0, r, dil, blk, nq)
    full = slice(0, span)
    o_ref[...] = load(ob, full) / load(lb, full)


def _dilated_attention(proj):
    s = proj.shape[0]
    span = 1024
    n = s // span
    per = span // C_HALO
    last = s // C_HALO - 1
    cur = lambda j: pl.BlockSpec((span, GROUP_W), lambda i: (i, j))
    prev = lambda j: pl.BlockSpec((C_HALO, GROUP_W), lambda i: (jnp.maximum(i * per - 1, 0), j))
    nxt = lambda j: pl.BlockSpec((C_HALO, GROUP_W), lambda i: (jnp.minimum((i + 1) * per, last), j))
    win = span + 2 * C_HALO
    return pl.pallas_call(
        functools.partial(_dilated_kernel, seq=s, span=span),
        grid=(n,),
        in_specs=[cur(7), prev(8), cur(8), nxt(8), prev(9), cur(9), nxt(9)],
        out_specs=pl.BlockSpec((span, GROUP_W), lambda i: (i, 0)),
        out_shape=jax.ShapeDtypeStruct((s, GROUP_W), F32),
        scratch_shapes=[pltpu.VMEM((2, span, LANES), F32),
                        pltpu.VMEM((2, win, LANES), F32), pltpu.VMEM((2, win, LANES), F32),
                        pltpu.VMEM((2, span, LANES), F32), pltpu.VMEM((2, span, LANES), F32),
                        pltpu.VMEM((2, span, LANES), F32)]
        + [pltpu.VMEM((N_HEADS * nq, nq + 2 * C_SIDE), F32) for nq in _dilated_block_sizes(span)],
        compiler_params=_cparams(1),
        name="dilated_attention",
    )(proj, proj, proj, proj, proj, proj, proj)


def _window_kernel(sink_ref, q_ref, kp_ref, kc_ref, kn_ref, vp_ref, vc_ref, vn_ref, o_ref, band_ref, *, seq):
    i = pl.program_id(0)
    tq = q_ref.shape[0]
    hw = D_HALF_WINDOW
    nk = tq + 2 * hw
    kwin = jnp.concatenate([kp_ref[...], kc_ref[...], kn_ref[...]], axis=0).astype(BF16)
    vwin = jnp.concatenate([vp_ref[...], vc_ref[...], vn_ref[...]], axis=0).astype(BF16)
    lo = lax.broadcasted_iota(jnp.int32, (1, LANES), 1) < HEAD_DIM
    q = q_ref[...] * (HEAD_DIM ** -0.5)
    qa, qb = q[:, :LANES], q[:, LANES:]
    swap = lambda x: pltpu.roll(x, HEAD_DIM, 1)
    lhs = (jnp.concatenate([jnp.where(lo, qa, 0.0), jnp.where(lo, swap(qa), 0.0)], axis=0),
           jnp.concatenate([jnp.where(lo, 0.0, swap(qb)), jnp.where(lo, 0.0, qb)], axis=0))
    @pl.when(i == 0)
    def _():
        a = lax.broadcasted_iota(jnp.int32, (2 * tq, nk), 0) & (tq - 1)
        b = lax.broadcasted_iota(jnp.int32, (2 * tq, nk), 1)
        band_ref[...] = jnp.where(jnp.abs(b - hw - a) <= hw, 0.0, NEG_INF)

    kpos = i * tq - hw + lax.broadcasted_iota(jnp.int32, (1, nk), 1)
    bias = band_ref[...] + jnp.where((kpos >= 0) & (kpos < seq), 0.0, NEG_INF)
    second = lax.broadcasted_iota(jnp.int32, (2 * tq, 1), 0) >= tq
    outs = []
    for g in range(2):
        sc = _dot_nt(lhs[g].astype(BF16), kwin) + bias
        m = jnp.max(sc, axis=1, keepdims=True)
        p = jnp.exp(sc - m)
        l = jnp.sum(p, axis=1, keepdims=True)
        o = _dot(p.astype(BF16), vwin)
        sk = jnp.where(second, sink_ref[2 * g + 1], sink_ref[2 * g])
        m_tot = jnp.maximum(m, sk)
        w = jnp.exp(m - m_tot)
        den = l * w + jnp.exp(sk - m_tot)
        outs.append(o * (w / den))
    o_ref[:, :LANES] = jnp.where(lo, outs[0][:tq], swap(outs[0][tq:]))
    o_ref[:, LANES:] = jnp.where(lo, swap(outs[1][:tq]), outs[1][tq:])


def _window_attention(proj, sink):
    s = proj.shape[0]
    tq = 256
    n = s // tq
    hw = D_HALF_WINDOW
    per = tq // hw
    last = s // hw - 1
    kcol, vcol = 2816 // LANES, 2944 // LANES
    cur = lambda j: pl.BlockSpec((tq, LANES), lambda i: (i, j))
    prev = lambda j: pl.BlockSpec((hw, LANES), lambda i: (jnp.maximum(i * per - 1, 0), j))
    nxt = lambda j: pl.BlockSpec((hw, LANES), lambda i: (jnp.minimum((i + 1) * per, last), j))
    return pl.pallas_call(
        functools.partial(_window_kernel, seq=s),
        grid=(n,),
        in_specs=[pl.BlockSpec(memory_space=pltpu.SMEM),
                  pl.BlockSpec((tq, GROUP_W), lambda i: (i, 10)),
                  prev(kcol), cur(kcol), nxt(kcol), prev(vcol), cur(vcol), nxt(vcol)],
        out_specs=pl.BlockSpec((tq, GROUP_W), lambda i: (i, 0)),
        out_shape=jax.ShapeDtypeStruct((s, GROUP_W), F32),
        scratch_shapes=[pltpu.VMEM((2 * tq, tq + 2 * hw), F32)],
        compiler_params=_cparams(1),
        name="window_attention",
    )(sink, proj, proj, proj, proj, proj, proj, proj)


def _outproj_kernel(ya_ref, yb_ref, yc_ref, yd_ref, w_ref, x_ref, g_ref, lw_ref, lb_ref, o_ref):
    y = None
    for k, ref in enumerate((ya_ref, yb_ref, yc_ref, yd_ref)):
        part = _dot(ref[...].astype(BF16), w_ref[k * GROUP_W:(k + 1) * GROUP_W, :])
        y = part if y is None else y + part
    r = DEEPNORM_ALPHA * x_ref[...] + (1.0 + g_ref[...]) * y
    o_ref[...] = _layer_norm(r, lw_ref[...], lb_ref[...])


def _outproj(ys, w_bf16, x, gate, ln_w, ln_b):
    s, d = x.shape
    tm = 512
    row = lambda i: (i, 0)
    const = lambda i: (0, 0)
    vec = pl.BlockSpec((1, d), const)
    grp = pl.BlockSpec((tm, GROUP_W), row)
    return pl.pallas_call(
        _outproj_kernel,
        grid=(s // tm,),
        in_specs=[grp, grp, grp, grp, pl.BlockSpec((4 * GROUP_W, d), const), pl.BlockSpec((tm, d), row),
                  vec, vec, vec],
        out_specs=pl.BlockSpec((tm, d), row),
        out_shape=jax.ShapeDtypeStruct((s, d), F32),
        compiler_params=_cparams(1),
        name="outproj_ln",
    )(*ys, w_bf16, x, gate, ln_w, ln_b)


FFN_CHUNK = 256


def _ffn_kernel(x_ref, sc_ref, sh_ref, wu_ref, wd_ref, g_ref, lw_ref, lb_ref, o_ref):
    x = x_ref[...]
    h = (x * (1.0 + sc_ref[...]) + sh_ref[...]).astype(BF16)
    acc = jnp.zeros(x.shape, F32)
    for c0 in range(0, FFN_DIM, FFN_CHUNK):
        gate = _dot(h, wu_ref[:, c0:c0 + FFN_CHUNK])
        up = _dot(h, wu_ref[:, FFN_DIM + c0:FFN_DIM + c0 + FFN_CHUNK])
        acc = acc + _dot((_silu(gate) * up).astype(BF16), wd_ref[c0:c0 + FFN_CHUNK, :])
    r = DEEPNORM_ALPHA * x + (1.0 + g_ref[...]) * acc
    o_ref[...] = _layer_norm(r, lw_ref[...], lb_ref[...])


def _dense_ffn(x, scale, shift, wu_bf16, wd_bf16, gate, ln_w, ln_b):
    s, d = x.shape
    tm = 512
    row = lambda i: (i, 0)
    const = lambda i: (0, 0)
    vec = pl.BlockSpec((1, d), const)
    return pl.pallas_call(
        _ffn_kernel,
        grid=(s // tm,),
        in_specs=[pl.BlockSpec((tm, d), row), vec, vec,
                  pl.BlockSpec((d, 2 * FFN_DIM), const), pl.BlockSpec((FFN_DIM, d), const), vec, vec, vec],
        out_specs=pl.BlockSpec((tm, d), row),
        out_shape=jax.ShapeDtypeStruct((s, d), F32),
        compiler_params=_cparams(1),
        name="dense_ffn_ln",
    )(x, scale, shift, wu_bf16, wd_bf16, gate, ln_w, ln_b)


ROW_TILE = 8


def _to_token_tiles(ref, val):
    n = val.shape[0]
    for c in range(ROW_TILE):
        ref[pl.ds(c, n, stride=ROW_TILE), :] = val[:, c * LANES:(c + 1) * LANES]


def _from_token_tiles(ref, n):
    return jnp.concatenate([ref[pl.ds(c, n, stride=ROW_TILE), :] for c in range(ROW_TILE)], axis=1)


def _router_kernel(x_ref, sc_ref, sh_ref, w_ref, h_ref, e_ref, p_ref, c_ref, cnt_ref):
    h = x_ref[...] * (1.0 + sc_ref[...]) + sh_ref[...]
    _to_token_tiles(h_ref, h)
    h_hi, h_lo = _split_bf16(h)
    w_hi, w_lo = _split_bf16(w_ref[...])
    logits = _dot(h_hi, w_hi) + _dot(h_hi, w_lo) + _dot(h_lo, w_hi)
    lane = lax.broadcasted_iota(jnp.int32, logits.shape, 1)
    logits = jnp.where(lane < N_EXPERTS, logits, NEG_INF)
    m1 = jnp.max(logits, axis=1, keepdims=True)
    e1 = jnp.min(jnp.where(logits == m1, lane, LANES), axis=1, keepdims=True)
    rest = jnp.where(lane == e1, NEG_INF, logits)
    m2 = jnp.max(rest, axis=1, keepdims=True)
    e2 = jnp.min(jnp.where(rest == m2, lane, LANES), axis=1, keepdims=True)
    t = jnp.exp(m2 - m1)
    g1 = 1.0 / (1.0 + t)
    p_ref[...] = jnp.where(lane == 0, g1, jnp.where(lane == 1, t * g1, 0.0))

    @pl.when(pl.program_id(0) == 0)
    def _():
        cnt_ref[...] = jnp.zeros_like(cnt_ref)

    tm = logits.shape[0]
    picks = jnp.where((lane == e1) | (lane == e2), 1.0, 0.0)
    rr = lax.broadcasted_iota(jnp.int32, (tm, tm), 0)
    cc = lax.broadcasted_iota(jnp.int32, (tm, tm), 1)
    before = _dot(jnp.where(cc < rr, 1.0, 0.0).astype(BF16), picks.astype(BF16)) + cnt_ref[...]
    r1 = jnp.sum(jnp.where(lane == e1, before, 0.0), axis=1, keepdims=True).astype(jnp.int32)
    r2 = jnp.sum(jnp.where(lane == e2, before, 0.0), axis=1, keepdims=True).astype(jnp.int32)
    cnt_ref[...] += jnp.sum(picks, axis=0, keepdims=True)
    c_ref[...] = jnp.broadcast_to(cnt_ref[...], c_ref.shape)
    e_ref[...] = jnp.where(lane == 0, e1, jnp.where(lane == 1, e2, jnp.where(lane == 2, r1, jnp.where(lane == 3, r2, 0))))


def _router(x, scale, shift, w_router):
    s, d = x.shape
    tm = 512
    row = lambda i: (i, 0)
    const = lambda i: (0, 0)
    vec = pl.BlockSpec((1, d), const)
    wpad = jnp.pad(w_router, ((0, 0), (0, LANES - N_EXPERTS)))
    return pl.pallas_call(
        _router_kernel,
        grid=(s // tm,),
        in_specs=[pl.BlockSpec((tm, d), row), vec, vec, pl.BlockSpec((d, LANES), const)],
        out_specs=[pl.BlockSpec((tm * ROW_TILE, LANES), row), pl.BlockSpec((tm, LANES), row),
                   pl.BlockSpec((tm, LANES), row), pl.BlockSpec((8, LANES), const)],
        out_shape=[jax.ShapeDtypeStruct((s * ROW_TILE, LANES), F32), jax.ShapeDtypeStruct((s, LANES), jnp.int32),
                   jax.ShapeDtypeStruct((s, LANES), F32), jax.ShapeDtypeStruct((8, LANES), F32)],
        scratch_shapes=[pltpu.VMEM((1, LANES), F32)],
        compiler_params=_cparams(1),
        name="moe_router",
    )(x, scale, shift, wpad)


MOE_TILE = 512
MOE_CHUNK = 1792
ROW_UNROLL = 8
TILE_ROWS = MOE_TILE * ROW_TILE


def _expert_kernel(te_ref, nu_ref, tokc_ref, tokn_ref, dst_ref, h_hbm, wg_ref, wu_ref, wd_ref, y_hbm,
                   xbuf, xb16, acc, ybuf, gsem, ssem):
    i, f = pl.program_id(0), pl.program_id(1)
    nt, nf = pl.num_programs(0), pl.num_programs(1)
    n_used = nu_ref[0]
    slot = i & 1
    tail = y_hbm.shape[0] - TILE_ROWS

    def rows_loop(fn):
        def body(g, carry):
            base = pl.multiple_of(g * ROW_UNROLL, ROW_UNROLL)
            for j in range(ROW_UNROLL):
                fn(base + j)
            return carry
        lax.fori_loop(0, MOE_TILE // ROW_UNROLL, body, 0)

    def token(ref, r):
        return ref.at[pl.ds(pl.multiple_of(r, ROW_TILE), ROW_TILE)]

    def gather_rows(ids_ref, s):
        rows_loop(lambda r: pltpu.make_async_copy(token(h_hbm, ids_ref[0, r]), token(xbuf.at[s], r * ROW_TILE),
                                                  gsem.at[s]).start())

    def gather_wait(s):
        pltpu.make_async_copy(h_hbm.at[pl.ds(0, TILE_ROWS)], xbuf.at[s], gsem.at[s]).wait()

    def scatter_rows(s):
        rows_loop(lambda r: pltpu.make_async_copy(token(ybuf.at[s], r * ROW_TILE), token(y_hbm, dst_ref[0, r]),
                                                  ssem.at[s]).start())

    def scatter_wait(s):
        pltpu.make_async_copy(ybuf.at[s], y_hbm.at[pl.ds(0, TILE_ROWS)], ssem.at[s]).wait()

    @pl.when(f == 0)
    def _():
        @pl.when(i == 0)
        def _():
            ybuf[1] = jnp.zeros(ybuf.shape[1:], F32)
            zero_tail = pltpu.make_async_copy(ybuf.at[1], y_hbm.at[pl.ds(tail, TILE_ROWS)], ssem.at[1])
            zero_tail.start()
            zero_tail.wait()
            gather_rows(tokc_ref, 0)

        @pl.when(i + 1 < n_used)
        def _():
            gather_rows(tokn_ref, 1 - slot)

        @pl.when(i < n_used)
        def _():
            gather_wait(slot)
            xb16[...] = _from_token_tiles(xbuf.at[slot], MOE_TILE).astype(BF16)

    @pl.when(i < n_used)
    def _():
        x = xb16[...]
        act = _silu(_dot(x, wg_ref[...])) * _dot(x, wu_ref[...])
        part = _dot(act.astype(BF16), wd_ref[...])

        @pl.when(f == 0)
        def _():
            acc[...] = part

        @pl.when((f > 0) & (f < nf - 1))
        def _():
            acc[...] += part

        @pl.when(f == nf - 1)
        def _():
            _to_token_tiles(ybuf.at[slot], acc[...] + part)

    @pl.when(f == nf - 1)
    def _():
        @pl.when((i >= 1) & (i - 1 < n_used))
        def _():
            scatter_wait(1 - slot)

        @pl.when(i < n_used)
        def _():
            scatter_rows(slot)

        @pl.when((i == nt - 1) & (i < n_used))
        def _():
            scatter_wait(slot)


def _experts(h, slot_tok, slot_dst, tile_expert, n_used, wu_bf16, wd_bf16, n_rows_out):
    d = D_MODEL
    n_tiles = slot_tok.shape[0] // MOE_TILE
    nf = EXPERT_DIM // MOE_CHUNK
    assert nf >= 2
    fidx = lambda i, f, nu: jnp.where(i < nu[0], f, nf - 1)
    ids = lambda fn: pl.BlockSpec((None, 1, MOE_TILE), lambda i, f, te, nu: (fn(i), 0, 0), memory_space=pltpu.SMEM)
    grid_spec = pltpu.PrefetchScalarGridSpec(
        num_scalar_prefetch=2,
        grid=(n_tiles, nf),
        in_specs=[ids(lambda i: i), ids(lambda i: jnp.minimum(i + 1, n_tiles - 1)), ids(lambda i: i),
                  pl.BlockSpec(memory_space=pl.ANY),
                  pl.BlockSpec((None, d, MOE_CHUNK), lambda i, f, te, nu: (te[i], 0, fidx(i, f, nu))),
                  pl.BlockSpec((None, d, MOE_CHUNK), lambda i, f, te, nu: (te[i], 0, nf + fidx(i, f, nu))),
                  pl.BlockSpec((None, MOE_CHUNK, d), lambda i, f, te, nu: (te[i], fidx(i, f, nu), 0))],
        out_specs=pl.BlockSpec(memory_space=pl.ANY),
        scratch_shapes=[pltpu.VMEM((2, TILE_ROWS, LANES), F32), pltpu.VMEM((MOE_TILE, d), BF16),
                        pltpu.VMEM((MOE_TILE, d), F32), pltpu.VMEM((2, TILE_ROWS, LANES), F32),
                        pltpu.SemaphoreType.DMA((2,)), pltpu.SemaphoreType.DMA((2,))],
    )
    tok3 = (slot_tok * ROW_TILE).reshape(n_tiles, 1, MOE_TILE)
    dst3 = (slot_dst * ROW_TILE).reshape(n_tiles, 1, MOE_TILE)
    return pl.pallas_call(
        _expert_kernel,
        grid_spec=grid_spec,
        out_shape=jax.ShapeDtypeStruct((n_rows_out * ROW_TILE, LANES), F32),
        compiler_params=_cparams(2),
        name="moe_experts",
    )(tile_expert, n_used, tok3, tok3, dst3, h, wu_bf16, wu_bf16, wd_bf16)


def _combine_kernel(x_ref, y1_ref, y2_ref, p_ref, g_ref, lw_ref, lb_ref, o_ref):
    p = p_ref[...]
    n = x_ref.shape[0]
    f = p[:, 0:1] * _from_token_tiles(y1_ref, n) + p[:, 1:2] * _from_token_tiles(y2_ref, n)
    r = DEEPNORM_ALPHA * x_ref[...] + (1.0 + g_ref[...]) * f
    o_ref[...] = _layer_norm(r, lw_ref[...], lb_ref[...])


def _combine(x, y, probs, gate, ln_w, ln_b):
    s, d = x.shape
    tm = 512
    row = lambda i: (i, 0)
    vec = pl.BlockSpec((1, d), lambda i: (0, 0))
    big = pl.BlockSpec((tm, d), row)
    return pl.pallas_call(
        _combine_kernel,
        grid=(s // tm,),
        in_specs=[big, pl.BlockSpec((tm * ROW_TILE, LANES), row),
                  pl.BlockSpec((tm * ROW_TILE, LANES), lambda i: (s // tm + i, 0)),
                  pl.BlockSpec((tm, LANES), row), vec, vec, vec],
        out_specs=big,
        out_shape=jax.ShapeDtypeStruct((s, d), F32),
        compiler_params=_cparams(1),
        name="moe_combine_ln",
    )(x, y, y, probs, gate, ln_w, ln_b)


def _moe(x, scale, shift, w_router, wu_bf16, wd_bf16, gate, ln_w, ln_b):
    t, d = x.shape
    h, sel, probs, cnt = _router(x, scale, shift, w_router)
    flat_e = sel[:, :TOP_K].reshape(-1)
    rank = sel[:, TOP_K:2 * TOP_K].reshape(-1)
    counts = cnt[0, :N_EXPERTS].astype(jnp.int32)
    n_assign = t * TOP_K
    n_tiles = -(-n_assign // MOE_TILE) + N_EXPERTS
    n_slots = n_tiles * MOE_TILE
    padded = (counts + MOE_TILE - 1) // MOE_TILE * MOE_TILE
    padded_end = jnp.cumsum(padded)
    slot = (padded_end - padded)[flat_e] + rank
    assign = jnp.full((n_slots,), -1, jnp.int32).at[slot].set(jnp.arange(n_assign, dtype=jnp.int32))
    used = assign >= 0
    slot_tok = jnp.where(used, assign >> 1, 0)
    slot_dst = jnp.where(used, (assign & 1) * t + (assign >> 1),
                         n_assign + (jnp.arange(n_slots, dtype=jnp.int32) & (MOE_TILE - 1)))
    n_used = (padded_end[-1] // MOE_TILE).astype(jnp.int32).reshape(1)
    tile_ids = jnp.minimum(jnp.arange(n_tiles, dtype=jnp.int32), n_used[0] - 1)
    tile_expert = jnp.minimum(jnp.sum((tile_ids * MOE_TILE)[:, None] >= padded_end[None, :], axis=1),
                              N_EXPERTS - 1).astype(jnp.int32)
    y = _experts(h, slot_tok, slot_dst, tile_expert, n_used, wu_bf16, wd_bf16, n_assign + MOE_TILE)
    return _combine(x, y, probs, gate, ln_w, ln_b)


def kernel(x, c, positions, w_ada, b_ada, w_in, w_out, a_lower_bound, a_norm_w, b_conv_w, b_conv_b, b_norm_w,
           b_norm_b, d_sink, ln_w, ln_b, ffn_w_up, ffn_w_down, moe_router, moe_w_up, moe_w_down):
    batch, s, d = x.shape
    assert batch == 1 and d == D_MODEL
    x = x.reshape(s, d)
    mod = _adaln(c, w_ada, b_ada)
    rope = _rope_tables(positions)
    lb_cum = jnp.cumsum(jax.nn.softmax(a_lower_bound.astype(F32), axis=0), axis=0)
    lb_all = lb_cum - lb_cum[0]
    row = lambda v: v.reshape(1, -1)
    for layer in range(DEPTH):
        shift1, scale1, gate1, shift2, scale2, gate2 = [mod[layer, :, k * d:(k + 1) * d] for k in range(6)]
        proj = _inproj(x, scale1, shift1, w_in[layer].astype(BF16), rope)
        ya = _hgrn2(proj, row(lb_all[layer]), row(jnp.tile(a_norm_w[layer], GROUP_W // HEAD_DIM)))
        yb = _conv_module(proj, b_conv_w[layer], row(b_conv_b[layer]), row(b_norm_w[layer]), row(b_norm_b[layer]))
        yc = _dilated_attention(proj)
        yd = _window_attention(proj, d_sink[layer])
        x = _outproj((ya, yb, yc, yd), w_out[layer].astype(BF16), x, gate1, row(ln_w[layer, 0]), row(ln_b[layer, 0]))
        lw, lbias = row(ln_w[layer, 1]), row(ln_b[layer, 1])
        if layer % 2 == 0:
            x = _dense_ffn(x, scale2, shift2, ffn_w_up[layer // 2].astype(BF16),
                           ffn_w_down[layer // 2].astype(BF16), gate2, lw, lbias)
        else:
            x = _moe(x, scale2, shift2, moe_router[layer // 2], moe_w_up[layer // 2].astype(BF16),
                     moe_w_down[layer // 2].astype(BF16), gate2, lw, lbias)
    return x.reshape(batch, s, d)
```

```python
import functools

import numpy as np
import jax
import jax.numpy as jnp
from jax import lax
from jax.experimental import pallas as pl
from jax.experimental.pallas import tpu as pltpu

F32 = jnp.float32
BF16 = jnp.bfloat16

D_MODEL = 1024
DEPTH = 2
HEAD_DIM = 64
A_CHUNK = 16
B_KERNEL = 31
GROUP_W = 256
C_PATTERNS = ((128, 1), (512, 4), (2048, 16))
C_SIDE = 64
C_HALO = 1024
D_HALF_WINDOW = 128
ROPE_THETA = 500000.0
ROPE_DIM = HEAD_DIM // 4
FFN_DIM = 2816
N_EXPERTS = 8
TOP_K = 2
EXPERT_DIM = 3584
IN_WIDTH = 3072
DEEPNORM_ALPHA = (2 * DEPTH) ** 0.25
LN_EPS = 1e-5
RMS_EPS = 1e-6
NEG_INF = -1e30

LANES = 128
SUBLANES = 8
VMEM_LIMIT = 56 * 1024 * 1024


def _cparams(n_axes=1, vmem=VMEM_LIMIT):
    return pltpu.CompilerParams(dimension_semantics=("arbitrary",) * n_axes, vmem_limit_bytes=vmem)


def _sigmoid(x):
    return 1.0 / (1.0 + jnp.exp(-x))


def _silu(x):
    return x * _sigmoid(x)


def _split_bf16(x):
    hi = x.astype(BF16)
    lo = (x - hi.astype(F32)).astype(BF16)
    return hi, lo


def _layer_norm(r, w, b):
    mu = jnp.mean(r, axis=-1, keepdims=True)
    d = r - mu
    var = jnp.mean(d * d, axis=-1, keepdims=True)
    return d * lax.rsqrt(var + LN_EPS) * w + b


def _dot_nt(a, b):
    return lax.dot_general(a, b, (((1,), (1,)), ((), ())), preferred_element_type=F32)


def _dot_tn(a, b):
    return lax.dot_general(a, b, (((0,), (0,)), ((), ())), preferred_element_type=F32)


def _dot(a, b):
    return jnp.dot(a, b, preferred_element_type=F32)


def _adaln_kernel(c_ref, w_ref, b_ref, o_ref):
    a_hi, a_lo = _split_bf16(_silu(c_ref[...]))
    w_hi, w_lo = _split_bf16(w_ref[...])
    o_ref[...] = _dot(a_hi, w_hi) + _dot(a_hi, w_lo) + _dot(a_lo, w_hi) + b_ref[...]


def _adaln(c, w_ada, b_ada):
    depth, d, n = w_ada.shape
    tn = 1536
    c8 = jnp.broadcast_to(c.reshape(1, d), (8, d))
    out = pl.pallas_call(
        _adaln_kernel,
        grid=(depth, n // tn),
        in_specs=[pl.BlockSpec((8, d), lambda l, j: (0, 0)),
                  pl.BlockSpec((None, d, tn), lambda l, j: (l, 0, j)),
                  pl.BlockSpec((None, 1, tn), lambda l, j: (l, 0, j))],
        out_specs=pl.BlockSpec((None, 8, tn), lambda l, j: (l, 0, j)),
        out_shape=jax.ShapeDtypeStruct((depth, 8, n), F32),
        compiler_params=_cparams(2),
        name="adaln",
    )(c8, w_ada, b_ada.reshape(depth, 1, n))
    return out[:, 0:1, :]


def _rope_kernel(pos_ref, invf_ref, c_ref, s1_ref, s2_ref):
    ang = pos_ref[...].astype(F32) * invf_ref[...]
    j = lax.broadcasted_iota(jnp.int32, ang.shape, 1) & (HEAD_DIM - 1)
    cosv, sinv = jnp.cos(ang), jnp.sin(ang)
    half = ROPE_DIM // 2
    c_ref[...] = jnp.where(j < ROPE_DIM, cosv, 1.0)
    s1_ref[...] = jnp.where(j < half, -sinv, 0.0)
    s2_ref[...] = jnp.where((j >= half) & (j < ROPE_DIM), sinv, 0.0)


def _rope_tables(positions):
    s = positions.shape[-1]
    ts = 2048
    half = ROPE_DIM // 2
    inv_freq = np.float32(ROPE_THETA) ** (-np.arange(half, dtype=np.float32) * np.float32(2.0 / ROPE_DIM))
    invf = jnp.asarray(inv_freq[(np.arange(LANES) % HEAD_DIM) % half].reshape(1, LANES).astype(np.float32))
    spec = pl.BlockSpec((ts, LANES), lambda i: (i, 0))
    return pl.pallas_call(
        _rope_kernel,
        grid=(s // ts,),
        in_specs=[pl.BlockSpec((ts, 1), lambda i: (i, 0)), pl.BlockSpec((1, LANES), lambda i: (0, 0))],
        out_specs=[spec, spec, spec],
        out_shape=[jax.ShapeDtypeStruct((s, LANES), F32)] * 3,
        compiler_params=_cparams(1),
        name="rope_tables",
    )(positions.reshape(s, 1), invf)


_ROPE_COL_CHUNKS = (1792, 1920, 2048, 2176, 2560, 2688, 2816)


ATTN_COL0 = 1792
ATTN_WIDTH = IN_WIDTH - ATTN_COL0


def _inproj_kernel(x_ref, sc_ref, sh_ref, w_ref, rc_ref, rs1_ref, rs2_ref, o_ref, oa_ref):
    h = (x_ref[...] * (1.0 + sc_ref[...]) + sh_ref[...]).astype(BF16)
    half = ROPE_DIM // 2
    for c0 in range(0, IN_WIDTH, GROUP_W):
        acc = _dot(h, w_ref[:, c0:c0 + GROUP_W])
        for k in range(GROUP_W // LANES):
            col = c0 + k * LANES
            part = acc[:, k * LANES:(k + 1) * LANES]
            if col in _ROPE_COL_CHUNKS:
                part = (part * rc_ref[...] + pltpu.roll(part, LANES - half, 1) * rs1_ref[...]
                        + pltpu.roll(part, half, 1) * rs2_ref[...])
            if col < ATTN_COL0:
                o_ref[:, col:col + LANES] = part
            else:
                oa_ref[:, col - ATTN_COL0:col - ATTN_COL0 + LANES] = part.astype(BF16)


def _inproj(x, scale, shift, w_bf16, rope):
    s, d = x.shape
    tm = 512
    row = lambda i: (i, 0)
    const = lambda i: (0, 0)
    return pl.pallas_call(
        _inproj_kernel,
        grid=(s // tm,),
        in_specs=[pl.BlockSpec((tm, d), row), pl.BlockSpec((1, d), const), pl.BlockSpec((1, d), const),
                  pl.BlockSpec((d, IN_WIDTH), const),
                  pl.BlockSpec((tm, LANES), row), pl.BlockSpec((tm, LANES), row), pl.BlockSpec((tm, LANES), row)],
        out_specs=[pl.BlockSpec((tm, ATTN_COL0), row), pl.BlockSpec((tm, ATTN_WIDTH), row)],
        out_shape=[jax.ShapeDtypeStruct((s, ATTN_COL0), F32), jax.ShapeDtypeStruct((s, ATTN_WIDTH), BF16)],
        compiler_params=_cparams(1),
        name="inproj",
    )(x, scale, shift, w_bf16, *rope)


HG_TILE = 512
HG_CHUNKS = HG_TILE // A_CHUNK
N_HEADS = GROUP_W // HEAD_DIM


def _same_head_ones():
    li = lax.broadcasted_iota(jnp.int32, (GROUP_W, GROUP_W), 0)
    lj = lax.broadcasted_iota(jnp.int32, (GROUP_W, GROUP_W), 1)
    return jnp.where((li >> 6) == (lj >> 6), 1.0, 0.0).astype(BF16)


def _hgrn_core(q_ref, z_ref, v_ref, lb_ref, st_ref, qs, zs, vs, qt, kt, ot, oi_ref, rev):
    nb = HG_CHUNKS
    lb = lb_ref[...]
    for src, dst in ((q_ref, qs), (z_ref, zs), (v_ref, vs)):
        val = src[...]
        dst[0] = val[:, :LANES]
        dst[1] = val[:, LANES:]

    def by_pos(buf, j):
        sel = pl.ds(j, nb, stride=A_CHUNK)
        return jnp.concatenate([buf[0, sel, :], buf[1, sel, :]], axis=1)

    order = list(range(A_CHUNK))[::-1] if rev else list(range(A_CHUNK))
    q, f, k, v = [], [], [], []
    for j in order:
        aq, z = by_pos(qs, j), by_pos(zs, j)
        q.append(_silu(aq))
        f.append(lb + (1.0 - lb) * _sigmoid(z))
        k.append((1.0 - lb) * _sigmoid(-z))
        v.append(by_pos(vs, j))
    incl = [f[0]]
    for p in range(1, A_CHUNK):
        incl.append(incl[-1] * f[p])
    rest = [None] * A_CHUNK
    for p in range(A_CHUNK - 2, -1, -1):
        rest[p] = f[p + 1] if rest[p + 1] is None else rest[p + 1] * f[p + 1]
    dec = incl[-1]

    head = lax.broadcasted_iota(jnp.int32, (1, GROUP_W), 1) >> 6
    lo = lax.broadcasted_iota(jnp.int32, (1, LANES), 1) < HEAD_DIM
    ones_bd = _same_head_ones()

    def put(dst, sel, val):
        dst[0, sel, :] = val[:, :LANES]
        dst[1, sel, :] = val[:, LANES:]

    for p, j in enumerate(order):
        xs, decay = [], None
        for d in range(p + 1):
            x = q[p] * k[p - d]
            xs.append((x if decay is None else x * decay).astype(BF16))
            decay = f[p - d] if decay is None else decay * f[p - d]
        sc = _dot(jnp.concatenate(xs, axis=0), ones_bd)
        o = sc[0:nb] * v[p]
        for d in range(1, p + 1):
            o = o + sc[d * nb:(d + 1) * nb] * v[p - d]
        sel = pl.ds(j, nb, stride=A_CHUNK)
        put(ot, sel, o)
        put(qt, sel, q[p] * incl[p])
        put(kt, sel, k[p] if rest[p] is None else k[p] * rest[p])

    def per_head(x):
        return jnp.concatenate([jnp.where(head == h, x, 0.0) for h in range(N_HEADS)], axis=0).astype(BF16)

    chunks = range(nb - 1, -1, -1) if rev else range(nb)
    state = st_ref[...]
    for n in chunks:
        r1 = slice(n * A_CHUNK, (n + 1) * A_CHUNK)
        lq = per_head(jnp.concatenate([qt[0, r1, :], qt[1, r1, :]], axis=1))
        rk = per_head(jnp.concatenate([kt[0, r1, :], kt[1, r1, :]], axis=1))
        va, vb = v_ref[r1, :LANES], v_ref[r1, LANES:]
        lv = jnp.concatenate([va, pltpu.roll(va, HEAD_DIM, 1), vb, pltpu.roll(vb, HEAD_DIM, 1)], axis=0).astype(BF16)
        sb = state.astype(BF16)
        res = _dot_nt(lq, jnp.concatenate([sb, sb], axis=0))
        oi_ref[r1, :LANES] = jnp.where(lo, res[0:A_CHUNK], res[A_CHUNK:2 * A_CHUNK])
        oi_ref[r1, LANES:] = jnp.where(lo, res[2 * A_CHUNK:3 * A_CHUNK], res[3 * A_CHUNK:])
        kv = _dot_tn(lv, rk)[:HEAD_DIM]
        state = state * dec[n:n + 1] + kv
    st_ref[...] = state
    full = slice(0, HG_TILE)
    return jnp.concatenate([ot[0, full, :], ot[1, full, :]], axis=1) + oi_ref[...]


def _hgrn_fwd_kernel(q_ref, f_ref, i_ref, lb_ref, o_ref, st_ref, *scratch):
    @pl.when(pl.program_id(0) == 0)
    def _():
        st_ref[...] = jnp.zeros_like(st_ref)

    o_ref[...] = _hgrn_core(q_ref, f_ref, i_ref, lb_ref, st_ref, *scratch, rev=False)


def _hgrn_bwd_kernel(q_ref, f_ref, i_ref, g_ref, of_ref, lb_ref, nw_ref, o_ref, st_ref, *scratch):
    @pl.when(pl.program_id(0) == 0)
    def _():
        st_ref[...] = jnp.zeros_like(st_ref)

    o = of_ref[...] + _hgrn_core(q_ref, f_ref, i_ref, lb_ref, st_ref, *scratch, rev=True)
    ones_bd = _same_head_ones()
    sq_hi, sq_lo = _split_bf16(o * o)
    ms = (_dot(sq_hi, ones_bd) + _dot(sq_lo, ones_bd)) * (1.0 / HEAD_DIM)
    o_ref[...] = o * lax.rsqrt(ms + RMS_EPS) * nw_ref[...] * _silu(g_ref[...])


def _hgrn_scratch():
    split = lambda rows: pltpu.VMEM((2, rows, LANES), F32)
    return [pltpu.VMEM((HEAD_DIM, GROUP_W), F32),
            split(HG_TILE), split(HG_TILE), split(HG_TILE),
            split(HG_TILE), split(HG_TILE), split(HG_TILE), pltpu.VMEM((HG_TILE, GROUP_W), F32)]


def _hgrn2(proj, lb, norm_w):
    s = proj.shape[0]
    ts = HG_TILE
    n = s // ts
    col = lambda j: pl.BlockSpec((ts, GROUP_W), lambda i: (i, j))
    rcol = lambda j: pl.BlockSpec((ts, GROUP_W), lambda i: (n - 1 - i, j))
    vec = pl.BlockSpec((1, GROUP_W), lambda i: (0, 0))
    o_f = pl.pallas_call(
        _hgrn_fwd_kernel,
        grid=(n,),
        in_specs=[col(0), col(1), col(3), vec],
        out_specs=pl.BlockSpec((ts, GROUP_W), lambda i: (i, 0)),
        out_shape=jax.ShapeDtypeStruct((s, GROUP_W), F32),
        scratch_shapes=_hgrn_scratch(),
        compiler_params=_cparams(1),
        name="hgrn_fwd",
    )(proj, proj, proj, lb)
    return pl.pallas_call(
        _hgrn_bwd_kernel,
        grid=(n,),
        in_specs=[rcol(0), rcol(2), rcol(3), rcol(4), pl.BlockSpec((ts, GROUP_W), lambda i: (n - 1 - i, 0)),
                  vec, vec],
        out_specs=pl.BlockSpec((ts, GROUP_W), lambda i: (n - 1 - i, 0)),
        out_shape=jax.ShapeDtypeStruct((s, GROUP_W), F32),
        scratch_shapes=_hgrn_scratch(),
        compiler_params=_cparams(1),
        name="hgrn_bwd",
    )(proj, proj, proj, proj, o_f, lb, norm_w)


B_PAD = 16


def _conv_kernel(v_ref, g_ref, vp_ref, gp_ref, vn_ref, gn_ref, w_ref, b_ref, nw_ref, nb_ref, o_ref, ubuf):
    i = pl.program_id(0)
    ts = v_ref.shape[0]
    glu = lambda a, b: a * _sigmoid(b)
    ubuf[0:B_PAD, :] = jnp.where(i > 0, glu(vp_ref[...], gp_ref[...]), 0.0)
    ubuf[B_PAD:B_PAD + ts, :] = glu(v_ref[...], g_ref[...])
    ubuf[B_PAD + ts:, :] = jnp.where(i < pl.num_programs(0) - 1, glu(vn_ref[...], gn_ref[...]), 0.0)
    acc = jnp.zeros((ts, GROUP_W), F32) + b_ref[...]
    span = ts + 2 * B_PAD - SUBLANES
    shifted = [ubuf[pl.ds(r, span), :] for r in range(SUBLANES)]
    for j in range(B_KERNEL):
        off = B_PAD - B_KERNEL // 2 + j
        base = off - off % SUBLANES
        acc = acc + shifted[off % SUBLANES][base:base + ts] * w_ref[j:j + 1, :]
    o_ref[...] = _silu(_layer_norm(acc, nw_ref[...], nb_ref[...]))


def _conv_module(proj, conv_w, conv_b, norm_w, norm_b):
    s = proj.shape[0]
    ts = 512
    n = s // ts
    per = ts // B_PAD
    last = s // B_PAD - 1
    cur = lambda j: pl.BlockSpec((ts, GROUP_W), lambda i: (i, j))
    prev = lambda j: pl.BlockSpec((B_PAD, GROUP_W), lambda i: (jnp.maximum(i * per - 1, 0), j))
    nxt = lambda j: pl.BlockSpec((B_PAD, GROUP_W), lambda i: (jnp.minimum((i + 1) * per, last), j))
    vec = pl.BlockSpec((1, GROUP_W), lambda i: (0, 0))
    wpad = jnp.pad(conv_w, ((0, 1), (0, 0)))
    return pl.pallas_call(
        _conv_kernel,
        grid=(n,),
        in_specs=[cur(5), cur(6), prev(5), prev(6), nxt(5), nxt(6),
                  pl.BlockSpec((B_KERNEL + 1, GROUP_W), lambda i: (0, 0)), vec, vec, vec],
        out_specs=pl.BlockSpec((ts, GROUP_W), lambda i: (i, 0)),
        out_shape=jax.ShapeDtypeStruct((s, GROUP_W), F32),
        scratch_shapes=[pltpu.VMEM((ts + 2 * B_PAD, GROUP_W), F32)],
        compiler_params=_cparams(1),
        name="conv_module",
    )(proj, proj, proj, proj, proj, proj, wpad, conv_b, norm_w, norm_b)


C_MAX_BLOCK = 128


def _dilated_block_sizes(span):
    return sorted({min(span // dil, C_MAX_BLOCK) for _, dil in C_PATTERNS})


def _dilated_kernel(q_ref, kp_ref, kc_ref, kn_ref, vp_ref, vc_ref, vn_ref, o_ref, qw, kw, vw, ob, mb, lb, *band_refs,
                    seq, span):
    i = pl.program_id(0)
    band = dict(zip(_dilated_block_sizes(span), band_refs))

    @pl.when(i == 0)
    def _():
        for nq, ref in band.items():
            a = lax.broadcasted_iota(jnp.int32, ref.shape, 0) & (nq - 1)
            b = lax.broadcasted_iota(jnp.int32, ref.shape, 1)
            ref[...] = jnp.where(jnp.abs(a - (b - C_SIDE)) <= C_SIDE, 0.0, NEG_INF)

    def put(dst, row0, val):
        n = val.shape[0]
        dst[0, row0:row0 + n, :] = val[:, :LANES]
        dst[1, row0:row0 + n, :] = val[:, LANES:]

    put(qw, 0, q_ref[...].astype(F32) * (HEAD_DIM ** -0.5))
    for dst, refs in ((kw, (kp_ref, kc_ref, kn_ref)), (vw, (vp_ref, vc_ref, vn_ref))):
        put(dst, 0, refs[0][...].astype(F32))
        put(dst, C_HALO, refs[1][...].astype(F32))
        put(dst, C_HALO + span, refs[2][...].astype(F32))

    head = lax.broadcasted_iota(jnp.int32, (1, GROUP_W), 1) >> 6
    heads = [head == h for h in range(4)]

    def rows(start, size, dil):
        return pl.ds(start, size) if dil == 1 else pl.ds(start, size, stride=dil)

    def load(src, sel):
        return jnp.concatenate([src[0, sel, :], src[1, sel, :]], axis=1)

    def store(dst, sel, val):
        dst[0, sel, :] = val[:, :LANES]
        dst[1, sel, :] = val[:, LANES:]

    def by_head(x, nq):
        out = jnp.where(heads[0], x[0:nq], 0.0)
        for h in range(1, 4):
            out = out + jnp.where(heads[h], x[h * nq:(h + 1) * nq], 0.0)
        return out

    def attend(first, r, dil, blk, nq):
        nk = nq + 2 * C_SIDE
        qrows = rows(r + dil * blk * nq, nq, dil)
        koff = C_HALO + r + dil * (blk * nq - C_SIDE)
        qs = load(qw, qrows)
        ks = load(kw, rows(koff, nk, dil)).astype(BF16)
        vs = load(vw, rows(koff, nk, dil)).astype(BF16)
        lhs = jnp.concatenate([jnp.where(hm, qs, 0.0) for hm in heads], axis=0).astype(BF16)
        kpos = i * span + (koff - C_HALO) + dil * lax.broadcasted_iota(jnp.int32, (1, nk), 1)
        edge = jnp.where((kpos >= 0) & (kpos < seq), 0.0, NEG_INF)
        sc = _dot_nt(lhs, ks) + band[nq][...] + edge
        m = jnp.max(sc, axis=1, keepdims=True)
        p = jnp.exp(sc - m)
        l = jnp.sum(p, axis=1, keepdims=True)
        o_new = by_head(_dot(p.astype(BF16), vs), nq)
        m_new = by_head(jnp.broadcast_to(m, (4 * nq, GROUP_W)), nq)
        l_new = by_head(jnp.broadcast_to(l, (4 * nq, GROUP_W)), nq)
        if first:
            store(ob, qrows, o_new)
            store(mb, qrows, m_new)
            store(lb, qrows, l_new)
        else:
            m_old = load(mb, qrows)
            m_tot = jnp.maximum(m_old, m_new)
            w_old, w_new = jnp.exp(m_old - m_tot), jnp.exp(m_new - m_tot)
            store(ob, qrows, w_old * load(ob, qrows) + w_new * o_new)
            store(lb, qrows, w_old * load(lb, qrows) + w_new * l_new)
            store(mb, qrows, m_tot)

    for pat, (_, dil) in enumerate(C_PATTERNS):
        sub = span // dil
        nq = min(sub, C_MAX_BLOCK)
        for r in range(dil):
            for blk in range(sub // nq):
                attend(pat == 0, r, dil, blk, nq)
    full = slice(0, span)
    o_ref[...] = load(ob, full) / load(lb, full)


def _dilated_attention(proj):
    s = proj.shape[0]
    span = 1024
    n = s // span
    per = span // C_HALO
    last = s // C_HALO - 1
    cur = lambda j: pl.BlockSpec((span, GROUP_W), lambda i: (i, j))
    prev = lambda j: pl.BlockSpec((C_HALO, GROUP_W), lambda i: (jnp.maximum(i * per - 1, 0), j))
    nxt = lambda j: pl.BlockSpec((C_HALO, GROUP_W), lambda i: (jnp.minimum((i + 1) * per, last), j))
    win = span + 2 * C_HALO
    return pl.pallas_call(
        functools.partial(_dilated_kernel, seq=s, span=span),
        grid=(n,),
        in_specs=[cur(0), prev(1), cur(1), nxt(1), prev(2), cur(2), nxt(2)],
        out_specs=pl.BlockSpec((span, GROUP_W), lambda i: (i, 0)),
        out_shape=jax.ShapeDtypeStruct((s, GROUP_W), F32),
        scratch_shapes=[pltpu.VMEM((2, span, LANES), F32),
                        pltpu.VMEM((2, win, LANES), F32), pltpu.VMEM((2, win, LANES), F32),
                        pltpu.VMEM((2, span, LANES), F32), pltpu.VMEM((2, span, LANES), F32),
                        pltpu.VMEM((2, span, LANES), F32)]
        + [pltpu.VMEM((N_HEADS * nq, nq + 2 * C_SIDE), F32) for nq in _dilated_block_sizes(span)],
        compiler_params=_cparams(1),
        name="dilated_attention",
    )(proj, proj, proj, proj, proj, proj, proj)


def _window_kernel(sink_ref, q_ref, kp_ref, kc_ref, kn_ref, vp_ref, vc_ref, vn_ref, o_ref, band_ref, *, seq):
    i = pl.program_id(0)
    tq = q_ref.shape[0]
    hw = D_HALF_WINDOW
    nk = tq + 2 * hw
    kwin = jnp.concatenate([kp_ref[...], kc_ref[...], kn_ref[...]], axis=0).astype(BF16)
    vwin = jnp.concatenate([vp_ref[...], vc_ref[...], vn_ref[...]], axis=0).astype(BF16)
    lo = lax.broadcasted_iota(jnp.int32, (1, LANES), 1) < HEAD_DIM
    q = q_ref[...].astype(F32) * (HEAD_DIM ** -0.5)
    qa, qb = q[:, :LANES], q[:, LANES:]
    swap = lambda x: pltpu.roll(x, HEAD_DIM, 1)
    lhs = (jnp.concatenate([jnp.where(lo, qa, 0.0), jnp.where(lo, swap(qa), 0.0)], axis=0),
           jnp.concatenate([jnp.where(lo, 0.0, swap(qb)), jnp.where(lo, 0.0, qb)], axis=0))
    @pl.when(i == 0)
    def _():
        a = lax.broadcasted_iota(jnp.int32, (2 * tq, nk), 0) & (tq - 1)
        b = lax.broadcasted_iota(jnp.int32, (2 * tq, nk), 1)
        band_ref[...] = jnp.where(jnp.abs(b - hw - a) <= hw, 0.0, NEG_INF)

    kpos = i * tq - hw + lax.broadcasted_iota(jnp.int32, (1, nk), 1)
    bias = band_ref[...] + jnp.where((kpos >= 0) & (kpos < seq), 0.0, NEG_INF)
    second = lax.broadcasted_iota(jnp.int32, (2 * tq, 1), 0) >= tq
    outs = []
    for g in range(2):
        sc = _dot_nt(lhs[g].astype(BF16), kwin) + bias
        m = jnp.max(sc, axis=1, keepdims=True)
        p = jnp.exp(sc - m)
        l = jnp.sum(p, axis=1, keepdims=True)
        o = _dot(p.astype(BF16), vwin)
        sk = jnp.where(second, sink_ref[2 * g + 1], sink_ref[2 * g])
        m_tot = jnp.maximum(m, sk)
        w = jnp.exp(m - m_tot)
        den = l * w + jnp.exp(sk - m_tot)
        outs.append(o * (w / den))
    o_ref[:, :LANES] = jnp.where(lo, outs[0][:tq], swap(outs[0][tq:]))
    o_ref[:, LANES:] = jnp.where(lo, swap(outs[1][:tq]), outs[1][tq:])


def _window_attention(proj, sink):
    s = proj.shape[0]
    tq = 256
    n = s // tq
    hw = D_HALF_WINDOW
    per = tq // hw
    last = s // hw - 1
    kcol, vcol = (2816 - ATTN_COL0) // LANES, (2944 - ATTN_COL0) // LANES
    cur = lambda j: pl.BlockSpec((tq, LANES), lambda i: (i, j))
    prev = lambda j: pl.BlockSpec((hw, LANES), lambda i: (jnp.maximum(i * per - 1, 0), j))
    nxt = lambda j: pl.BlockSpec((hw, LANES), lambda i: (jnp.minimum((i + 1) * per, last), j))
    return pl.pallas_call(
        functools.partial(_window_kernel, seq=s),
        grid=(n,),
        in_specs=[pl.BlockSpec(memory_space=pltpu.SMEM),
                  pl.BlockSpec((tq, GROUP_W), lambda i: (i, (2560 - ATTN_COL0) // GROUP_W)),
                  prev(kcol), cur(kcol), nxt(kcol), prev(vcol), cur(vcol), nxt(vcol)],
        out_specs=pl.BlockSpec((tq, GROUP_W), lambda i: (i, 0)),
        out_shape=jax.ShapeDtypeStruct((s, GROUP_W), F32),
        scratch_shapes=[pltpu.VMEM((2 * tq, tq + 2 * hw), F32)],
        compiler_params=_cparams(1),
        name="window_attention",
    )(sink, proj, proj, proj, proj, proj, proj, proj)


def _outproj_kernel(ya_ref, yb_ref, yc_ref, yd_ref, w_ref, x_ref, g_ref, lw_ref, lb_ref, o_ref):
    y = None
    for k, ref in enumerate((ya_ref, yb_ref, yc_ref, yd_ref)):
        part = _dot(ref[...].astype(BF16), w_ref[k * GROUP_W:(k + 1) * GROUP_W, :])
        y = part if y is None else y + part
    r = DEEPNORM_ALPHA * x_ref[...] + (1.0 + g_ref[...]) * y
    o_ref[...] = _layer_norm(r, lw_ref[...], lb_ref[...])


def _outproj(ys, w_bf16, x, gate, ln_w, ln_b):
    s, d = x.shape
    tm = 512
    row = lambda i: (i, 0)
    const = lambda i: (0, 0)
    vec = pl.BlockSpec((1, d), const)
    grp = pl.BlockSpec((tm, GROUP_W), row)
    return pl.pallas_call(
        _outproj_kernel,
        grid=(s // tm,),
        in_specs=[grp, grp, grp, grp, pl.BlockSpec((4 * GROUP_W, d), const), pl.BlockSpec((tm, d), row),
                  vec, vec, vec],
        out_specs=pl.BlockSpec((tm, d), row),
        out_shape=jax.ShapeDtypeStruct((s, d), F32),
        compiler_params=_cparams(1),
        name="outproj_ln",
    )(*ys, w_bf16, x, gate, ln_w, ln_b)


FFN_CHUNK = 256


def _ffn_kernel(x_ref, sc_ref, sh_ref, wu_ref, wd_ref, g_ref, lw_ref, lb_ref, o_ref):
    x = x_ref[...]
    h = (x * (1.0 + sc_ref[...]) + sh_ref[...]).astype(BF16)
    acc = jnp.zeros(x.shape, F32)
    for c0 in range(0, FFN_DIM, FFN_CHUNK):
        gate = _dot(h, wu_ref[:, c0:c0 + FFN_CHUNK])
        up = _dot(h, wu_ref[:, FFN_DIM + c0:FFN_DIM + c0 + FFN_CHUNK])
        acc = acc + _dot((_silu(gate) * up).astype(BF16), wd_ref[c0:c0 + FFN_CHUNK, :])
    r = DEEPNORM_ALPHA * x + (1.0 + g_ref[...]) * acc
    o_ref[...] = _layer_norm(r, lw_ref[...], lb_ref[...])


def _dense_ffn(x, scale, shift, wu_bf16, wd_bf16, gate, ln_w, ln_b):
    s, d = x.shape
    tm = 512
    row = lambda i: (i, 0)
    const = lambda i: (0, 0)
    vec = pl.BlockSpec((1, d), const)
    return pl.pallas_call(
        _ffn_kernel,
        grid=(s // tm,),
        in_specs=[pl.BlockSpec((tm, d), row), vec, vec,
                  pl.BlockSpec((d, 2 * FFN_DIM), const), pl.BlockSpec((FFN_DIM, d), const), vec, vec, vec],
        out_specs=pl.BlockSpec((tm, d), row),
        out_shape=jax.ShapeDtypeStruct((s, d), F32),
        compiler_params=_cparams(1),
        name="dense_ffn_ln",
    )(x, scale, shift, wu_bf16, wd_bf16, gate, ln_w, ln_b)


ROW_TILE = 8


def _to_token_tiles(ref, val):
    n = val.shape[0]
    for c in range(ROW_TILE):
        ref[pl.ds(c, n, stride=ROW_TILE), :] = val[:, c * LANES:(c + 1) * LANES]


def _from_token_tiles(ref, n):
    return jnp.concatenate([ref[pl.ds(c, n, stride=ROW_TILE), :] for c in range(ROW_TILE)], axis=1)


def _router_kernel(x_ref, sc_ref, sh_ref, w_ref, h_ref, e_ref, p_ref, c_ref, cnt_ref):
    h = x_ref[...] * (1.0 + sc_ref[...]) + sh_ref[...]
    _to_token_tiles(h_ref, h)
    h_hi, h_lo = _split_bf16(h)
    w_hi, w_lo = _split_bf16(w_ref[...])
    logits = _dot(h_hi, w_hi) + _dot(h_hi, w_lo) + _dot(h_lo, w_hi)
    lane = lax.broadcasted_iota(jnp.int32, logits.shape, 1)
    logits = jnp.where(lane < N_EXPERTS, logits, NEG_INF)
    m1 = jnp.max(logits, axis=1, keepdims=True)
    e1 = jnp.min(jnp.where(logits == m1, lane, LANES), axis=1, keepdims=True)
    rest = jnp.where(lane == e1, NEG_INF, logits)
    m2 = jnp.max(rest, axis=1, keepdims=True)
    e2 = jnp.min(jnp.where(rest == m2, lane, LANES), axis=1, keepdims=True)
    t = jnp.exp(m2 - m1)
    g1 = 1.0 / (1.0 + t)
    p_ref[...] = jnp.where(lane == 0, g1, jnp.where(lane == 1, t * g1, 0.0))

    @pl.when(pl.program_id(0) == 0)
    def _():
        cnt_ref[...] = jnp.zeros_like(cnt_ref)

    tm = logits.shape[0]
    picks = jnp.where((lane == e1) | (lane == e2), 1.0, 0.0)
    rr = lax.broadcasted_iota(jnp.int32, (tm, tm), 0)
    cc = lax.broadcasted_iota(jnp.int32, (tm, tm), 1)
    before = _dot(jnp.where(cc < rr, 1.0, 0.0).astype(BF16), picks.astype(BF16)) + cnt_ref[...]
    r1 = jnp.sum(jnp.where(lane == e1, before, 0.0), axis=1, keepdims=True).astype(jnp.int32)
    r2 = jnp.sum(jnp.where(lane == e2, before, 0.0), axis=1, keepdims=True).astype(jnp.int32)
    cnt_ref[...] += jnp.sum(picks, axis=0, keepdims=True)
    c_ref[...] = jnp.broadcast_to(cnt_ref[...], c_ref.shape)
    e_ref[...] = jnp.where(lane == 0, e1, jnp.where(lane == 1, e2, jnp.where(lane == 2, r1, jnp.where(lane == 3, r2, 0))))


def _router(x, scale, shift, w_router):
    s, d = x.shape
    tm = 512
    row = lambda i: (i, 0)
    const = lambda i: (0, 0)
    vec = pl.BlockSpec((1, d), const)
    wpad = jnp.pad(w_router, ((0, 0), (0, LANES - N_EXPERTS)))
    return pl.pallas_call(
        _router_kernel,
        grid=(s // tm,),
        in_specs=[pl.BlockSpec((tm, d), row), vec, vec, pl.BlockSpec((d, LANES), const)],
        out_specs=[pl.BlockSpec((tm * ROW_TILE, LANES), row), pl.BlockSpec((tm, LANES), row),
                   pl.BlockSpec((tm, LANES), row), pl.BlockSpec((8, LANES), const)],
        out_shape=[jax.ShapeDtypeStruct((s * ROW_TILE, LANES), F32), jax.ShapeDtypeStruct((s, LANES), jnp.int32),
                   jax.ShapeDtypeStruct((s, LANES), F32), jax.ShapeDtypeStruct((8, LANES), F32)],
        scratch_shapes=[pltpu.VMEM((1, LANES), F32)],
        compiler_params=_cparams(1),
        name="moe_router",
    )(x, scale, shift, wpad)


MOE_TILE = 512
MOE_CHUNK = 1792
ROW_UNROLL = 8
TILE_ROWS = MOE_TILE * ROW_TILE


def _expert_kernel(te_ref, nu_ref, tokc_ref, tokn_ref, dst_ref, h_hbm, wg_ref, wu_ref, wd_ref, y_hbm,
                   xbuf, xb16, acc, ybuf, gsem, ssem):
    i, f = pl.program_id(0), pl.program_id(1)
    nt, nf = pl.num_programs(0), pl.num_programs(1)
    n_used = nu_ref[0]
    slot = i & 1
    tail = y_hbm.shape[0] - TILE_ROWS

    def rows_loop(fn):
        def body(g, carry):
            base = pl.multiple_of(g * ROW_UNROLL, ROW_UNROLL)
            for j in range(ROW_UNROLL):
                fn(base + j)
            return carry
        lax.fori_loop(0, MOE_TILE // ROW_UNROLL, body, 0)

    def token(ref, r):
        return ref.at[pl.ds(pl.multiple_of(r, ROW_TILE), ROW_TILE)]

    def gather_rows(ids_ref, s):
        rows_loop(lambda r: pltpu.make_async_copy(token(h_hbm, ids_ref[0, r]), token(xbuf.at[s], r * ROW_TILE),
                                                  gsem.at[s]).start())

    def gather_wait(s):
        pltpu.make_async_copy(h_hbm.at[pl.ds(0, TILE_ROWS)], xbuf.at[s], gsem.at[s]).wait()

    def scatter_rows(s):
        rows_loop(lambda r: pltpu.make_async_copy(token(ybuf.at[s], r * ROW_TILE), token(y_hbm, dst_ref[0, r]),
                                                  ssem.at[s]).start())

    def scatter_wait(s):
        pltpu.make_async_copy(ybuf.at[s], y_hbm.at[pl.ds(0, TILE_ROWS)], ssem.at[s]).wait()

    @pl.when(f == 0)
    def _():
        @pl.when(i == 0)
        def _():
            ybuf[1] = jnp.zeros(ybuf.shape[1:], F32)
            zero_tail = pltpu.make_async_copy(ybuf.at[1], y_hbm.at[pl.ds(tail, TILE_ROWS)], ssem.at[1])
            zero_tail.start()
            zero_tail.wait()
            gather_rows(tokc_ref, 0)

        @pl.when(i + 1 < n_used)
        def _():
            gather_rows(tokn_ref, 1 - slot)

        @pl.when(i < n_used)
        def _():
            gather_wait(slot)
            xb16[...] = _from_token_tiles(xbuf.at[slot], MOE_TILE).astype(BF16)

    @pl.when(i < n_used)
    def _():
        x = xb16[...]
        act = _silu(_dot(x, wg_ref[...])) * _dot(x, wu_ref[...])
        part = _dot(act.astype(BF16), wd_ref[...])

        @pl.when(f == 0)
        def _():
            acc[...] = part

        @pl.when((f > 0) & (f < nf - 1))
        def _():
            acc[...] += part

        @pl.when(f == nf - 1)
        def _():
            _to_token_tiles(ybuf.at[slot], acc[...] + part)

    @pl.when(f == nf - 1)
    def _():
        @pl.when((i >= 1) & (i - 1 < n_used))
        def _():
            scatter_wait(1 - slot)

        @pl.when(i < n_used)
        def _():
            scatter_rows(slot)

        @pl.when((i == nt - 1) & (i < n_used))
        def _():
            scatter_wait(slot)


def _experts(h, slot_tok, slot_dst, tile_expert, n_used, wu_bf16, wd_bf16, n_rows_out):
    d = D_MODEL
    n_tiles = slot_tok.shape[0] // MOE_TILE
    nf = EXPERT_DIM // MOE_CHUNK
    assert nf >= 2
    fidx = lambda i, f, nu: jnp.where(i < nu[0], f, nf - 1)
    ids = lambda fn: pl.BlockSpec((None, 1, MOE_TILE), lambda i, f, te, nu: (fn(i), 0, 0), memory_space=pltpu.SMEM)
    grid_spec = pltpu.PrefetchScalarGridSpec(
        num_scalar_prefetch=2,
        grid=(n_tiles, nf),
        in_specs=[ids(lambda i: i), ids(lambda i: jnp.minimum(i + 1, n_tiles - 1)), ids(lambda i: i),
                  pl.BlockSpec(memory_space=pl.ANY),
                  pl.BlockSpec((None, d, MOE_CHUNK), lambda i, f, te, nu: (te[i], 0, fidx(i, f, nu))),
                  pl.BlockSpec((None, d, MOE_CHUNK), lambda i, f, te, nu: (te[i], 0, nf + fidx(i, f, nu))),
                  pl.BlockSpec((None, MOE_CHUNK, d), lambda i, f, te, nu: (te[i], fidx(i, f, nu), 0))],
        out_specs=pl.BlockSpec(memory_space=pl.ANY),
        scratch_shapes=[pltpu.VMEM((2, TILE_ROWS, LANES), F32), pltpu.VMEM((MOE_TILE, d), BF16),
                        pltpu.VMEM((MOE_TILE, d), F32), pltpu.VMEM((2, TILE_ROWS, LANES), F32),
                        pltpu.SemaphoreType.DMA((2,)), pltpu.SemaphoreType.DMA((2,))],
    )
    tok3 = (slot_tok * ROW_TILE).reshape(n_tiles, 1, MOE_TILE)
    dst3 = (slot_dst * ROW_TILE).reshape(n_tiles, 1, MOE_TILE)
    return pl.pallas_call(
        _expert_kernel,
        grid_spec=grid_spec,
        out_shape=jax.ShapeDtypeStruct((n_rows_out * ROW_TILE, LANES), F32),
        compiler_params=_cparams(2),
        name="moe_experts",
    )(tile_expert, n_used, tok3, tok3, dst3, h, wu_bf16, wu_bf16, wd_bf16)


def _combine_kernel(x_ref, y1_ref, y2_ref, p_ref, g_ref, lw_ref, lb_ref, o_ref):
    p = p_ref[...]
    n = x_ref.shape[0]
    f = p[:, 0:1] * _from_token_tiles(y1_ref, n) + p[:, 1:2] * _from_token_tiles(y2_ref, n)
    r = DEEPNORM_ALPHA * x_ref[...] + (1.0 + g_ref[...]) * f
    o_ref[...] = _layer_norm(r, lw_ref[...], lb_ref[...])


def _combine(x, y, probs, gate, ln_w, ln_b):
    s, d = x.shape
    tm = 512
    row = lambda i: (i, 0)
    vec = pl.BlockSpec((1, d), lambda i: (0, 0))
    big = pl.BlockSpec((tm, d), row)
    return pl.pallas_call(
        _combine_kernel,
        grid=(s // tm,),
        in_specs=[big, pl.BlockSpec((tm * ROW_TILE, LANES), row),
                  pl.BlockSpec((tm * ROW_TILE, LANES), lambda i: (s // tm + i, 0)),
                  pl.BlockSpec((tm, LANES), row), vec, vec, vec],
        out_specs=big,
        out_shape=jax.ShapeDtypeStruct((s, d), F32),
        compiler_params=_cparams(1),
        name="moe_combine_ln",
    )(x, y, y, probs, gate, ln_w, ln_b)


def _moe(x, scale, shift, w_router, wu_bf16, wd_bf16, gate, ln_w, ln_b):
    t, d = x.shape
    h, sel, probs, cnt = _router(x, scale, shift, w_router)
    flat_e = sel[:, :TOP_K].reshape(-1)
    rank = sel[:, TOP_K:2 * TOP_K].reshape(-1)
    counts = cnt[0, :N_EXPERTS].astype(jnp.int32)
    n_assign = t * TOP_K
    n_tiles = -(-n_assign // MOE_TILE) + N_EXPERTS
    n_slots = n_tiles * MOE_TILE
    padded = (counts + MOE_TILE - 1) // MOE_TILE * MOE_TILE
    padded_end = jnp.cumsum(padded)
    slot = (padded_end - padded)[flat_e] + rank
    assign = jnp.full((n_slots,), -1, jnp.int32).at[slot].set(jnp.arange(n_assign, dtype=jnp.int32))
    used = assign >= 0
    slot_tok = jnp.where(used, assign >> 1, 0)
    slot_dst = jnp.where(used, (assign & 1) * t + (assign >> 1),
                         n_assign + (jnp.arange(n_slots, dtype=jnp.int32) & (MOE_TILE - 1)))
    n_used = (padded_end[-1] // MOE_TILE).astype(jnp.int32).reshape(1)
    tile_ids = jnp.minimum(jnp.arange(n_tiles, dtype=jnp.int32), n_used[0] - 1)
    tile_expert = jnp.minimum(jnp.sum((tile_ids * MOE_TILE)[:, None] >= padded_end[None, :], axis=1),
                              N_EXPERTS - 1).astype(jnp.int32)
    y = _experts(h, slot_tok, slot_dst, tile_expert, n_used, wu_bf16, wd_bf16, n_assign + MOE_TILE)
    return _combine(x, y, probs, gate, ln_w, ln_b)


def kernel(x, c, positions, w_ada, b_ada, w_in, w_out, a_lower_bound, a_norm_w, b_conv_w, b_conv_b, b_norm_w,
           b_norm_b, d_sink, ln_w, ln_b, ffn_w_up, ffn_w_down, moe_router, moe_w_up, moe_w_down):
    batch, s, d = x.shape
    assert batch == 1 and d == D_MODEL
    x = x.reshape(s, d)
    mod = _adaln(c, w_ada, b_ada)
    rope = _rope_tables(positions)
    lb_cum = jnp.cumsum(jax.nn.softmax(a_lower_bound.astype(F32), axis=0), axis=0)
    lb_all = lb_cum - lb_cum[0]
    row = lambda v: v.reshape(1, -1)
    for layer in range(DEPTH):
        shift1, scale1, gate1, shift2, scale2, gate2 = [mod[layer, :, k * d:(k + 1) * d] for k in range(6)]
        proj, attn = _inproj(x, scale1, shift1, w_in[layer].astype(BF16), rope)
        ya = _hgrn2(proj, row(lb_all[layer]), row(jnp.tile(a_norm_w[layer], GROUP_W // HEAD_DIM)))
        yb = _conv_module(proj, b_conv_w[layer], row(b_conv_b[layer]), row(b_norm_w[layer]), row(b_norm_b[layer]))
        yc = _dilated_attention(attn)
        yd = _window_attention(attn, d_sink[layer])
        x = _outproj((ya, yb, yc, yd), w_out[layer].astype(BF16), x, gate1, row(ln_w[layer, 0]), row(ln_b[layer, 0]))
        lw, lbias = row(ln_w[layer, 1]), row(ln_b[layer, 1])
        if layer % 2 == 0:
            x = _dense_ffn(x, scale2, shift2, ffn_w_up[layer // 2].astype(BF16),
                           ffn_w_down[layer // 2].astype(BF16), gate2, lw, lbias)
        else:
            x = _moe(x, scale2, shift2, moe_router[layer // 2], moe_w_up[layer // 2].astype(BF16),
                     moe_w_down[layer // 2].astype(BF16), gate2, lw, lbias)
    return x.reshape(batch, s, d)
```

```python
import functools

import numpy as np
import jax
import jax.numpy as jnp
from jax import lax
from jax.experimental import pallas as pl
from jax.experimental.pallas import tpu as pltpu

F32 = jnp.float32
BF16 = jnp.bfloat16

D_MODEL = 1024
DEPTH = 2
HEAD_DIM = 64
A_CHUNK = 16
B_KERNEL = 31
GROUP_W = 256
C_PATTERNS = ((128, 1), (512, 4), (2048, 16))
C_SIDE = 64
C_HALO = 1024
D_HALF_WINDOW = 128
ROPE_THETA = 500000.0
ROPE_DIM = HEAD_DIM // 4
FFN_DIM = 2816
N_EXPERTS = 8
TOP_K = 2
EXPERT_DIM = 3584
IN_WIDTH = 3072
DEEPNORM_ALPHA = (2 * DEPTH) ** 0.25
LN_EPS = 1e-5
RMS_EPS = 1e-6
NEG_INF = -1e30

LANES = 128
SUBLANES = 8
VMEM_LIMIT = 56 * 1024 * 1024


def _cparams(n_axes=1, vmem=VMEM_LIMIT):
    return pltpu.CompilerParams(dimension_semantics=("arbitrary",) * n_axes, vmem_limit_bytes=vmem)


def _sigmoid(x):
    return 1.0 / (1.0 + jnp.exp(-x))


def _silu(x):
    return x * _sigmoid(x)


def _split_bf16(x):
    hi = x.astype(BF16)
    lo = (x - hi.astype(F32)).astype(BF16)
    return hi, lo


def _layer_norm(r, w, b):
    mu = jnp.mean(r, axis=-1, keepdims=True)
    d = r - mu
    var = jnp.mean(d * d, axis=-1, keepdims=True)
    return d * lax.rsqrt(var + LN_EPS) * w + b


def _dot_nt(a, b):
    return lax.dot_general(a, b, (((1,), (1,)), ((), ())), preferred_element_type=F32)


def _dot_tn(a, b):
    return lax.dot_general(a, b, (((0,), (0,)), ((), ())), preferred_element_type=F32)


def _dot(a, b):
    return jnp.dot(a, b, preferred_element_type=F32)


def _adaln_kernel(c_ref, w_ref, b_ref, o_ref):
    a_hi, a_lo = _split_bf16(_silu(c_ref[...]))
    w_hi, w_lo = _split_bf16(w_ref[...])
    o_ref[...] = _dot(a_hi, w_hi) + _dot(a_hi, w_lo) + _dot(a_lo, w_hi) + b_ref[...]


def _adaln(c, w_ada, b_ada):
    depth, d, n = w_ada.shape
    tn = 1536
    c8 = jnp.broadcast_to(c.reshape(1, d), (8, d))
    out = pl.pallas_call(
        _adaln_kernel,
        grid=(depth, n // tn),
        in_specs=[pl.BlockSpec((8, d), lambda l, j: (0, 0)),
                  pl.BlockSpec((None, d, tn), lambda l, j: (l, 0, j)),
                  pl.BlockSpec((None, 1, tn), lambda l, j: (l, 0, j))],
        out_specs=pl.BlockSpec((None, 8, tn), lambda l, j: (l, 0, j)),
        out_shape=jax.ShapeDtypeStruct((depth, 8, n), F32),
        compiler_params=_cparams(2),
        name="adaln",
    )(c8, w_ada, b_ada.reshape(depth, 1, n))
    return out[:, 0:1, :]


def _rope_kernel(pos_ref, invf_ref, c_ref, s1_ref, s2_ref):
    ang = pos_ref[...].astype(F32) * invf_ref[...]
    j = lax.broadcasted_iota(jnp.int32, ang.shape, 1) & (HEAD_DIM - 1)
    cosv, sinv = jnp.cos(ang), jnp.sin(ang)
    half = ROPE_DIM // 2
    c_ref[...] = jnp.where(j < ROPE_DIM, cosv, 1.0)
    s1_ref[...] = jnp.where(j < half, -sinv, 0.0)
    s2_ref[...] = jnp.where((j >= half) & (j < ROPE_DIM), sinv, 0.0)


def _rope_tables(positions):
    s = positions.shape[-1]
    ts = 2048
    half = ROPE_DIM // 2
    inv_freq = np.float32(ROPE_THETA) ** (-np.arange(half, dtype=np.float32) * np.float32(2.0 / ROPE_DIM))
    invf = jnp.asarray(inv_freq[(np.arange(LANES) % HEAD_DIM) % half].reshape(1, LANES).astype(np.float32))
    spec = pl.BlockSpec((ts, LANES), lambda i: (i, 0))
    return pl.pallas_call(
        _rope_kernel,
        grid=(s // ts,),
        in_specs=[pl.BlockSpec((ts, 1), lambda i: (i, 0)), pl.BlockSpec((1, LANES), lambda i: (0, 0))],
        out_specs=[spec, spec, spec],
        out_shape=[jax.ShapeDtypeStruct((s, LANES), F32)] * 3,
        compiler_params=_cparams(1),
        name="rope_tables",
    )(positions.reshape(s, 1), invf)


_ROPE_COL_CHUNKS = (1792, 1920, 2048, 2176, 2560, 2688, 2816)


ATTN_COL0 = 1792
ATTN_WIDTH = IN_WIDTH - ATTN_COL0


def _inproj_kernel(x_ref, sc_ref, sh_ref, w_ref, rc_ref, rs1_ref, rs2_ref, o_ref, oa_ref):
    h = (x_ref[...] * (1.0 + sc_ref[...]) + sh_ref[...]).astype(BF16)
    half = ROPE_DIM // 2
    for c0 in range(0, IN_WIDTH, GROUP_W):
        acc = _dot(h, w_ref[:, c0:c0 + GROUP_W])
        for k in range(GROUP_W // LANES):
            col = c0 + k * LANES
            part = acc[:, k * LANES:(k + 1) * LANES]
            if col in _ROPE_COL_CHUNKS:
                part = (part * rc_ref[...] + pltpu.roll(part, LANES - half, 1) * rs1_ref[...]
                        + pltpu.roll(part, half, 1) * rs2_ref[...])
            if col < ATTN_COL0:
                o_ref[:, col:col + LANES] = part
            else:
                oa_ref[:, col - ATTN_COL0:col - ATTN_COL0 + LANES] = part.astype(BF16)


def _inproj(x, scale, shift, w_bf16, rope):
    s, d = x.shape
    tm = 512
    row = lambda i: (i, 0)
    const = lambda i: (0, 0)
    return pl.pallas_call(
        _inproj_kernel,
        grid=(s // tm,),
        in_specs=[pl.BlockSpec((tm, d), row), pl.BlockSpec((1, d), const), pl.BlockSpec((1, d), const),
                  pl.BlockSpec((d, IN_WIDTH), const),
                  pl.BlockSpec((tm, LANES), row), pl.BlockSpec((tm, LANES), row), pl.BlockSpec((tm, LANES), row)],
        out_specs=[pl.BlockSpec((tm, ATTN_COL0), row), pl.BlockSpec((tm, ATTN_WIDTH), row)],
        out_shape=[jax.ShapeDtypeStruct((s, ATTN_COL0), F32), jax.ShapeDtypeStruct((s, ATTN_WIDTH), BF16)],
        compiler_params=_cparams(1),
        name="inproj",
    )(x, scale, shift, w_bf16, *rope)


HG_TILE = 512
HG_CHUNKS = HG_TILE // A_CHUNK
N_HEADS = GROUP_W // HEAD_DIM


def _same_head_ones():
    li = lax.broadcasted_iota(jnp.int32, (GROUP_W, GROUP_W), 0)
    lj = lax.broadcasted_iota(jnp.int32, (GROUP_W, GROUP_W), 1)
    return jnp.where((li >> 6) == (lj >> 6), 1.0, 0.0).astype(BF16)


def _hgrn_core(q_ref, z_ref, v_ref, lb_ref, st_ref, qs, zs, vs, qt, kt, ot, oi_ref, rev):
    nb = HG_CHUNKS
    lb = lb_ref[...]
    for src, dst in ((q_ref, qs), (z_ref, zs), (v_ref, vs)):
        val = src[...]
        dst[0] = val[:, :LANES]
        dst[1] = val[:, LANES:]

    def by_pos(buf, j):
        sel = pl.ds(j, nb, stride=A_CHUNK)
        return jnp.concatenate([buf[0, sel, :], buf[1, sel, :]], axis=1)

    order = list(range(A_CHUNK))[::-1] if rev else list(range(A_CHUNK))
    q, f, k, v = [], [], [], []
    for j in order:
        aq, z = by_pos(qs, j), by_pos(zs, j)
        q.append(_silu(aq))
        f.append(lb + (1.0 - lb) * _sigmoid(z))
        k.append((1.0 - lb) * _sigmoid(-z))
        v.append(by_pos(vs, j))
    incl = [f[0]]
    for p in range(1, A_CHUNK):
        incl.append(incl[-1] * f[p])
    rest = [None] * A_CHUNK
    for p in range(A_CHUNK - 2, -1, -1):
        rest[p] = f[p + 1] if rest[p + 1] is None else rest[p + 1] * f[p + 1]
    dec = incl[-1]

    head = lax.broadcasted_iota(jnp.int32, (1, GROUP_W), 1) >> 6
    lo = lax.broadcasted_iota(jnp.int32, (1, LANES), 1) < HEAD_DIM
    ones_bd = _same_head_ones()

    def put(dst, sel, val):
        dst[0, sel, :] = val[:, :LANES]
        dst[1, sel, :] = val[:, LANES:]

    for p, j in enumerate(order):
        xs, decay = [], None
        for d in range(p + 1):
            x = q[p] * k[p - d]
            xs.append((x if decay is None else x * decay).astype(BF16))
            decay = f[p - d] if decay is None else decay * f[p - d]
        sc = _dot(jnp.concatenate(xs, axis=0), ones_bd)
        o = sc[0:nb] * v[p]
        for d in range(1, p + 1):
            o = o + sc[d * nb:(d + 1) * nb] * v[p - d]
        sel = pl.ds(j, nb, stride=A_CHUNK)
        put(ot, sel, o)
        put(qt, sel, q[p] * incl[p])
        put(kt, sel, k[p] if rest[p] is None else k[p] * rest[p])

    def per_head(x):
        return jnp.concatenate([jnp.where(head == h, x, 0.0) for h in range(N_HEADS)], axis=0).astype(BF16)

    chunks = range(nb - 1, -1, -1) if rev else range(nb)
    state = st_ref[...]
    for n in chunks:
        r1 = slice(n * A_CHUNK, (n + 1) * A_CHUNK)
        lq = per_head(jnp.concatenate([qt[0, r1, :], qt[1, r1, :]], axis=1))
        rk = per_head(jnp.concatenate([kt[0, r1, :], kt[1, r1, :]], axis=1))
        va, vb = v_ref[r1, :LANES], v_ref[r1, LANES:]
        lv = jnp.concatenate([va, pltpu.roll(va, HEAD_DIM, 1), vb, pltpu.roll(vb, HEAD_DIM, 1)], axis=0).astype(BF16)
        sb = state.astype(BF16)
        res = _dot_nt(lq, jnp.concatenate([sb, sb], axis=0))
        oi_ref[r1, :LANES] = jnp.where(lo, res[0:A_CHUNK], res[A_CHUNK:2 * A_CHUNK])
        oi_ref[r1, LANES:] = jnp.where(lo, res[2 * A_CHUNK:3 * A_CHUNK], res[3 * A_CHUNK:])
        kv = _dot_tn(lv, rk)[:HEAD_DIM]
        state = state * dec[n:n + 1] + kv
    st_ref[...] = state
    full = slice(0, HG_TILE)
    return jnp.concatenate([ot[0, full, :], ot[1, full, :]], axis=1) + oi_ref[...]


def _hgrn_fwd_kernel(q_ref, f_ref, i_ref, lb_ref, o_ref, st_ref, *scratch):
    @pl.when(pl.program_id(0) == 0)
    def _():
        st_ref[...] = jnp.zeros_like(st_ref)

    o_ref[...] = _hgrn_core(q_ref, f_ref, i_ref, lb_ref, st_ref, *scratch, rev=False)


def _hgrn_bwd_kernel(q_ref, f_ref, i_ref, g_ref, of_ref, lb_ref, nw_ref, o_ref, st_ref, *scratch):
    @pl.when(pl.program_id(0) == 0)
    def _():
        st_ref[...] = jnp.zeros_like(st_ref)

    o = of_ref[...] + _hgrn_core(q_ref, f_ref, i_ref, lb_ref, st_ref, *scratch, rev=True)
    ones_bd = _same_head_ones()
    sq_hi, sq_lo = _split_bf16(o * o)
    ms = (_dot(sq_hi, ones_bd) + _dot(sq_lo, ones_bd)) * (1.0 / HEAD_DIM)
    o_ref[...] = o * lax.rsqrt(ms + RMS_EPS) * nw_ref[...] * _silu(g_ref[...])


def _hgrn_scratch():
    split = lambda rows: pltpu.VMEM((2, rows, LANES), F32)
    return [pltpu.VMEM((HEAD_DIM, GROUP_W), F32),
            split(HG_TILE), split(HG_TILE), split(HG_TILE),
            split(HG_TILE), split(HG_TILE), split(HG_TILE), pltpu.VMEM((HG_TILE, GROUP_W), F32)]


def _hgrn2(proj, lb, norm_w):
    s = proj.shape[0]
    ts = HG_TILE
    n = s // ts
    col = lambda j: pl.BlockSpec((ts, GROUP_W), lambda i: (i, j))
    rcol = lambda j: pl.BlockSpec((ts, GROUP_W), lambda i: (n - 1 - i, j))
    vec = pl.BlockSpec((1, GROUP_W), lambda i: (0, 0))
    o_f = pl.pallas_call(
        _hgrn_fwd_kernel,
        grid=(n,),
        in_specs=[col(0), col(1), col(3), vec],
        out_specs=pl.BlockSpec((ts, GROUP_W), lambda i: (i, 0)),
        out_shape=jax.ShapeDtypeStruct((s, GROUP_W), F32),
        scratch_shapes=_hgrn_scratch(),
        compiler_params=_cparams(1),
        name="hgrn_fwd",
    )(proj, proj, proj, lb)
    return pl.pallas_call(
        _hgrn_bwd_kernel,
        grid=(n,),
        in_specs=[rcol(0), rcol(2), rcol(3), rcol(4), pl.BlockSpec((ts, GROUP_W), lambda i: (n - 1 - i, 0)),
                  vec, vec],
        out_specs=pl.BlockSpec((ts, GROUP_W), lambda i: (n - 1 - i, 0)),
        out_shape=jax.ShapeDtypeStruct((s, GROUP_W), F32),
        scratch_shapes=_hgrn_scratch(),
        compiler_params=_cparams(1),
        name="hgrn_bwd",
    )(proj, proj, proj, proj, o_f, lb, norm_w)


B_PAD = 16


def _conv_kernel(v_ref, g_ref, vp_ref, gp_ref, vn_ref, gn_ref, w_ref, b_ref, nw_ref, nb_ref, o_ref, ubuf):
    i = pl.program_id(0)
    ts = v_ref.shape[0]
    glu = lambda a, b: a * _sigmoid(b)
    ubuf[0:B_PAD, :] = jnp.where(i > 0, glu(vp_ref[...], gp_ref[...]), 0.0)
    ubuf[B_PAD:B_PAD + ts, :] = glu(v_ref[...], g_ref[...])
    ubuf[B_PAD + ts:, :] = jnp.where(i < pl.num_programs(0) - 1, glu(vn_ref[...], gn_ref[...]), 0.0)
    acc = jnp.zeros((ts, GROUP_W), F32) + b_ref[...]
    span = ts + 2 * B_PAD - SUBLANES
    shifted = [ubuf[pl.ds(r, span), :] for r in range(SUBLANES)]
    for j in range(B_KERNEL):
        off = B_PAD - B_KERNEL // 2 + j
        base = off - off % SUBLANES
        acc = acc + shifted[off % SUBLANES][base:base + ts] * w_ref[j:j + 1, :]
    o_ref[...] = _silu(_layer_norm(acc, nw_ref[...], nb_ref[...]))


def _conv_module(proj, conv_w, conv_b, norm_w, norm_b):
    s = proj.shape[0]
    ts = 512
    n = s // ts
    per = ts // B_PAD
    last = s // B_PAD - 1
    cur = lambda j: pl.BlockSpec((ts, GROUP_W), lambda i: (i, j))
    prev = lambda j: pl.BlockSpec((B_PAD, GROUP_W), lambda i: (jnp.maximum(i * per - 1, 0), j))
    nxt = lambda j: pl.BlockSpec((B_PAD, GROUP_W), lambda i: (jnp.minimum((i + 1) * per, last), j))
    vec = pl.BlockSpec((1, GROUP_W), lambda i: (0, 0))
    wpad = jnp.pad(conv_w, ((0, 1), (0, 0)))
    return pl.pallas_call(
        _conv_kernel,
        grid=(n,),
        in_specs=[cur(5), cur(6), prev(5), prev(6), nxt(5), nxt(6),
                  pl.BlockSpec((B_KERNEL + 1, GROUP_W), lambda i: (0, 0)), vec, vec, vec],
        out_specs=pl.BlockSpec((ts, GROUP_W), lambda i: (i, 0)),
        out_shape=jax.ShapeDtypeStruct((s, GROUP_W), F32),
        scratch_shapes=[pltpu.VMEM((ts + 2 * B_PAD, GROUP_W), F32)],
        compiler_params=_cparams(1),
        name="conv_module",
    )(proj, proj, proj, proj, proj, proj, wpad, conv_b, norm_w, norm_b)


C_MAX_BLOCK = 128


def _dilated_block_sizes(span):
    return sorted({min(span // dil, C_MAX_BLOCK) for _, dil in C_PATTERNS})


def _dilated_kernel(q_ref, kp_ref, kc_ref, kn_ref, vp_ref, vc_ref, vn_ref, o_ref, qw, kw, vw, ob, mb, lb, *band_refs,
                    seq, span):
    i = pl.program_id(0)
    band = dict(zip(_dilated_block_sizes(span), band_refs))

    @pl.when(i == 0)
    def _():
        for nq, ref in band.items():
            a = lax.broadcasted_iota(jnp.int32, ref.shape, 0) & (nq - 1)
            b = lax.broadcasted_iota(jnp.int32, ref.shape, 1)
            ref[...] = jnp.where(jnp.abs(a - (b - C_SIDE)) <= C_SIDE, 0.0, NEG_INF)

    def put(dst, row0, val):
        n = val.shape[0]
        dst[0, row0:row0 + n, :] = val[:, :LANES]
        dst[1, row0:row0 + n, :] = val[:, LANES:]

    put(qw, 0, q_ref[...].astype(F32) * (HEAD_DIM ** -0.5))
    for dst, refs in ((kw, (kp_ref, kc_ref, kn_ref)), (vw, (vp_ref, vc_ref, vn_ref))):
        put(dst, 0, refs[0][...].astype(F32))
        put(dst, C_HALO, refs[1][...].astype(F32))
        put(dst, C_HALO + span, refs[2][...].astype(F32))

    head = lax.broadcasted_iota(jnp.int32, (1, GROUP_W), 1) >> 6
    heads = [head == h for h in range(4)]

    def rows(start, size, dil):
        return pl.ds(start, size) if dil == 1 else pl.ds(start, size, stride=dil)

    def load(src, sel):
        return jnp.concatenate([src[0, sel, :], src[1, sel, :]], axis=1)

    def store(dst, sel, val):
        dst[0, sel, :] = val[:, :LANES]
        dst[1, sel, :] = val[:, LANES:]

    def by_head(x, nq):
        out = jnp.where(heads[0], x[0:nq], 0.0)
        for h in range(1, 4):
            out = out + jnp.where(heads[h], x[h * nq:(h + 1) * nq], 0.0)
        return out

    def cols_by_head(col, nq):
        out = jnp.broadcast_to(col[3 * nq:4 * nq], (nq, GROUP_W))
        for h in range(2, -1, -1):
            out = jnp.where(heads[h], col[h * nq:(h + 1) * nq], out)
        return out

    def attend(first, r, dil, blk, nq):
        nk = nq + 2 * C_SIDE
        qrows = rows(r + dil * blk * nq, nq, dil)
        koff = C_HALO + r + dil * (blk * nq - C_SIDE)
        qs = load(qw, qrows)
        ks = load(kw, rows(koff, nk, dil)).astype(BF16)
        vs = load(vw, rows(koff, nk, dil)).astype(BF16)
        lhs = jnp.concatenate([jnp.where(hm, qs, 0.0) for hm in heads], axis=0).astype(BF16)
        kpos = i * span + (koff - C_HALO) + dil * lax.broadcasted_iota(jnp.int32, (1, nk), 1)
        edge = jnp.where((kpos >= 0) & (kpos < seq), 0.0, NEG_INF)
        sc = _dot_nt(lhs, ks) + band[nq][...] + edge
        m = jnp.max(sc, axis=1, keepdims=True)
        p = jnp.exp(sc - m)
        l = jnp.sum(p, axis=1, keepdims=True)
        o_new = by_head(_dot(p.astype(BF16), vs), nq)
        m_new = cols_by_head(m, nq)
        l_new = cols_by_head(l, nq)
        if first:
            store(ob, qrows, o_new)
            store(mb, qrows, m_new)
            store(lb, qrows, l_new)
        else:
            m_old = load(mb, qrows)
            m_tot = jnp.maximum(m_old, m_new)
            w_old, w_new = jnp.exp(m_old - m_tot), jnp.exp(m_new - m_tot)
            store(ob, qrows, w_old * load(ob, qrows) + w_new * o_new)
            store(lb, qrows, w_old * load(lb, qrows) + w_new * l_new)
            store(mb, qrows, m_tot)

    for pat, (_, dil) in enumerate(C_PATTERNS):
        sub = span // dil
        nq = min(sub, C_MAX_BLOCK)
        for r in range(dil):
            for blk in range(sub // nq):
                attend(pat == 0, r, dil, blk, nq)
    full = slice(0, span)
    o_ref[...] = load(ob, full) / load(lb, full)


def _dilated_attention(proj):
    s = proj.shape[0]
    span = 1024
    n = s // span
    per = span // C_HALO
    last = s // C_HALO - 1
    cur = lambda j: pl.BlockSpec((span, GROUP_W), lambda i: (i, j))
    prev = lambda j: pl.BlockSpec((C_HALO, GROUP_W), lambda i: (jnp.maximum(i * per - 1, 0), j))
    nxt = lambda j: pl.BlockSpec((C_HALO, GROUP_W), lambda i: (jnp.minimum((i + 1) * per, last), j))
    win = span + 2 * C_HALO
    return pl.pallas_call(
        functools.partial(_dilated_kernel, seq=s, span=span),
        grid=(n,),
        in_specs=[cur(0), prev(1), cur(1), nxt(1), prev(2), cur(2), nxt(2)],
        out_specs=pl.BlockSpec((span, GROUP_W), lambda i: (i, 0)),
        out_shape=jax.ShapeDtypeStruct((s, GROUP_W), F32),
        scratch_shapes=[pltpu.VMEM((2, span, LANES), F32),
                        pltpu.VMEM((2, win, LANES), F32), pltpu.VMEM((2, win, LANES), F32),
                        pltpu.VMEM((2, span, LANES), F32), pltpu.VMEM((2, span, LANES), F32),
                        pltpu.VMEM((2, span, LANES), F32)]
        + [pltpu.VMEM((N_HEADS * nq, nq + 2 * C_SIDE), F32) for nq in _dilated_block_sizes(span)],
        compiler_params=_cparams(1),
        name="dilated_attention",
    )(proj, proj, proj, proj, proj, proj, proj)


def _window_kernel(sink_ref, q_ref, kp_ref, kc_ref, kn_ref, vp_ref, vc_ref, vn_ref, o_ref, band_ref, *, seq):
    i = pl.program_id(0)
    tq = q_ref.shape[0]
    hw = D_HALF_WINDOW
    nk = tq + 2 * hw
    kwin = jnp.concatenate([kp_ref[...], kc_ref[...], kn_ref[...]], axis=0).astype(BF16)
    vwin = jnp.concatenate([vp_ref[...], vc_ref[...], vn_ref[...]], axis=0).astype(BF16)
    lo = lax.broadcasted_iota(jnp.int32, (1, LANES), 1) < HEAD_DIM
    q = q_ref[...].astype(F32) * (HEAD_DIM ** -0.5)
    qa, qb = q[:, :LANES], q[:, LANES:]
    swap = lambda x: pltpu.roll(x, HEAD_DIM, 1)
    lhs = (jnp.concatenate([jnp.where(lo, qa, 0.0), jnp.where(lo, swap(qa), 0.0)], axis=0),
           jnp.concatenate([jnp.where(lo, 0.0, swap(qb)), jnp.where(lo, 0.0, qb)], axis=0))
    @pl.when(i == 0)
    def _():
        a = lax.broadcasted_iota(jnp.int32, (2 * tq, nk), 0) & (tq - 1)
        b = lax.broadcasted_iota(jnp.int32, (2 * tq, nk), 1)
        band_ref[...] = jnp.where(jnp.abs(b - hw - a) <= hw, 0.0, NEG_INF)

    kpos = i * tq - hw + lax.broadcasted_iota(jnp.int32, (1, nk), 1)
    bias = band_ref[...] + jnp.where((kpos >= 0) & (kpos < seq), 0.0, NEG_INF)
    second = lax.broadcasted_iota(jnp.int32, (2 * tq, 1), 0) >= tq
    outs = []
    for g in range(2):
        sc = _dot_nt(lhs[g].astype(BF16), kwin) + bias
        m = jnp.max(sc, axis=1, keepdims=True)
        p = jnp.exp(sc - m)
        l = jnp.sum(p, axis=1, keepdims=True)
        o = _dot(p.astype(BF16), vwin)
        sk = jnp.where(second, sink_ref[2 * g + 1], sink_ref[2 * g])
        m_tot = jnp.maximum(m, sk)
        w = jnp.exp(m - m_tot)
        den = l * w + jnp.exp(sk - m_tot)
        outs.append(o * (w / den))
    o_ref[:, :LANES] = jnp.where(lo, outs[0][:tq], swap(outs[0][tq:]))
    o_ref[:, LANES:] = jnp.where(lo, swap(outs[1][:tq]), outs[1][tq:])


def _window_attention(proj, sink):
    s = proj.shape[0]
    tq = 256
    n = s // tq
    hw = D_HALF_WINDOW
    per = tq // hw
    last = s // hw - 1
    kcol, vcol = (2816 - ATTN_COL0) // LANES, (2944 - ATTN_COL0) // LANES
    cur = lambda j: pl.BlockSpec((tq, LANES), lambda i: (i, j))
    prev = lambda j: pl.BlockSpec((hw, LANES), lambda i: (jnp.maximum(i * per - 1, 0), j))
    nxt = lambda j: pl.BlockSpec((hw, LANES), lambda i: (jnp.minimum((i + 1) * per, last), j))
    return pl.pallas_call(
        functools.partial(_window_kernel, seq=s),
        grid=(n,),
        in_specs=[pl.BlockSpec(memory_space=pltpu.SMEM),
                  pl.BlockSpec((tq, GROUP_W), lambda i: (i, (2560 - ATTN_COL0) // GROUP_W)),
                  prev(kcol), cur(kcol), nxt(kcol), prev(vcol), cur(vcol), nxt(vcol)],
        out_specs=pl.BlockSpec((tq, GROUP_W), lambda i: (i, 0)),
        out_shape=jax.ShapeDtypeStruct((s, GROUP_W), F32),
        scratch_shapes=[pltpu.VMEM((2 * tq, tq + 2 * hw), F32)],
        compiler_params=_cparams(1),
        name="window_attention",
    )(sink, proj, proj, proj, proj, proj, proj, proj)


def _outproj_kernel(ya_ref, yb_ref, yc_ref, yd_ref, w_ref, x_ref, g_ref, lw_ref, lb_ref, o_ref):
    y = None
    for k, ref in enumerate((ya_ref, yb_ref, yc_ref, yd_ref)):
        part = _dot(ref[...].astype(BF16), w_ref[k * GROUP_W:(k + 1) * GROUP_W, :])
        y = part if y is None else y + part
    r = DEEPNORM_ALPHA * x_ref[...] + (1.0 + g_ref[...]) * y
    o_ref[...] = _layer_norm(r, lw_ref[...], lb_ref[...])


def _outproj(ys, w_bf16, x, gate, ln_w, ln_b):
    s, d = x.shape
    tm = 512
    row = lambda i: (i, 0)
    const = lambda i: (0, 0)
    vec = pl.BlockSpec((1, d), const)
    grp = pl.BlockSpec((tm, GROUP_W), row)
    return pl.pallas_call(
        _outproj_kernel,
        grid=(s // tm,),
        in_specs=[grp, grp, grp, grp, pl.BlockSpec((4 * GROUP_W, d), const), pl.BlockSpec((tm, d), row),
                  vec, vec, vec],
        out_specs=pl.BlockSpec((tm, d), row),
        out_shape=jax.ShapeDtypeStruct((s, d), F32),
        compiler_params=_cparams(1),
        name="outproj_ln",
    )(*ys, w_bf16, x, gate, ln_w, ln_b)


FFN_CHUNK = 256


def _ffn_kernel(x_ref, sc_ref, sh_ref, wu_ref, wd_ref, g_ref, lw_ref, lb_ref, o_ref):
    x = x_ref[...]
    h = (x * (1.0 + sc_ref[...]) + sh_ref[...]).astype(BF16)
    acc = jnp.zeros(x.shape, F32)
    for c0 in range(0, FFN_DIM, FFN_CHUNK):
        gate = _dot(h, wu_ref[:, c0:c0 + FFN_CHUNK])
        up = _dot(h, wu_ref[:, FFN_DIM + c0:FFN_DIM + c0 + FFN_CHUNK])
        acc = acc + _dot((_silu(gate) * up).astype(BF16), wd_ref[c0:c0 + FFN_CHUNK, :])
    r = DEEPNORM_ALPHA * x + (1.0 + g_ref[...]) * acc
    o_ref[...] = _layer_norm(r, lw_ref[...], lb_ref[...])


def _dense_ffn(x, scale, shift, wu_bf16, wd_bf16, gate, ln_w, ln_b):
    s, d = x.shape
    tm = 512
    row = lambda i: (i, 0)
    const = lambda i: (0, 0)
    vec = pl.BlockSpec((1, d), const)
    return pl.pallas_call(
        _ffn_kernel,
        grid=(s // tm,),
        in_specs=[pl.BlockSpec((tm, d), row), vec, vec,
                  pl.BlockSpec((d, 2 * FFN_DIM), const), pl.BlockSpec((FFN_DIM, d), const), vec, vec, vec],
        out_specs=pl.BlockSpec((tm, d), row),
        out_shape=jax.ShapeDtypeStruct((s, d), F32),
        compiler_params=_cparams(1),
        name="dense_ffn_ln",
    )(x, scale, shift, wu_bf16, wd_bf16, gate, ln_w, ln_b)


ROW_TILE = 8


def _to_token_tiles(ref, val):
    n = val.shape[0]
    for c in range(ROW_TILE):
        ref[pl.ds(c, n, stride=ROW_TILE), :] = val[:, c * LANES:(c + 1) * LANES]


def _from_token_tiles(ref, n):
    return jnp.concatenate([ref[pl.ds(c, n, stride=ROW_TILE), :] for c in range(ROW_TILE)], axis=1)


def _router_kernel(x_ref, sc_ref, sh_ref, w_ref, h_ref, e_ref, p_ref, c_ref, cnt_ref):
    h = x_ref[...] * (1.0 + sc_ref[...]) + sh_ref[...]
    _to_token_tiles(h_ref, h)
    h_hi, h_lo = _split_bf16(h)
    w_hi, w_lo = _split_bf16(w_ref[...])
    logits = _dot(h_hi, w_hi) + _dot(h_hi, w_lo) + _dot(h_lo, w_hi)
    lane = lax.broadcasted_iota(jnp.int32, logits.shape, 1)
    logits = jnp.where(lane < N_EXPERTS, logits, NEG_INF)
    m1 = jnp.max(logits, axis=1, keepdims=True)
    e1 = jnp.min(jnp.where(logits == m1, lane, LANES), axis=1, keepdims=True)
    rest = jnp.where(lane == e1, NEG_INF, logits)
    m2 = jnp.max(rest, axis=1, keepdims=True)
    e2 = jnp.min(jnp.where(rest == m2, lane, LANES), axis=1, keepdims=True)
    t = jnp.exp(m2 - m1)
    g1 = 1.0 / (1.0 + t)
    p_ref[...] = jnp.where(lane == 0, g1, jnp.where(lane == 1, t * g1, 0.0))

    @pl.when(pl.program_id(0) == 0)
    def _():
        cnt_ref[...] = jnp.zeros_like(cnt_ref)

    tm = logits.shape[0]
    picks = jnp.where((lane == e1) | (lane == e2), 1.0, 0.0)
    rr = lax.broadcasted_iota(jnp.int32, (tm, tm), 0)
    cc = lax.broadcasted_iota(jnp.int32, (tm, tm), 1)
    before = _dot(jnp.where(cc < rr, 1.0, 0.0).astype(BF16), picks.astype(BF16)) + cnt_ref[...]
    r1 = jnp.sum(jnp.where(lane == e1, before, 0.0), axis=1, keepdims=True).astype(jnp.int32)
    r2 = jnp.sum(jnp.where(lane == e2, before, 0.0), axis=1, keepdims=True).astype(jnp.int32)
    cnt_ref[...] += jnp.sum(picks, axis=0, keepdims=True)
    c_ref[...] = jnp.broadcast_to(cnt_ref[...], c_ref.shape)
    e_ref[...] = jnp.where(lane == 0, e1, jnp.where(lane == 1, e2, jnp.where(lane == 2, r1, jnp.where(lane == 3, r2, 0))))


def _router(x, scale, shift, w_router):
    s, d = x.shape
    tm = 512
    row = lambda i: (i, 0)
    const = lambda i: (0, 0)
    vec = pl.BlockSpec((1, d), const)
    wpad = jnp.pad(w_router, ((0, 0), (0, LANES - N_EXPERTS)))
    return pl.pallas_call(
        _router_kernel,
        grid=(s // tm,),
        in_specs=[pl.BlockSpec((tm, d), row), vec, vec, pl.BlockSpec((d, LANES), const)],
        out_specs=[pl.BlockSpec((tm * ROW_TILE, LANES), row), pl.BlockSpec((tm, LANES), row),
                   pl.BlockSpec((tm, LANES), row), pl.BlockSpec((8, LANES), const)],
        out_shape=[jax.ShapeDtypeStruct((s * ROW_TILE, LANES), F32), jax.ShapeDtypeStruct((s, LANES), jnp.int32),
                   jax.ShapeDtypeStruct((s, LANES), F32), jax.ShapeDtypeStruct((8, LANES), F32)],
        scratch_shapes=[pltpu.VMEM((1, LANES), F32)],
        compiler_params=_cparams(1),
        name="moe_router",
    )(x, scale, shift, wpad)


MOE_TILE = 512
MOE_CHUNK = 1792
ROW_UNROLL = 16
TILE_ROWS = MOE_TILE * ROW_TILE


def _expert_kernel(te_ref, nu_ref, tokc_ref, tokn_ref, dst_ref, h_hbm, wg_ref, wu_ref, wd_ref, y_hbm,
                   xbuf, xb16, acc, ybuf, gsem, ssem):
    i, f = pl.program_id(0), pl.program_id(1)
    nt, nf = pl.num_programs(0), pl.num_programs(1)
    n_used = nu_ref[0]
    slot = i & 1
    tail = y_hbm.shape[0] - TILE_ROWS

    def rows_loop(fn):
        def body(g, carry):
            base = pl.multiple_of(g * ROW_UNROLL, ROW_UNROLL)
            for j in range(ROW_UNROLL):
                fn(base + j)
            return carry
        lax.fori_loop(0, MOE_TILE // ROW_UNROLL, body, 0)

    def token(ref, r):
        return ref.at[pl.ds(pl.multiple_of(r, ROW_TILE), ROW_TILE)]

    def gather_rows(ids_ref, s):
        rows_loop(lambda r: pltpu.make_async_copy(token(h_hbm, ids_ref[0, r]), token(xbuf.at[s], r * ROW_TILE),
                                                  gsem.at[s]).start())

    def gather_wait(s):
        pltpu.make_async_copy(h_hbm.at[pl.ds(0, TILE_ROWS)], xbuf.at[s], gsem.at[s]).wait()

    def scatter_rows(s):
        rows_loop(lambda r: pltpu.make_async_copy(token(ybuf.at[s], r * ROW_TILE), token(y_hbm, dst_ref[0, r]),
                                                  ssem.at[s]).start())

    def scatter_wait(s):
        pltpu.make_async_copy(ybuf.at[s], y_hbm.at[pl.ds(0, TILE_ROWS)], ssem.at[s]).wait()

    @pl.when(f == 0)
    def _():
        @pl.when(i == 0)
        def _():
            ybuf[1] = jnp.zeros(ybuf.shape[1:], F32)
            zero_tail = pltpu.make_async_copy(ybuf.at[1], y_hbm.at[pl.ds(tail, TILE_ROWS)], ssem.at[1])
            zero_tail.start()
            zero_tail.wait()
            gather_rows(tokc_ref, 0)

        @pl.when(i + 1 < n_used)
        def _():
            gather_rows(tokn_ref, 1 - slot)

        @pl.when(i < n_used)
        def _():
            gather_wait(slot)
            xb16[...] = _from_token_tiles(xbuf.at[slot], MOE_TILE).astype(BF16)

    @pl.when(i < n_used)
    def _():
        x = xb16[...]
        act = _silu(_dot(x, wg_ref[...])) * _dot(x, wu_ref[...])
        part = _dot(act.astype(BF16), wd_ref[...])

        @pl.when(f == 0)
        def _():
            acc[...] = part

        @pl.when((f > 0) & (f < nf - 1))
        def _():
            acc[...] += part

        @pl.when(f == nf - 1)
        def _():
            _to_token_tiles(ybuf.at[slot], acc[...] + part)

    @pl.when(f == nf - 1)
    def _():
        @pl.when((i >= 1) & (i - 1 < n_used))
        def _():
            scatter_wait(1 - slot)

        @pl.when(i < n_used)
        def _():
            scatter_rows(slot)

        @pl.when((i == nt - 1) & (i < n_used))
        def _():
            scatter_wait(slot)


def _experts(h, slot_tok, slot_dst, tile_expert, n_used, wu_bf16, wd_bf16, n_rows_out):
    d = D_MODEL
    n_tiles = slot_tok.shape[0] // MOE_TILE
    nf = EXPERT_DIM // MOE_CHUNK
    assert nf >= 2
    fidx = lambda i, f, nu: jnp.where(i < nu[0], f, nf - 1)
    ids = lambda fn: pl.BlockSpec((None, 1, MOE_TILE), lambda i, f, te, nu: (fn(i), 0, 0), memory_space=pltpu.SMEM)
    grid_spec = pltpu.PrefetchScalarGridSpec(
        num_scalar_prefetch=2,
        grid=(n_tiles, nf),
        in_specs=[ids(lambda i: i), ids(lambda i: jnp.minimum(i + 1, n_tiles - 1)), ids(lambda i: i),
                  pl.BlockSpec(memory_space=pl.ANY),
                  pl.BlockSpec((None, d, MOE_CHUNK), lambda i, f, te, nu: (te[i], 0, fidx(i, f, nu))),
                  pl.BlockSpec((None, d, MOE_CHUNK), lambda i, f, te, nu: (te[i], 0, nf + fidx(i, f, nu))),
                  pl.BlockSpec((None, MOE_CHUNK, d), lambda i, f, te, nu: (te[i], fidx(i, f, nu), 0))],
        out_specs=pl.BlockSpec(memory_space=pl.ANY),
        scratch_shapes=[pltpu.VMEM((2, TILE_ROWS, LANES), F32), pltpu.VMEM((MOE_TILE, d), BF16),
                        pltpu.VMEM((MOE_TILE, d), F32), pltpu.VMEM((2, TILE_ROWS, LANES), F32),
                        pltpu.SemaphoreType.DMA((2,)), pltpu.SemaphoreType.DMA((2,))],
    )
    tok3 = (slot_tok * ROW_TILE).reshape(n_tiles, 1, MOE_TILE)
    dst3 = (slot_dst * ROW_TILE).reshape(n_tiles, 1, MOE_TILE)
    return pl.pallas_call(
        _expert_kernel,
        grid_spec=grid_spec,
        out_shape=jax.ShapeDtypeStruct((n_rows_out * ROW_TILE, LANES), F32),
        compiler_params=_cparams(2),
        name="moe_experts",
    )(tile_expert, n_used, tok3, tok3, dst3, h, wu_bf16, wu_bf16, wd_bf16)


def _combine_kernel(x_ref, y1_ref, y2_ref, p_ref, g_ref, lw_ref, lb_ref, o_ref):
    p = p_ref[...]
    n = x_ref.shape[0]
    f = p[:, 0:1] * _from_token_tiles(y1_ref, n) + p[:, 1:2] * _from_token_tiles(y2_ref, n)
    r = DEEPNORM_ALPHA * x_ref[...] + (1.0 + g_ref[...]) * f
    o_ref[...] = _layer_norm(r, lw_ref[...], lb_ref[...])


def _combine(x, y, probs, gate, ln_w, ln_b):
    s, d = x.shape
    tm = 512
    row = lambda i: (i, 0)
    vec = pl.BlockSpec((1, d), lambda i: (0, 0))
    big = pl.BlockSpec((tm, d), row)
    return pl.pallas_call(
        _combine_kernel,
        grid=(s // tm,),
        in_specs=[big, pl.BlockSpec((tm * ROW_TILE, LANES), row),
                  pl.BlockSpec((tm * ROW_TILE, LANES), lambda i: (s // tm + i, 0)),
                  pl.BlockSpec((tm, LANES), row), vec, vec, vec],
        out_specs=big,
        out_shape=jax.ShapeDtypeStruct((s, d), F32),
        compiler_params=_cparams(1),
        name="moe_combine_ln",
    )(x, y, y, probs, gate, ln_w, ln_b)


def _moe(x, scale, shift, w_router, wu_bf16, wd_bf16, gate, ln_w, ln_b):
    t, d = x.shape
    h, sel, probs, cnt = _router(x, scale, shift, w_router)
    flat_e = sel[:, :TOP_K].reshape(-1)
    rank = sel[:, TOP_K:2 * TOP_K].reshape(-1)
    counts = cnt[0, :N_EXPERTS].astype(jnp.int32)
    n_assign = t * TOP_K
    n_tiles = -(-n_assign // MOE_TILE) + N_EXPERTS
    n_slots = n_tiles * MOE_TILE
    padded = (counts + MOE_TILE - 1) // MOE_TILE * MOE_TILE
    padded_end = jnp.cumsum(padded)
    slot = (padded_end - padded)[flat_e] + rank
    assign = jnp.full((n_slots,), -1, jnp.int32).at[slot].set(jnp.arange(n_assign, dtype=jnp.int32))
    used = assign >= 0
    slot_tok = jnp.where(used, assign >> 1, 0)
    slot_dst = jnp.where(used, (assign & 1) * t + (assign >> 1),
                         n_assign + (jnp.arange(n_slots, dtype=jnp.int32) & (MOE_TILE - 1)))
    n_used = (padded_end[-1] // MOE_TILE).astype(jnp.int32).reshape(1)
    tile_ids = jnp.minimum(jnp.arange(n_tiles, dtype=jnp.int32), n_used[0] - 1)
    tile_expert = jnp.minimum(jnp.sum((tile_ids * MOE_TILE)[:, None] >= padded_end[None, :], axis=1),
                              N_EXPERTS - 1).astype(jnp.int32)
    y = _experts(h, slot_tok, slot_dst, tile_expert, n_used, wu_bf16, wd_bf16, n_assign + MOE_TILE)
    return _combine(x, y, probs, gate, ln_w, ln_b)


def kernel(x, c, positions, w_ada, b_ada, w_in, w_out, a_lower_bound, a_norm_w, b_conv_w, b_conv_b, b_norm_w,
           b_norm_b, d_sink, ln_w, ln_b, ffn_w_up, ffn_w_down, moe_router, moe_w_up, moe_w_down):
    batch, s, d = x.shape
    assert batch == 1 and d == D_MODEL
    x = x.reshape(s, d)
    mod = _adaln(c, w_ada, b_ada)
    rope = _rope_tables(positions)
    lb_cum = jnp.cumsum(jax.nn.softmax(a_lower_bound.astype(F32), axis=0), axis=0)
    lb_all = lb_cum - lb_cum[0]
    row = lambda v: v.reshape(1, -1)
    for layer in range(DEPTH):
        shift1, scale1, gate1, shift2, scale2, gate2 = [mod[layer, :, k * d:(k + 1) * d] for k in range(6)]
        proj, attn = _inproj(x, scale1, shift1, w_in[layer].astype(BF16), rope)
        ya = _hgrn2(proj, row(lb_all[layer]), row(jnp.tile(a_norm_w[layer], GROUP_W // HEAD_DIM)))
        yb = _conv_module(proj, b_conv_w[layer], row(b_conv_b[layer]), row(b_norm_w[layer]), row(b_norm_b[layer]))
        yc = _dilated_attention(attn)
        yd = _window_attention(attn, d_sink[layer])
        x = _outproj((ya, yb, yc, yd), w_out[layer].astype(BF16), x, gate1, row(ln_w[layer, 0]), row(ln_b[layer, 0]))
        lw, lbias = row(ln_w[layer, 1]), row(ln_b[layer, 1])
        if layer % 2 == 0:
            x = _dense_ffn(x, scale2, shift2, ffn_w_up[layer // 2].astype(BF16),
                           ffn_w_down[layer // 2].astype(BF16), gate2, lw, lbias)
        else:
            x = _moe(x, scale2, shift2, moe_router[layer // 2], moe_w_up[layer // 2].astype(BF16),
                     moe_w_down[layer // 2].astype(BF16), gate2, lw, lbias)
    return x.reshape(batch, s, d)
```

```python
import functools

import numpy as np
import jax
import jax.numpy as jnp
from jax import lax
from jax.experimental import pallas as pl
from jax.experimental.pallas import tpu as pltpu

F32 = jnp.float32
BF16 = jnp.bfloat16
MIX_DTYPE = BF16

D_MODEL = 1024
DEPTH = 2
HEAD_DIM = 64
A_CHUNK = 16
B_KERNEL = 31
GROUP_W = 256
C_PATTERNS = ((128, 1), (512, 4), (2048, 16))
C_SIDE = 64
C_HALO = 1024
D_HALF_WINDOW = 128
ROPE_THETA = 500000.0
ROPE_DIM = HEAD_DIM // 4
FFN_DIM = 2816
N_EXPERTS = 8
TOP_K = 2
EXPERT_DIM = 3584
IN_WIDTH = 3072
DEEPNORM_ALPHA = (2 * DEPTH) ** 0.25
LN_EPS = 1e-5
RMS_EPS = 1e-6
NEG_INF = -1e30

LANES = 128
SUBLANES = 8
VMEM_LIMIT = 56 * 1024 * 1024


def _cparams(n_axes=1, vmem=VMEM_LIMIT):
    return pltpu.CompilerParams(dimension_semantics=("arbitrary",) * n_axes, vmem_limit_bytes=vmem)


def _sigmoid(x):
    return 1.0 / (1.0 + jnp.exp(-x))


def _silu(x):
    return x * _sigmoid(x)


def _split_bf16(x):
    hi = x.astype(BF16)
    lo = (x - hi.astype(F32)).astype(BF16)
    return hi, lo


def _layer_norm(r, w, b):
    mu = jnp.mean(r, axis=-1, keepdims=True)
    d = r - mu
    var = jnp.mean(d * d, axis=-1, keepdims=True)
    return d * lax.rsqrt(var + LN_EPS) * w + b


def _dot_nt(a, b):
    return lax.dot_general(a, b, (((1,), (1,)), ((), ())), preferred_element_type=F32)


def _dot_tn(a, b):
    return lax.dot_general(a, b, (((0,), (0,)), ((), ())), preferred_element_type=F32)


def _dot(a, b):
    return jnp.dot(a, b, preferred_element_type=F32)


def _adaln_kernel(c_ref, w_ref, b_ref, o_ref):
    a_hi, a_lo = _split_bf16(_silu(c_ref[...]))
    w_hi, w_lo = _split_bf16(w_ref[...])
    o_ref[...] = _dot(a_hi, w_hi) + _dot(a_hi, w_lo) + _dot(a_lo, w_hi) + b_ref[...]


def _adaln(c, w_ada, b_ada):
    depth, d, n = w_ada.shape
    tn = 1536
    c8 = jnp.broadcast_to(c.reshape(1, d), (8, d))
    out = pl.pallas_call(
        _adaln_kernel,
        grid=(depth, n // tn),
        in_specs=[pl.BlockSpec((8, d), lambda l, j: (0, 0)),
                  pl.BlockSpec((None, d, tn), lambda l, j: (l, 0, j)),
                  pl.BlockSpec((None, 1, tn), lambda l, j: (l, 0, j))],
        out_specs=pl.BlockSpec((None, 8, tn), lambda l, j: (l, 0, j)),
        out_shape=jax.ShapeDtypeStruct((depth, 8, n), F32),
        compiler_params=_cparams(2),
        name="adaln",
    )(c8, w_ada, b_ada.reshape(depth, 1, n))
    return out[:, 0:1, :]


def _rope_kernel(pos_ref, invf_ref, c_ref, s1_ref, s2_ref):
    ang = pos_ref[...].astype(F32) * invf_ref[...]
    j = lax.broadcasted_iota(jnp.int32, ang.shape, 1) & (HEAD_DIM - 1)
    cosv, sinv = jnp.cos(ang), jnp.sin(ang)
    half = ROPE_DIM // 2
    c_ref[...] = jnp.where(j < ROPE_DIM, cosv, 1.0)
    s1_ref[...] = jnp.where(j < half, -sinv, 0.0)
    s2_ref[...] = jnp.where((j >= half) & (j < ROPE_DIM), sinv, 0.0)


def _rope_tables(positions):
    s = positions.shape[-1]
    ts = 2048
    half = ROPE_DIM // 2
    inv_freq = np.float32(ROPE_THETA) ** (-np.arange(half, dtype=np.float32) * np.float32(2.0 / ROPE_DIM))
    invf = jnp.asarray(inv_freq[(np.arange(LANES) % HEAD_DIM) % half].reshape(1, LANES).astype(np.float32))
    spec = pl.BlockSpec((ts, LANES), lambda i: (i, 0))
    return pl.pallas_call(
        _rope_kernel,
        grid=(s // ts,),
        in_specs=[pl.BlockSpec((ts, 1), lambda i: (i, 0)), pl.BlockSpec((1, LANES), lambda i: (0, 0))],
        out_specs=[spec, spec, spec],
        out_shape=[jax.ShapeDtypeStruct((s, LANES), F32)] * 3,
        compiler_params=_cparams(1),
        name="rope_tables",
    )(positions.reshape(s, 1), invf)


_ROPE_COL_CHUNKS = (1792, 1920, 2048, 2176, 2560, 2688, 2816)


ATTN_COL0 = 1792
ATTN_WIDTH = IN_WIDTH - ATTN_COL0


def _inproj_kernel(x_ref, sc_ref, sh_ref, w_ref, rc_ref, rs1_ref, rs2_ref, o_ref, oa_ref):
    h = (x_ref[...] * (1.0 + sc_ref[...]) + sh_ref[...]).astype(BF16)
    half = ROPE_DIM // 2
    for c0 in range(0, IN_WIDTH, GROUP_W):
        acc = _dot(h, w_ref[:, c0:c0 + GROUP_W])
        for k in range(GROUP_W // LANES):
            col = c0 + k * LANES
            part = acc[:, k * LANES:(k + 1) * LANES]
            if col in _ROPE_COL_CHUNKS:
                part = (part * rc_ref[...] + pltpu.roll(part, LANES - half, 1) * rs1_ref[...]
                        + pltpu.roll(part, half, 1) * rs2_ref[...])
            if col < ATTN_COL0:
                o_ref[:, col:col + LANES] = part
            else:
                oa_ref[:, col - ATTN_COL0:col - ATTN_COL0 + LANES] = part.astype(BF16)


def _inproj(x, scale, shift, w_bf16, rope):
    s, d = x.shape
    tm = 512
    row = lambda i: (i, 0)
    const = lambda i: (0, 0)
    return pl.pallas_call(
        _inproj_kernel,
        grid=(s // tm,),
        in_specs=[pl.BlockSpec((tm, d), row), pl.BlockSpec((1, d), const), pl.BlockSpec((1, d), const),
                  pl.BlockSpec((d, IN_WIDTH), const),
                  pl.BlockSpec((tm, LANES), row), pl.BlockSpec((tm, LANES), row), pl.BlockSpec((tm, LANES), row)],
        out_specs=[pl.BlockSpec((tm, ATTN_COL0), row), pl.BlockSpec((tm, ATTN_WIDTH), row)],
        out_shape=[jax.ShapeDtypeStruct((s, ATTN_COL0), F32), jax.ShapeDtypeStruct((s, ATTN_WIDTH), BF16)],
        compiler_params=_cparams(1),
        name="inproj",
    )(x, scale, shift, w_bf16, *rope)


HG_TILE = 512
HG_CHUNKS = HG_TILE // A_CHUNK
N_HEADS = GROUP_W // HEAD_DIM


def _same_head_ones():
    li = lax.broadcasted_iota(jnp.int32, (GROUP_W, GROUP_W), 0)
    lj = lax.broadcasted_iota(jnp.int32, (GROUP_W, GROUP_W), 1)
    return jnp.where((li >> 6) == (lj >> 6), 1.0, 0.0).astype(BF16)


def _hgrn_core(q_ref, z_ref, v_ref, lb_ref, st_ref, qs, zs, vs, qt, kt, ot, oi_ref, rev):
    nb = HG_CHUNKS
    lb = lb_ref[...]
    for src, dst in ((q_ref, qs), (z_ref, zs), (v_ref, vs)):
        val = src[...]
        dst[0] = val[:, :LANES]
        dst[1] = val[:, LANES:]

    def by_pos(buf, j):
        sel = pl.ds(j, nb, stride=A_CHUNK)
        return jnp.concatenate([buf[0, sel, :], buf[1, sel, :]], axis=1)

    order = list(range(A_CHUNK))[::-1] if rev else list(range(A_CHUNK))
    q, f, k, v = [], [], [], []
    for j in order:
        aq, z = by_pos(qs, j), by_pos(zs, j)
        q.append(_silu(aq))
        f.append(lb + (1.0 - lb) * _sigmoid(z))
        k.append((1.0 - lb) * _sigmoid(-z))
        v.append(by_pos(vs, j))
    incl = [f[0]]
    for p in range(1, A_CHUNK):
        incl.append(incl[-1] * f[p])
    rest = [None] * A_CHUNK
    for p in range(A_CHUNK - 2, -1, -1):
        rest[p] = f[p + 1] if rest[p + 1] is None else rest[p + 1] * f[p + 1]
    dec = incl[-1]

    head = lax.broadcasted_iota(jnp.int32, (1, GROUP_W), 1) >> 6
    lo = lax.broadcasted_iota(jnp.int32, (1, LANES), 1) < HEAD_DIM
    ones_bd = _same_head_ones()

    def put(dst, sel, val):
        dst[0, sel, :] = val[:, :LANES]
        dst[1, sel, :] = val[:, LANES:]

    for p, j in enumerate(order):
        xs, decay = [], None
        for d in range(p + 1):
            x = q[p] * k[p - d]
            xs.append((x if decay is None else x * decay).astype(BF16))
            decay = f[p - d] if decay is None else decay * f[p - d]
        sc = _dot(jnp.concatenate(xs, axis=0), ones_bd)
        o = sc[0:nb] * v[p]
        for d in range(1, p + 1):
            o = o + sc[d * nb:(d + 1) * nb] * v[p - d]
        sel = pl.ds(j, nb, stride=A_CHUNK)
        put(ot, sel, o)
        put(qt, sel, q[p] * incl[p])
        put(kt, sel, k[p] if rest[p] is None else k[p] * rest[p])

    def per_head(x):
        return jnp.concatenate([jnp.where(head == h, x, 0.0) for h in range(N_HEADS)], axis=0).astype(BF16)

    chunks = range(nb - 1, -1, -1) if rev else range(nb)
    state = st_ref[...]
    for n in chunks:
        r1 = slice(n * A_CHUNK, (n + 1) * A_CHUNK)
        lq = per_head(jnp.concatenate([qt[0, r1, :], qt[1, r1, :]], axis=1))
        rk = per_head(jnp.concatenate([kt[0, r1, :], kt[1, r1, :]], axis=1))
        va, vb = v_ref[r1, :LANES], v_ref[r1, LANES:]
        lv = jnp.concatenate([va, pltpu.roll(va, HEAD_DIM, 1), vb, pltpu.roll(vb, HEAD_DIM, 1)], axis=0).astype(BF16)
        sb = state.astype(BF16)
        res = _dot_nt(lq, jnp.concatenate([sb, sb], axis=0))
        oi_ref[r1, :LANES] = jnp.where(lo, res[0:A_CHUNK], res[A_CHUNK:2 * A_CHUNK])
        oi_ref[r1, LANES:] = jnp.where(lo, res[2 * A_CHUNK:3 * A_CHUNK], res[3 * A_CHUNK:])
        kv = _dot_tn(lv, rk)[:HEAD_DIM]
        state = state * dec[n:n + 1] + kv
    st_ref[...] = state
    full = slice(0, HG_TILE)
    return jnp.concatenate([ot[0, full, :], ot[1, full, :]], axis=1) + oi_ref[...]


def _hgrn_fwd_kernel(q_ref, f_ref, i_ref, lb_ref, o_ref, st_ref, *scratch):
    @pl.when(pl.program_id(0) == 0)
    def _():
        st_ref[...] = jnp.zeros_like(st_ref)

    o_ref[...] = _hgrn_core(q_ref, f_ref, i_ref, lb_ref, st_ref, *scratch, rev=False)


def _hgrn_bwd_kernel(q_ref, f_ref, i_ref, g_ref, of_ref, lb_ref, nw_ref, o_ref, st_ref, *scratch):
    @pl.when(pl.program_id(0) == 0)
    def _():
        st_ref[...] = jnp.zeros_like(st_ref)

    o = of_ref[...] + _hgrn_core(q_ref, f_ref, i_ref, lb_ref, st_ref, *scratch, rev=True)
    ones_bd = _same_head_ones()
    sq_hi, sq_lo = _split_bf16(o * o)
    ms = (_dot(sq_hi, ones_bd) + _dot(sq_lo, ones_bd)) * (1.0 / HEAD_DIM)
    o_ref[...] = (o * lax.rsqrt(ms + RMS_EPS) * nw_ref[...] * _silu(g_ref[...])).astype(o_ref.dtype)


def _hgrn_scratch():
    split = lambda rows: pltpu.VMEM((2, rows, LANES), F32)
    return [pltpu.VMEM((HEAD_DIM, GROUP_W), F32),
            split(HG_TILE), split(HG_TILE), split(HG_TILE),
            split(HG_TILE), split(HG_TILE), split(HG_TILE), pltpu.VMEM((HG_TILE, GROUP_W), F32)]


def _hgrn2(proj, lb, norm_w):
    s = proj.shape[0]
    ts = HG_TILE
    n = s // ts
    col = lambda j: pl.BlockSpec((ts, GROUP_W), lambda i: (i, j))
    rcol = lambda j: pl.BlockSpec((ts, GROUP_W), lambda i: (n - 1 - i, j))
    vec = pl.BlockSpec((1, GROUP_W), lambda i: (0, 0))
    o_f = pl.pallas_call(
        _hgrn_fwd_kernel,
        grid=(n,),
        in_specs=[col(0), col(1), col(3), vec],
        out_specs=pl.BlockSpec((ts, GROUP_W), lambda i: (i, 0)),
        out_shape=jax.ShapeDtypeStruct((s, GROUP_W), F32),
        scratch_shapes=_hgrn_scratch(),
        compiler_params=_cparams(1),
        name="hgrn_fwd",
    )(proj, proj, proj, lb)
    return pl.pallas_call(
        _hgrn_bwd_kernel,
        grid=(n,),
        in_specs=[rcol(0), rcol(2), rcol(3), rcol(4), pl.BlockSpec((ts, GROUP_W), lambda i: (n - 1 - i, 0)),
                  vec, vec],
        out_specs=pl.BlockSpec((ts, GROUP_W), lambda i: (n - 1 - i, 0)),
        out_shape=jax.ShapeDtypeStruct((s, GROUP_W), MIX_DTYPE),
        scratch_shapes=_hgrn_scratch(),
        compiler_params=_cparams(1),
        name="hgrn_bwd",
    )(proj, proj, proj, proj, o_f, lb, norm_w)


B_PAD = 16


def _conv_kernel(v_ref, g_ref, vp_ref, gp_ref, vn_ref, gn_ref, w_ref, b_ref, nw_ref, nb_ref, o_ref, ubuf):
    i = pl.program_id(0)
    ts = v_ref.shape[0]
    glu = lambda a, b: a * _sigmoid(b)
    ubuf[0:B_PAD, :] = jnp.where(i > 0, glu(vp_ref[...], gp_ref[...]), 0.0)
    ubuf[B_PAD:B_PAD + ts, :] = glu(v_ref[...], g_ref[...])
    ubuf[B_PAD + ts:, :] = jnp.where(i < pl.num_programs(0) - 1, glu(vn_ref[...], gn_ref[...]), 0.0)
    acc = jnp.zeros((ts, GROUP_W), F32) + b_ref[...]
    span = ts + 2 * B_PAD - SUBLANES
    shifted = [ubuf[pl.ds(r, span), :] for r in range(SUBLANES)]
    for j in range(B_KERNEL):
        off = B_PAD - B_KERNEL // 2 + j
        base = off - off % SUBLANES
        acc = acc + shifted[off % SUBLANES][base:base + ts] * w_ref[j:j + 1, :]
    o_ref[...] = _silu(_layer_norm(acc, nw_ref[...], nb_ref[...])).astype(o_ref.dtype)


def _conv_module(proj, conv_w, conv_b, norm_w, norm_b):
    s = proj.shape[0]
    ts = 512
    n = s // ts
    per = ts // B_PAD
    last = s // B_PAD - 1
    cur = lambda j: pl.BlockSpec((ts, GROUP_W), lambda i: (i, j))
    prev = lambda j: pl.BlockSpec((B_PAD, GROUP_W), lambda i: (jnp.maximum(i * per - 1, 0), j))
    nxt = lambda j: pl.BlockSpec((B_PAD, GROUP_W), lambda i: (jnp.minimum((i + 1) * per, last), j))
    vec = pl.BlockSpec((1, GROUP_W), lambda i: (0, 0))
    wpad = jnp.pad(conv_w, ((0, 1), (0, 0)))
    return pl.pallas_call(
        _conv_kernel,
        grid=(n,),
        in_specs=[cur(5), cur(6), prev(5), prev(6), nxt(5), nxt(6),
                  pl.BlockSpec((B_KERNEL + 1, GROUP_W), lambda i: (0, 0)), vec, vec, vec],
        out_specs=pl.BlockSpec((ts, GROUP_W), lambda i: (i, 0)),
        out_shape=jax.ShapeDtypeStruct((s, GROUP_W), MIX_DTYPE),
        scratch_shapes=[pltpu.VMEM((ts + 2 * B_PAD, GROUP_W), F32)],
        compiler_params=_cparams(1),
        name="conv_module",
    )(proj, proj, proj, proj, proj, proj, wpad, conv_b, norm_w, norm_b)


C_MAX_BLOCK = 128


def _dilated_block_sizes(span):
    return sorted({min(span // dil, C_MAX_BLOCK) for _, dil in C_PATTERNS})


def _dilated_kernel(q_ref, kp_ref, kc_ref, kn_ref, vp_ref, vc_ref, vn_ref, o_ref, qw, kw, vw, ob, mb, lb, *band_refs,
                    seq, span):
    i = pl.program_id(0)
    band = dict(zip(_dilated_block_sizes(span), band_refs))

    @pl.when(i == 0)
    def _():
        for nq, ref in band.items():
            a = lax.broadcasted_iota(jnp.int32, ref.shape, 0) & (nq - 1)
            b = lax.broadcasted_iota(jnp.int32, ref.shape, 1)
            ref[...] = jnp.where(jnp.abs(a - (b - C_SIDE)) <= C_SIDE, 0.0, NEG_INF)

    def put(dst, row0, val):
        n = val.shape[0]
        dst[0, row0:row0 + n, :] = val[:, :LANES]
        dst[1, row0:row0 + n, :] = val[:, LANES:]

    put(qw, 0, q_ref[...].astype(F32) * (HEAD_DIM ** -0.5))
    for dst, refs in ((kw, (kp_ref, kc_ref, kn_ref)), (vw, (vp_ref, vc_ref, vn_ref))):
        put(dst, 0, refs[0][...].astype(F32))
        put(dst, C_HALO, refs[1][...].astype(F32))
        put(dst, C_HALO + span, refs[2][...].astype(F32))

    head = lax.broadcasted_iota(jnp.int32, (1, GROUP_W), 1) >> 6
    heads = [head == h for h in range(4)]

    def rows(start, size, dil):
        return pl.ds(start, size) if dil == 1 else pl.ds(start, size, stride=dil)

    def load(src, sel):
        return jnp.concatenate([src[0, sel, :], src[1, sel, :]], axis=1)

    def store(dst, sel, val):
        dst[0, sel, :] = val[:, :LANES]
        dst[1, sel, :] = val[:, LANES:]

    def by_head(x, nq):
        out = jnp.where(heads[0], x[0:nq], 0.0)
        for h in range(1, 4):
            out = out + jnp.where(heads[h], x[h * nq:(h + 1) * nq], 0.0)
        return out

    def cols_by_head(col, nq):
        out = jnp.broadcast_to(col[3 * nq:4 * nq], (nq, GROUP_W))
        for h in range(2, -1, -1):
            out = jnp.where(heads[h], col[h * nq:(h + 1) * nq], out)
        return out

    def attend(first, r, dil, blk, nq):
        nk = nq + 2 * C_SIDE
        qrows = rows(r + dil * blk * nq, nq, dil)
        koff = C_HALO + r + dil * (blk * nq - C_SIDE)
        qs = load(qw, qrows)
        ks = load(kw, rows(koff, nk, dil)).astype(BF16)
        vs = load(vw, rows(koff, nk, dil)).astype(BF16)
        lhs = jnp.concatenate([jnp.where(hm, qs, 0.0) for hm in heads], axis=0).astype(BF16)
        kpos = i * span + (koff - C_HALO) + dil * lax.broadcasted_iota(jnp.int32, (1, nk), 1)
        edge = jnp.where((kpos >= 0) & (kpos < seq), 0.0, NEG_INF)
        sc = _dot_nt(lhs, ks) + band[nq][...] + edge
        m = jnp.max(sc, axis=1, keepdims=True)
        p = jnp.exp(sc - m)
        l = jnp.sum(p, axis=1, keepdims=True)
        o_new = by_head(_dot(p.astype(BF16), vs), nq)
        m_new = cols_by_head(m, nq)
        l_new = cols_by_head(l, nq)
        if first:
            store(ob, qrows, o_new)
            store(mb, qrows, m_new)
            store(lb, qrows, l_new)
        else:
            m_old = load(mb, qrows)
            m_tot = jnp.maximum(m_old, m_new)
            w_old, w_new = jnp.exp(m_old - m_tot), jnp.exp(m_new - m_tot)
            store(ob, qrows, w_old * load(ob, qrows) + w_new * o_new)
            store(lb, qrows, w_old * load(lb, qrows) + w_new * l_new)
            store(mb, qrows, m_tot)

    for pat, (_, dil) in enumerate(C_PATTERNS):
        sub = span // dil
        nq = min(sub, C_MAX_BLOCK)
        for r in range(dil):
            for blk in range(sub // nq):
                attend(pat == 0, r, dil, blk, nq)
    full = slice(0, span)
    o_ref[...] = (load(ob, full) / load(lb, full)).astype(o_ref.dtype)


def _dilated_attention(proj):
    s = proj.shape[0]
    span = 1024
    n = s // span
    per = span // C_HALO
    last = s // C_HALO - 1
    cur = lambda j: pl.BlockSpec((span, GROUP_W), lambda i: (i, j))
    prev = lambda j: pl.BlockSpec((C_HALO, GROUP_W), lambda i: (jnp.maximum(i * per - 1, 0), j))
    nxt = lambda j: pl.BlockSpec((C_HALO, GROUP_W), lambda i: (jnp.minimum((i + 1) * per, last), j))
    win = span + 2 * C_HALO
    return pl.pallas_call(
        functools.partial(_dilated_kernel, seq=s, span=span),
        grid=(n,),
        in_specs=[cur(0), prev(1), cur(1), nxt(1), prev(2), cur(2), nxt(2)],
        out_specs=pl.BlockSpec((span, GROUP_W), lambda i: (i, 0)),
        out_shape=jax.ShapeDtypeStruct((s, GROUP_W), MIX_DTYPE),
        scratch_shapes=[pltpu.VMEM((2, span, LANES), F32),
                        pltpu.VMEM((2, win, LANES), F32), pltpu.VMEM((2, win, LANES), F32),
                        pltpu.VMEM((2, span, LANES), F32), pltpu.VMEM((2, span, LANES), F32),
                        pltpu.VMEM((2, span, LANES), F32)]
        + [pltpu.VMEM((N_HEADS * nq, nq + 2 * C_SIDE), F32) for nq in _dilated_block_sizes(span)],
        compiler_params=_cparams(1),
        name="dilated_attention",
    )(proj, proj, proj, proj, proj, proj, proj)


def _window_kernel(sink_ref, q_ref, kp_ref, kc_ref, kn_ref, vp_ref, vc_ref, vn_ref, o_ref, band_ref, *, seq):
    i = pl.program_id(0)
    tq = q_ref.shape[0]
    hw = D_HALF_WINDOW
    nk = tq + 2 * hw
    kwin = jnp.concatenate([kp_ref[...], kc_ref[...], kn_ref[...]], axis=0).astype(BF16)
    vwin = jnp.concatenate([vp_ref[...], vc_ref[...], vn_ref[...]], axis=0).astype(BF16)
    lo = lax.broadcasted_iota(jnp.int32, (1, LANES), 1) < HEAD_DIM
    q = q_ref[...].astype(F32) * (HEAD_DIM ** -0.5)
    qa, qb = q[:, :LANES], q[:, LANES:]
    swap = lambda x: pltpu.roll(x, HEAD_DIM, 1)
    lhs = (jnp.concatenate([jnp.where(lo, qa, 0.0), jnp.where(lo, swap(qa), 0.0)], axis=0),
           jnp.concatenate([jnp.where(lo, 0.0, swap(qb)), jnp.where(lo, 0.0, qb)], axis=0))
    @pl.when(i == 0)
    def _():
        a = lax.broadcasted_iota(jnp.int32, (2 * tq, nk), 0) & (tq - 1)
        b = lax.broadcasted_iota(jnp.int32, (2 * tq, nk), 1)
        band_ref[...] = jnp.where(jnp.abs(b - hw - a) <= hw, 0.0, NEG_INF)

    kpos = i * tq - hw + lax.broadcasted_iota(jnp.int32, (1, nk), 1)
    bias = band_ref[...] + jnp.where((kpos >= 0) & (kpos < seq), 0.0, NEG_INF)
    second = lax.broadcasted_iota(jnp.int32, (2 * tq, 1), 0) >= tq
    outs = []
    for g in range(2):
        sc = _dot_nt(lhs[g].astype(BF16), kwin) + bias
        m = jnp.max(sc, axis=1, keepdims=True)
        p = jnp.exp(sc - m)
        l = jnp.sum(p, axis=1, keepdims=True)
        o = _dot(p.astype(BF16), vwin)
        sk = jnp.where(second, sink_ref[2 * g + 1], sink_ref[2 * g])
        m_tot = jnp.maximum(m, sk)
        w = jnp.exp(m - m_tot)
        den = l * w + jnp.exp(sk - m_tot)
        outs.append(o * (w / den))
    o_ref[:, :LANES] = jnp.where(lo, outs[0][:tq], swap(outs[0][tq:])).astype(o_ref.dtype)
    o_ref[:, LANES:] = jnp.where(lo, swap(outs[1][:tq]), outs[1][tq:]).astype(o_ref.dtype)


def _window_attention(proj, sink):
    s = proj.shape[0]
    tq = 256
    n = s // tq
    hw = D_HALF_WINDOW
    per = tq // hw
    last = s // hw - 1
    kcol, vcol = (2816 - ATTN_COL0) // LANES, (2944 - ATTN_COL0) // LANES
    cur = lambda j: pl.BlockSpec((tq, LANES), lambda i: (i, j))
    prev = lambda j: pl.BlockSpec((hw, LANES), lambda i: (jnp.maximum(i * per - 1, 0), j))
    nxt = lambda j: pl.BlockSpec((hw, LANES), lambda i: (jnp.minimum((i + 1) * per, last), j))
    return pl.pallas_call(
        functools.partial(_window_kernel, seq=s),
        grid=(n,),
        in_specs=[pl.BlockSpec(memory_space=pltpu.SMEM),
                  pl.BlockSpec((tq, GROUP_W), lambda i: (i, (2560 - ATTN_COL0) // GROUP_W)),
                  prev(kcol), cur(kcol), nxt(kcol), prev(vcol), cur(vcol), nxt(vcol)],
        out_specs=pl.BlockSpec((tq, GROUP_W), lambda i: (i, 0)),
        out_shape=jax.ShapeDtypeStruct((s, GROUP_W), MIX_DTYPE),
        scratch_shapes=[pltpu.VMEM((2 * tq, tq + 2 * hw), F32)],
        compiler_params=_cparams(1),
        name="window_attention",
    )(sink, proj, proj, proj, proj, proj, proj, proj)


def _outproj_kernel(ya_ref, yb_ref, yc_ref, yd_ref, w_ref, x_ref, g_ref, lw_ref, lb_ref, o_ref):
    y = None
    for k, ref in enumerate((ya_ref, yb_ref, yc_ref, yd_ref)):
        part = _dot(ref[...].astype(BF16), w_ref[k * GROUP_W:(k + 1) * GROUP_W, :])
        y = part if y is None else y + part
    r = DEEPNORM_ALPHA * x_ref[...] + (1.0 + g_ref[...]) * y
    o_ref[...] = _layer_norm(r, lw_ref[...], lb_ref[...])


def _outproj(ys, w_bf16, x, gate, ln_w, ln_b):
    s, d = x.shape
    tm = 512
    row = lambda i: (i, 0)
    const = lambda i: (0, 0)
    vec = pl.BlockSpec((1, d), const)
    grp = pl.BlockSpec((tm, GROUP_W), row)
    return pl.pallas_call(
        _outproj_kernel,
        grid=(s // tm,),
        in_specs=[grp, grp, grp, grp, pl.BlockSpec((4 * GROUP_W, d), const), pl.BlockSpec((tm, d), row),
                  vec, vec, vec],
        out_specs=pl.BlockSpec((tm, d), row),
        out_shape=jax.ShapeDtypeStruct((s, d), F32),
        compiler_params=_cparams(1),
        name="outproj_ln",
    )(*ys, w_bf16, x, gate, ln_w, ln_b)


FFN_CHUNK = 256


def _ffn_kernel(x_ref, sc_ref, sh_ref, wu_ref, wd_ref, g_ref, lw_ref, lb_ref, o_ref):
    x = x_ref[...]
    h = (x * (1.0 + sc_ref[...]) + sh_ref[...]).astype(BF16)
    acc = jnp.zeros(x.shape, F32)
    for c0 in range(0, FFN_DIM, FFN_CHUNK):
        gate = _dot(h, wu_ref[:, c0:c0 + FFN_CHUNK])
        up = _dot(h, wu_ref[:, FFN_DIM + c0:FFN_DIM + c0 + FFN_CHUNK])
        acc = acc + _dot((_silu(gate) * up).astype(BF16), wd_ref[c0:c0 + FFN_CHUNK, :])
    r = DEEPNORM_ALPHA * x + (1.0 + g_ref[...]) * acc
    o_ref[...] = _layer_norm(r, lw_ref[...], lb_ref[...])


def _dense_ffn(x, scale, shift, wu_bf16, wd_bf16, gate, ln_w, ln_b):
    s, d = x.shape
    tm = 512
    row = lambda i: (i, 0)
    const = lambda i: (0, 0)
    vec = pl.BlockSpec((1, d), const)
    return pl.pallas_call(
        _ffn_kernel,
        grid=(s // tm,),
        in_specs=[pl.BlockSpec((tm, d), row), vec, vec,
                  pl.BlockSpec((d, 2 * FFN_DIM), const), pl.BlockSpec((FFN_DIM, d), const), vec, vec, vec],
        out_specs=pl.BlockSpec((tm, d), row),
        out_shape=jax.ShapeDtypeStruct((s, d), F32),
        compiler_params=_cparams(1),
        name="dense_ffn_ln",
    )(x, scale, shift, wu_bf16, wd_bf16, gate, ln_w, ln_b)


ROW_TILE = 8


def _to_token_tiles(ref, val):
    n = val.shape[0]
    for c in range(ROW_TILE):
        ref[pl.ds(c, n, stride=ROW_TILE), :] = val[:, c * LANES:(c + 1) * LANES]


def _from_token_tiles(ref, n):
    return jnp.concatenate([ref[pl.ds(c, n, stride=ROW_TILE), :] for c in range(ROW_TILE)], axis=1)


def _router_kernel(x_ref, sc_ref, sh_ref, w_ref, h_ref, e_ref, p_ref, c_ref, cnt_ref):
    h = x_ref[...] * (1.0 + sc_ref[...]) + sh_ref[...]
    _to_token_tiles(h_ref, h)
    h_hi, h_lo = _split_bf16(h)
    w_hi, w_lo = _split_bf16(w_ref[...])
    logits = _dot(h_hi, w_hi) + _dot(h_hi, w_lo) + _dot(h_lo, w_hi)
    lane = lax.broadcasted_iota(jnp.int32, logits.shape, 1)
    logits = jnp.where(lane < N_EXPERTS, logits, NEG_INF)
    m1 = jnp.max(logits, axis=1, keepdims=True)
    e1 = jnp.min(jnp.where(logits == m1, lane, LANES), axis=1, keepdims=True)
    rest = jnp.where(lane == e1, NEG_INF, logits)
    m2 = jnp.max(rest, axis=1, keepdims=True)
    e2 = jnp.min(jnp.where(rest == m2, lane, LANES), axis=1, keepdims=True)
    t = jnp.exp(m2 - m1)
    g1 = 1.0 / (1.0 + t)
    p_ref[...] = jnp.where(lane == 0, g1, jnp.where(lane == 1, t * g1, 0.0))

    @pl.when(pl.program_id(0) == 0)
    def _():
        cnt_ref[...] = jnp.zeros_like(cnt_ref)

    tm = logits.shape[0]
    picks = jnp.where((lane == e1) | (lane == e2), 1.0, 0.0)
    rr = lax.broadcasted_iota(jnp.int32, (tm, tm), 0)
    cc = lax.broadcasted_iota(jnp.int32, (tm, tm), 1)
    before = _dot(jnp.where(cc < rr, 1.0, 0.0).astype(BF16), picks.astype(BF16)) + cnt_ref[...]
    r1 = jnp.sum(jnp.where(lane == e1, before, 0.0), axis=1, keepdims=True).astype(jnp.int32)
    r2 = jnp.sum(jnp.where(lane == e2, before, 0.0), axis=1, keepdims=True).astype(jnp.int32)
    cnt_ref[...] += jnp.sum(picks, axis=0, keepdims=True)
    c_ref[...] = jnp.broadcast_to(cnt_ref[...], c_ref.shape)
    e_ref[...] = jnp.where(lane == 0, e1, jnp.where(lane == 1, e2, jnp.where(lane == 2, r1, jnp.where(lane == 3, r2, 0))))


def _router(x, scale, shift, w_router):
    s, d = x.shape
    tm = 512
    row = lambda i: (i, 0)
    const = lambda i: (0, 0)
    vec = pl.BlockSpec((1, d), const)
    wpad = jnp.pad(w_router, ((0, 0), (0, LANES - N_EXPERTS)))
    return pl.pallas_call(
        _router_kernel,
        grid=(s // tm,),
        in_specs=[pl.BlockSpec((tm, d), row), vec, vec, pl.BlockSpec((d, LANES), const)],
        out_specs=[pl.BlockSpec((tm * ROW_TILE, LANES), row), pl.BlockSpec((tm, LANES), row),
                   pl.BlockSpec((tm, LANES), row), pl.BlockSpec((8, LANES), const)],
        out_shape=[jax.ShapeDtypeStruct((s * ROW_TILE, LANES), F32), jax.ShapeDtypeStruct((s, LANES), jnp.int32),
                   jax.ShapeDtypeStruct((s, LANES), F32), jax.ShapeDtypeStruct((8, LANES), F32)],
        scratch_shapes=[pltpu.VMEM((1, LANES), F32)],
        compiler_params=_cparams(1),
        name="moe_router",
    )(x, scale, shift, wpad)


MOE_TILE = 512
MOE_CHUNK = 1792
ROW_UNROLL = 16
TILE_ROWS = MOE_TILE * ROW_TILE


def _expert_kernel(te_ref, nu_ref, tokc_ref, tokn_ref, dst_ref, h_hbm, wg_ref, wu_ref, wd_ref, y_hbm,
                   xbuf, xb16, acc, ybuf, gsem, ssem):
    i, f = pl.program_id(0), pl.program_id(1)
    nt, nf = pl.num_programs(0), pl.num_programs(1)
    n_used = nu_ref[0]
    slot = i & 1
    tail = y_hbm.shape[0] - TILE_ROWS

    def rows_loop(fn):
        def body(g, carry):
            base = pl.multiple_of(g * ROW_UNROLL, ROW_UNROLL)
            for j in range(ROW_UNROLL):
                fn(base + j)
            return carry
        lax.fori_loop(0, MOE_TILE // ROW_UNROLL, body, 0)

    def token(ref, r):
        return ref.at[pl.ds(pl.multiple_of(r, ROW_TILE), ROW_TILE)]

    def gather_rows(ids_ref, s):
        rows_loop(lambda r: pltpu.make_async_copy(token(h_hbm, ids_ref[0, r]), token(xbuf.at[s], r * ROW_TILE),
                                                  gsem.at[s]).start())

    def gather_wait(s):
        pltpu.make_async_copy(h_hbm.at[pl.ds(0, TILE_ROWS)], xbuf.at[s], gsem.at[s]).wait()

    def scatter_rows(s):
        rows_loop(lambda r: pltpu.make_async_copy(token(ybuf.at[s], r * ROW_TILE), token(y_hbm, dst_ref[0, r]),
                                                  ssem.at[s]).start())

    def scatter_wait(s):
        pltpu.make_async_copy(ybuf.at[s], y_hbm.at[pl.ds(0, TILE_ROWS)], ssem.at[s]).wait()

    @pl.when(f == 0)
    def _():
        @pl.when(i == 0)
        def _():
            ybuf[1] = jnp.zeros(ybuf.shape[1:], F32)
            zero_tail = pltpu.make_async_copy(ybuf.at[1], y_hbm.at[pl.ds(tail, TILE_ROWS)], ssem.at[1])
            zero_tail.start()
            zero_tail.wait()
            gather_rows(tokc_ref, 0)

        @pl.when(i + 1 < n_used)
        def _():
            gather_rows(tokn_ref, 1 - slot)

        @pl.when(i < n_used)
        def _():
            gather_wait(slot)
            xb16[...] = _from_token_tiles(xbuf.at[slot], MOE_TILE).astype(BF16)

    @pl.when(i < n_used)
    def _():
        x = xb16[...]
        act = _silu(_dot(x, wg_ref[...])) * _dot(x, wu_ref[...])
        part = _dot(act.astype(BF16), wd_ref[...])

        @pl.when(f == 0)
        def _():
            acc[...] = part

        @pl.when((f > 0) & (f < nf - 1))
        def _():
            acc[...] += part

        @pl.when(f == nf - 1)
        def _():
            _to_token_tiles(ybuf.at[slot], acc[...] + part)

    @pl.when(f == nf - 1)
    def _():
        @pl.when((i >= 1) & (i - 1 < n_used))
        def _():
            scatter_wait(1 - slot)

        @pl.when(i < n_used)
        def _():
            scatter_rows(slot)

        @pl.when((i == nt - 1) & (i < n_used))
        def _():
            scatter_wait(slot)


def _experts(h, slot_tok, slot_dst, tile_expert, n_used, wu_bf16, wd_bf16, n_rows_out):
    d = D_MODEL
    n_tiles = slot_tok.shape[0] // MOE_TILE
    nf = EXPERT_DIM // MOE_CHUNK
    assert nf >= 2
    fidx = lambda i, f, nu: jnp.where(i < nu[0], f, nf - 1)
    ids = lambda fn: pl.BlockSpec((None, 1, MOE_TILE), lambda i, f, te, nu: (fn(i), 0, 0), memory_space=pltpu.SMEM)
    grid_spec = pltpu.PrefetchScalarGridSpec(
        num_scalar_prefetch=2,
        grid=(n_tiles, nf),
        in_specs=[ids(lambda i: i), ids(lambda i: jnp.minimum(i + 1, n_tiles - 1)), ids(lambda i: i),
                  pl.BlockSpec(memory_space=pl.ANY),
                  pl.BlockSpec((None, d, MOE_CHUNK), lambda i, f, te, nu: (te[i], 0, fidx(i, f, nu))),
                  pl.BlockSpec((None, d, MOE_CHUNK), lambda i, f, te, nu: (te[i], 0, nf + fidx(i, f, nu))),
                  pl.BlockSpec((None, MOE_CHUNK, d), lambda i, f, te, nu: (te[i], fidx(i, f, nu), 0))],
        out_specs=pl.BlockSpec(memory_space=pl.ANY),
        scratch_shapes=[pltpu.VMEM((2, TILE_ROWS, LANES), F32), pltpu.VMEM((MOE_TILE, d), BF16),
                        pltpu.VMEM((MOE_TILE, d), F32), pltpu.VMEM((2, TILE_ROWS, LANES), F32),
                        pltpu.SemaphoreType.DMA((2,)), pltpu.SemaphoreType.DMA((2,))],
    )
    tok3 = (slot_tok * ROW_TILE).reshape(n_tiles, 1, MOE_TILE)
    dst3 = (slot_dst * ROW_TILE).reshape(n_tiles, 1, MOE_TILE)
    return pl.pallas_call(
        _expert_kernel,
        grid_spec=grid_spec,
        out_shape=jax.ShapeDtypeStruct((n_rows_out * ROW_TILE, LANES), F32),
        compiler_params=_cparams(2),
        name="moe_experts",
    )(tile_expert, n_used, tok3, tok3, dst3, h, wu_bf16, wu_bf16, wd_bf16)


def _combine_kernel(x_ref, y1_ref, y2_ref, p_ref, g_ref, lw_ref, lb_ref, o_ref):
    p = p_ref[...]
    n = x_ref.shape[0]
    f = p[:, 0:1] * _from_token_tiles(y1_ref, n) + p[:, 1:2] * _from_token_tiles(y2_ref, n)
    r = DEEPNORM_ALPHA * x_ref[...] + (1.0 + g_ref[...]) * f
    o_ref[...] = _layer_norm(r, lw_ref[...], lb_ref[...])


def _combine(x, y, probs, gate, ln_w, ln_b):
    s, d = x.shape
    tm = 512
    row = lambda i: (i, 0)
    vec = pl.BlockSpec((1, d), lambda i: (0, 0))
    big = pl.BlockSpec((tm, d), row)
    return pl.pallas_call(
        _combine_kernel,
        grid=(s // tm,),
        in_specs=[big, pl.BlockSpec((tm * ROW_TILE, LANES), row),
                  pl.BlockSpec((tm * ROW_TILE, LANES), lambda i: (s // tm + i, 0)),
                  pl.BlockSpec((tm, LANES), row), vec, vec, vec],
        out_specs=big,
        out_shape=jax.ShapeDtypeStruct((s, d), F32),
        compiler_params=_cparams(1),
        name="moe_combine_ln",
    )(x, y, y, probs, gate, ln_w, ln_b)


def _moe(x, scale, shift, w_router, wu_bf16, wd_bf16, gate, ln_w, ln_b):
    t, d = x.shape
    h, sel, probs, cnt = _router(x, scale, shift, w_router)
    flat_e = sel[:, :TOP_K].reshape(-1)
    rank = sel[:, TOP_K:2 * TOP_K].reshape(-1)
    counts = cnt[0, :N_EXPERTS].astype(jnp.int32)
    n_assign = t * TOP_K
    n_tiles = -(-n_assign // MOE_TILE) + N_EXPERTS
    n_slots = n_tiles * MOE_TILE
    padded = (counts + MOE_TILE - 1) // MOE_TILE * MOE_TILE
    padded_end = jnp.cumsum(padded)
    slot = (padded_end - padded)[flat_e] + rank
    assign = jnp.full((n_slots,), -1, jnp.int32).at[slot].set(jnp.arange(n_assign, dtype=jnp.int32))
    used = assign >= 0
    slot_tok = jnp.where(used, assign >> 1, 0)
    slot_dst = jnp.where(used, (assign & 1) * t + (assign >> 1),
                         n_assign + (jnp.arange(n_slots, dtype=jnp.int32) & (MOE_TILE - 1)))
    n_used = (padded_end[-1] // MOE_TILE).astype(jnp.int32).reshape(1)
    tile_ids = jnp.minimum(jnp.arange(n_tiles, dtype=jnp.int32), n_used[0] - 1)
    tile_expert = jnp.minimum(jnp.sum((tile_ids * MOE_TILE)[:, None] >= padded_end[None, :], axis=1),
                              N_EXPERTS - 1).astype(jnp.int32)
    y = _experts(h, slot_tok, slot_dst, tile_expert, n_used, wu_bf16, wd_bf16, n_assign + MOE_TILE)
    return _combine(x, y, probs, gate, ln_w, ln_b)


def kernel(x, c, positions, w_ada, b_ada, w_in, w_out, a_lower_bound, a_norm_w, b_conv_w, b_conv_b, b_norm_w,
           b_norm_b, d_sink, ln_w, ln_b, ffn_w_up, ffn_w_down, moe_router, moe_w_up, moe_w_down):
    batch, s, d = x.shape
    assert batch == 1 and d == D_MODEL
    x = x.reshape(s, d)
    mod = _adaln(c, w_ada, b_ada)
    rope = _rope_tables(positions)
    lb_cum = jnp.cumsum(jax.nn.softmax(a_lower_bound.astype(F32), axis=0), axis=0)
    lb_all = lb_cum - lb_cum[0]
    row = lambda v: v.reshape(1, -1)
    for layer in range(DEPTH):
        shift1, scale1, gate1, shift2, scale2, gate2 = [mod[layer, :, k * d:(k + 1) * d] for k in range(6)]
        proj, attn = _inproj(x, scale1, shift1, w_in[layer].astype(BF16), rope)
        ya = _hgrn2(proj, row(lb_all[layer]), row(jnp.tile(a_norm_w[layer], GROUP_W // HEAD_DIM)))
        yb = _conv_module(proj, b_conv_w[layer], row(b_conv_b[layer]), row(b_norm_w[layer]), row(b_norm_b[layer]))
        yc = _dilated_attention(attn)
        yd = _window_attention(attn, d_sink[layer])
        x = _outproj((ya, yb, yc, yd), w_out[layer].astype(BF16), x, gate1, row(ln_w[layer, 0]), row(ln_b[layer, 0]))
        lw, lbias = row(ln_w[layer, 1]), row(ln_b[layer, 1])
        if layer % 2 == 0:
            x = _dense_ffn(x, scale2, shift2, ffn_w_up[layer // 2].astype(BF16),
                           ffn_w_down[layer // 2].astype(BF16), gate2, lw, lbias)
        else:
            x = _moe(x, scale2, shift2, moe_router[layer // 2], moe_w_up[layer // 2].astype(BF16),
                     moe_w_down[layer // 2].astype(BF16), gate2, lw, lbias)
    return x.reshape(batch, s, d)
```

```python
import functools

import numpy as np
import jax
import jax.numpy as jnp
from jax import lax
from jax.experimental import pallas as pl
from jax.experimental.pallas import tpu as pltpu

F32 = jnp.float32
BF16 = jnp.bfloat16
MIX_DTYPE = BF16

D_MODEL = 1024
DEPTH = 2
HEAD_DIM = 64
A_CHUNK = 16
B_KERNEL = 31
GROUP_W = 256
C_PATTERNS = ((128, 1), (512, 4), (2048, 16))
C_SIDE = 64
C_HALO = 1024
D_HALF_WINDOW = 128
ROPE_THETA = 500000.0
ROPE_DIM = HEAD_DIM // 4
FFN_DIM = 2816
N_EXPERTS = 8
TOP_K = 2
EXPERT_DIM = 3584
IN_WIDTH = 3072
DEEPNORM_ALPHA = (2 * DEPTH) ** 0.25
LN_EPS = 1e-5
RMS_EPS = 1e-6
NEG_INF = -1e30

LANES = 128
SUBLANES = 8
VMEM_LIMIT = 56 * 1024 * 1024


def _cparams(n_axes=1, vmem=VMEM_LIMIT):
    return pltpu.CompilerParams(dimension_semantics=("arbitrary",) * n_axes, vmem_limit_bytes=vmem)


def _sigmoid(x):
    return 1.0 / (1.0 + jnp.exp(-x))


def _silu(x):
    return x * _sigmoid(x)


def _split_bf16(x):
    hi = x.astype(BF16)
    lo = (x - hi.astype(F32)).astype(BF16)
    return hi, lo


def _layer_norm(r, w, b):
    mu = jnp.mean(r, axis=-1, keepdims=True)
    d = r - mu
    var = jnp.mean(d * d, axis=-1, keepdims=True)
    return d * lax.rsqrt(var + LN_EPS) * w + b


def _dot_nt(a, b):
    return lax.dot_general(a, b, (((1,), (1,)), ((), ())), preferred_element_type=F32)


def _dot_tn(a, b):
    return lax.dot_general(a, b, (((0,), (0,)), ((), ())), preferred_element_type=F32)


def _dot(a, b):
    return jnp.dot(a, b, preferred_element_type=F32)


def _adaln_kernel(c_ref, w_ref, b_ref, o_ref):
    a_hi, a_lo = _split_bf16(_silu(c_ref[...]))
    w_hi, w_lo = _split_bf16(w_ref[...])
    o_ref[...] = _dot(a_hi, w_hi) + _dot(a_hi, w_lo) + _dot(a_lo, w_hi) + b_ref[...]


def _adaln(c, w_ada, b_ada):
    depth, d, n = w_ada.shape
    tn = 1536
    c8 = jnp.broadcast_to(c.reshape(1, d), (8, d))
    out = pl.pallas_call(
        _adaln_kernel,
        grid=(depth, n // tn),
        in_specs=[pl.BlockSpec((8, d), lambda l, j: (0, 0)),
                  pl.BlockSpec((None, d, tn), lambda l, j: (l, 0, j)),
                  pl.BlockSpec((None, 1, tn), lambda l, j: (l, 0, j))],
        out_specs=pl.BlockSpec((None, 8, tn), lambda l, j: (l, 0, j)),
        out_shape=jax.ShapeDtypeStruct((depth, 8, n), F32),
        compiler_params=_cparams(2),
        name="adaln",
    )(c8, w_ada, b_ada.reshape(depth, 1, n))
    return out[:, 0:1, :]


def _rope_kernel(pos_ref, invf_ref, c_ref, s1_ref, s2_ref):
    ang = pos_ref[...].astype(F32) * invf_ref[...]
    j = lax.broadcasted_iota(jnp.int32, ang.shape, 1) & (HEAD_DIM - 1)
    cosv, sinv = jnp.cos(ang), jnp.sin(ang)
    half = ROPE_DIM // 2
    c_ref[...] = jnp.where(j < ROPE_DIM, cosv, 1.0)
    s1_ref[...] = jnp.where(j < half, -sinv, 0.0)
    s2_ref[...] = jnp.where((j >= half) & (j < ROPE_DIM), sinv, 0.0)


def _rope_tables(positions):
    s = positions.shape[-1]
    ts = 2048
    half = ROPE_DIM // 2
    inv_freq = np.float32(ROPE_THETA) ** (-np.arange(half, dtype=np.float32) * np.float32(2.0 / ROPE_DIM))
    invf = jnp.asarray(inv_freq[(np.arange(LANES) % HEAD_DIM) % half].reshape(1, LANES).astype(np.float32))
    spec = pl.BlockSpec((ts, LANES), lambda i: (i, 0))
    return pl.pallas_call(
        _rope_kernel,
        grid=(s // ts,),
        in_specs=[pl.BlockSpec((ts, 1), lambda i: (i, 0)), pl.BlockSpec((1, LANES), lambda i: (0, 0))],
        out_specs=[spec, spec, spec],
        out_shape=[jax.ShapeDtypeStruct((s, LANES), F32)] * 3,
        compiler_params=_cparams(1),
        name="rope_tables",
    )(positions.reshape(s, 1), invf)


_ROPE_COL_CHUNKS = (1792, 1920, 2048, 2176, 2560, 2688, 2816)


ATTN_COL0 = 1792
ATTN_WIDTH = IN_WIDTH - ATTN_COL0


def _inproj_kernel(x_ref, sc_ref, sh_ref, w_ref, rc_ref, rs1_ref, rs2_ref, o_ref, oa_ref):
    h = (x_ref[...] * (1.0 + sc_ref[...]) + sh_ref[...]).astype(BF16)
    half = ROPE_DIM // 2
    for c0 in range(0, IN_WIDTH, GROUP_W):
        acc = _dot(h, w_ref[:, c0:c0 + GROUP_W])
        for k in range(GROUP_W // LANES):
            col = c0 + k * LANES
            part = acc[:, k * LANES:(k + 1) * LANES]
            if col in _ROPE_COL_CHUNKS:
                part = (part * rc_ref[...] + pltpu.roll(part, LANES - half, 1) * rs1_ref[...]
                        + pltpu.roll(part, half, 1) * rs2_ref[...])
            if col < ATTN_COL0:
                o_ref[:, col:col + LANES] = part
            else:
                oa_ref[:, col - ATTN_COL0:col - ATTN_COL0 + LANES] = part.astype(BF16)


def _inproj(x, scale, shift, w_bf16, rope):
    s, d = x.shape
    tm = 512
    row = lambda i: (i, 0)
    const = lambda i: (0, 0)
    return pl.pallas_call(
        _inproj_kernel,
        grid=(s // tm,),
        in_specs=[pl.BlockSpec((tm, d), row), pl.BlockSpec((1, d), const), pl.BlockSpec((1, d), const),
                  pl.BlockSpec((d, IN_WIDTH), const),
                  pl.BlockSpec((tm, LANES), row), pl.BlockSpec((tm, LANES), row), pl.BlockSpec((tm, LANES), row)],
        out_specs=[pl.BlockSpec((tm, ATTN_COL0), row), pl.BlockSpec((tm, ATTN_WIDTH), row)],
        out_shape=[jax.ShapeDtypeStruct((s, ATTN_COL0), F32), jax.ShapeDtypeStruct((s, ATTN_WIDTH), BF16)],
        compiler_params=_cparams(1),
        name="inproj",
    )(x, scale, shift, w_bf16, *rope)


HG_TILE = 512
HG_CHUNKS = HG_TILE // A_CHUNK
N_HEADS = GROUP_W // HEAD_DIM


def _same_head_ones():
    li = lax.broadcasted_iota(jnp.int32, (GROUP_W, GROUP_W), 0)
    lj = lax.broadcasted_iota(jnp.int32, (GROUP_W, GROUP_W), 1)
    return jnp.where((li >> 6) == (lj >> 6), 1.0, 0.0).astype(BF16)


def _hgrn_core(q_ref, z_ref, v_ref, lb_ref, st_ref, qs, zs, vs, qt, kt, ot, oi_ref, rev):
    nb = HG_CHUNKS
    lb = lb_ref[...]
    for src, dst in ((q_ref, qs), (z_ref, zs), (v_ref, vs)):
        val = src[...]
        dst[0] = val[:, :LANES]
        dst[1] = val[:, LANES:]

    def by_pos(buf, j):
        sel = pl.ds(j, nb, stride=A_CHUNK)
        return jnp.concatenate([buf[0, sel, :], buf[1, sel, :]], axis=1)

    order = list(range(A_CHUNK))[::-1] if rev else list(range(A_CHUNK))
    q, f, k, v = [], [], [], []
    for j in order:
        aq, z = by_pos(qs, j), by_pos(zs, j)
        q.append(_silu(aq))
        f.append(lb + (1.0 - lb) * _sigmoid(z))
        k.append((1.0 - lb) * _sigmoid(-z))
        v.append(by_pos(vs, j))
    incl = [f[0]]
    for p in range(1, A_CHUNK):
        incl.append(incl[-1] * f[p])
    rest = [None] * A_CHUNK
    for p in range(A_CHUNK - 2, -1, -1):
        rest[p] = f[p + 1] if rest[p + 1] is None else rest[p + 1] * f[p + 1]
    dec = incl[-1]

    head = lax.broadcasted_iota(jnp.int32, (1, GROUP_W), 1) >> 6
    lo = lax.broadcasted_iota(jnp.int32, (1, LANES), 1) < HEAD_DIM
    ones_bd = _same_head_ones()

    def put(dst, sel, val):
        dst[0, sel, :] = val[:, :LANES]
        dst[1, sel, :] = val[:, LANES:]

    for p, j in enumerate(order):
        xs, decay = [], None
        for d in range(p + 1):
            x = q[p] * k[p - d]
            xs.append((x if decay is None else x * decay).astype(BF16))
            decay = f[p - d] if decay is None else decay * f[p - d]
        sc = _dot(jnp.concatenate(xs, axis=0), ones_bd)
        o = sc[0:nb] * v[p]
        for d in range(1, p + 1):
            o = o + sc[d * nb:(d + 1) * nb] * v[p - d]
        sel = pl.ds(j, nb, stride=A_CHUNK)
        put(ot, sel, o)
        put(qt, sel, q[p] * incl[p])
        put(kt, sel, k[p] if rest[p] is None else k[p] * rest[p])

    def per_head(x):
        return jnp.concatenate([jnp.where(head == h, x, 0.0) for h in range(N_HEADS)], axis=0).astype(BF16)

    chunks = range(nb - 1, -1, -1) if rev else range(nb)
    state = st_ref[...]
    for n in chunks:
        r1 = slice(n * A_CHUNK, (n + 1) * A_CHUNK)
        lq = per_head(jnp.concatenate([qt[0, r1, :], qt[1, r1, :]], axis=1))
        rk = per_head(jnp.concatenate([kt[0, r1, :], kt[1, r1, :]], axis=1))
        va, vb = v_ref[r1, :LANES], v_ref[r1, LANES:]
        lv = jnp.concatenate([va, pltpu.roll(va, HEAD_DIM, 1), vb, pltpu.roll(vb, HEAD_DIM, 1)], axis=0).astype(BF16)
        sb = state.astype(BF16)
        res = _dot_nt(lq, jnp.concatenate([sb, sb], axis=0))
        oi_ref[r1, :LANES] = jnp.where(lo, res[0:A_CHUNK], res[A_CHUNK:2 * A_CHUNK])
        oi_ref[r1, LANES:] = jnp.where(lo, res[2 * A_CHUNK:3 * A_CHUNK], res[3 * A_CHUNK:])
        kv = _dot_tn(lv, rk)[:HEAD_DIM]
        state = state * dec[n:n + 1] + kv
    st_ref[...] = state
    full = slice(0, HG_TILE)
    return jnp.concatenate([ot[0, full, :], ot[1, full, :]], axis=1) + oi_ref[...]


def _hgrn_fwd_kernel(q_ref, f_ref, i_ref, lb_ref, o_ref, st_ref, *scratch):
    @pl.when(pl.program_id(0) == 0)
    def _():
        st_ref[...] = jnp.zeros_like(st_ref)

    o_ref[...] = _hgrn_core(q_ref, f_ref, i_ref, lb_ref, st_ref, *scratch, rev=False)


def _hgrn_bwd_kernel(q_ref, f_ref, i_ref, g_ref, of_ref, lb_ref, nw_ref, o_ref, st_ref, *scratch):
    @pl.when(pl.program_id(0) == 0)
    def _():
        st_ref[...] = jnp.zeros_like(st_ref)

    o = of_ref[...] + _hgrn_core(q_ref, f_ref, i_ref, lb_ref, st_ref, *scratch, rev=True)
    ones_bd = _same_head_ones()
    sq_hi, sq_lo = _split_bf16(o * o)
    ms = (_dot(sq_hi, ones_bd) + _dot(sq_lo, ones_bd)) * (1.0 / HEAD_DIM)
    o_ref[...] = (o * lax.rsqrt(ms + RMS_EPS) * nw_ref[...] * _silu(g_ref[...])).astype(o_ref.dtype)


def _hgrn_scratch():
    split = lambda rows: pltpu.VMEM((2, rows, LANES), F32)
    return [pltpu.VMEM((HEAD_DIM, GROUP_W), F32),
            split(HG_TILE), split(HG_TILE), split(HG_TILE),
            split(HG_TILE), split(HG_TILE), split(HG_TILE), pltpu.VMEM((HG_TILE, GROUP_W), F32)]


def _hgrn2(proj, lb, norm_w):
    s = proj.shape[0]
    ts = HG_TILE
    n = s // ts
    col = lambda j: pl.BlockSpec((ts, GROUP_W), lambda i: (i, j))
    rcol = lambda j: pl.BlockSpec((ts, GROUP_W), lambda i: (n - 1 - i, j))
    vec = pl.BlockSpec((1, GROUP_W), lambda i: (0, 0))
    o_f = pl.pallas_call(
        _hgrn_fwd_kernel,
        grid=(n,),
        in_specs=[col(0), col(1), col(3), vec],
        out_specs=pl.BlockSpec((ts, GROUP_W), lambda i: (i, 0)),
        out_shape=jax.ShapeDtypeStruct((s, GROUP_W), F32),
        scratch_shapes=_hgrn_scratch(),
        compiler_params=_cparams(1),
        name="hgrn_fwd",
    )(proj, proj, proj, lb)
    return pl.pallas_call(
        _hgrn_bwd_kernel,
        grid=(n,),
        in_specs=[rcol(0), rcol(2), rcol(3), rcol(4), pl.BlockSpec((ts, GROUP_W), lambda i: (n - 1 - i, 0)),
                  vec, vec],
        out_specs=pl.BlockSpec((ts, GROUP_W), lambda i: (n - 1 - i, 0)),
        out_shape=jax.ShapeDtypeStruct((s, GROUP_W), MIX_DTYPE),
        scratch_shapes=_hgrn_scratch(),
        compiler_params=_cparams(1),
        name="hgrn_bwd",
    )(proj, proj, proj, proj, o_f, lb, norm_w)


B_PAD = 16


def _conv_kernel(v_ref, g_ref, vp_ref, gp_ref, vn_ref, gn_ref, w_ref, b_ref, nw_ref, nb_ref, o_ref, ubuf):
    i = pl.program_id(0)
    ts = v_ref.shape[0]
    glu = lambda a, b: a * _sigmoid(b)
    ubuf[0:B_PAD, :] = jnp.where(i > 0, glu(vp_ref[...], gp_ref[...]), 0.0)
    ubuf[B_PAD:B_PAD + ts, :] = glu(v_ref[...], g_ref[...])
    ubuf[B_PAD + ts:, :] = jnp.where(i < pl.num_programs(0) - 1, glu(vn_ref[...], gn_ref[...]), 0.0)
    acc = jnp.zeros((ts, GROUP_W), F32) + b_ref[...]
    span = ts + 2 * B_PAD - SUBLANES
    shifted = [ubuf[pl.ds(r, span), :] for r in range(SUBLANES)]
    for j in range(B_KERNEL):
        off = B_PAD - B_KERNEL // 2 + j
        base = off - off % SUBLANES
        acc = acc + shifted[off % SUBLANES][base:base + ts] * w_ref[j:j + 1, :]
    o_ref[...] = _silu(_layer_norm(acc, nw_ref[...], nb_ref[...])).astype(o_ref.dtype)


def _conv_module(proj, conv_w, conv_b, norm_w, norm_b):
    s = proj.shape[0]
    ts = 512
    n = s // ts
    per = ts // B_PAD
    last = s // B_PAD - 1
    cur = lambda j: pl.BlockSpec((ts, GROUP_W), lambda i: (i, j))
    prev = lambda j: pl.BlockSpec((B_PAD, GROUP_W), lambda i: (jnp.maximum(i * per - 1, 0), j))
    nxt = lambda j: pl.BlockSpec((B_PAD, GROUP_W), lambda i: (jnp.minimum((i + 1) * per, last), j))
    vec = pl.BlockSpec((1, GROUP_W), lambda i: (0, 0))
    wpad = jnp.pad(conv_w, ((0, 1), (0, 0)))
    return pl.pallas_call(
        _conv_kernel,
        grid=(n,),
        in_specs=[cur(5), cur(6), prev(5), prev(6), nxt(5), nxt(6),
                  pl.BlockSpec((B_KERNEL + 1, GROUP_W), lambda i: (0, 0)), vec, vec, vec],
        out_specs=pl.BlockSpec((ts, GROUP_W), lambda i: (i, 0)),
        out_shape=jax.ShapeDtypeStruct((s, GROUP_W), MIX_DTYPE),
        scratch_shapes=[pltpu.VMEM((ts + 2 * B_PAD, GROUP_W), F32)],
        compiler_params=_cparams(1),
        name="conv_module",
    )(proj, proj, proj, proj, proj, proj, wpad, conv_b, norm_w, norm_b)


C_MAX_BLOCK = 128


def _dilated_block_sizes(span):
    return sorted({min(span // dil, C_MAX_BLOCK) for _, dil in C_PATTERNS})


def _dilated_kernel(q_ref, kp_ref, kc_ref, kn_ref, vp_ref, vc_ref, vn_ref, o_ref, qw, kw, vw, ob, mb, lb, *band_refs,
                    seq, span):
    i = pl.program_id(0)
    band = dict(zip(_dilated_block_sizes(span), band_refs))

    @pl.when(i == 0)
    def _():
        for nq, ref in band.items():
            a = lax.broadcasted_iota(jnp.int32, ref.shape, 0) & (nq - 1)
            b = lax.broadcasted_iota(jnp.int32, ref.shape, 1)
            ref[...] = jnp.where(jnp.abs(a - (b - C_SIDE)) <= C_SIDE, 0.0, NEG_INF)

    def put(dst, row0, val):
        n = val.shape[0]
        dst[0, row0:row0 + n, :] = val[:, :LANES]
        dst[1, row0:row0 + n, :] = val[:, LANES:]

    put(qw, 0, q_ref[...].astype(F32) * (HEAD_DIM ** -0.5))
    for dst, refs in ((kw, (kp_ref, kc_ref, kn_ref)), (vw, (vp_ref, vc_ref, vn_ref))):
        put(dst, 0, refs[0][...].astype(F32))
        put(dst, C_HALO, refs[1][...].astype(F32))
        put(dst, C_HALO + span, refs[2][...].astype(F32))

    head = lax.broadcasted_iota(jnp.int32, (1, GROUP_W), 1) >> 6
    heads = [head == h for h in range(4)]

    def rows(start, size, dil):
        return pl.ds(start, size) if dil == 1 else pl.ds(start, size, stride=dil)

    def load(src, sel):
        return jnp.concatenate([src[0, sel, :], src[1, sel, :]], axis=1)

    def store(dst, sel, val):
        dst[0, sel, :] = val[:, :LANES]
        dst[1, sel, :] = val[:, LANES:]

    def by_head(x, nq):
        out = jnp.where(heads[0], x[0:nq], 0.0)
        for h in range(1, 4):
            out = out + jnp.where(heads[h], x[h * nq:(h + 1) * nq], 0.0)
        return out

    def cols_by_head(col, nq):
        out = jnp.broadcast_to(col[3 * nq:4 * nq], (nq, GROUP_W))
        for h in range(2, -1, -1):
            out = jnp.where(heads[h], col[h * nq:(h + 1) * nq], out)
        return out

    def attend(first, r, dil, blk, nq):
        nk = nq + 2 * C_SIDE
        qrows = rows(r + dil * blk * nq, nq, dil)
        koff = C_HALO + r + dil * (blk * nq - C_SIDE)
        qs = load(qw, qrows)
        ks = load(kw, rows(koff, nk, dil)).astype(BF16)
        vs = load(vw, rows(koff, nk, dil)).astype(BF16)
        lhs = jnp.concatenate([jnp.where(hm, qs, 0.0) for hm in heads], axis=0).astype(BF16)
        kpos = i * span + (koff - C_HALO) + dil * lax.broadcasted_iota(jnp.int32, (1, nk), 1)
        edge = jnp.where((kpos >= 0) & (kpos < seq), 0.0, NEG_INF)
        sc = _dot_nt(lhs, ks) + band[nq][...] + edge
        m = jnp.max(sc, axis=1, keepdims=True)
        p = jnp.exp(sc - m)
        l = jnp.sum(p, axis=1, keepdims=True)
        o_new = by_head(_dot(p.astype(BF16), vs), nq)
        m_new = cols_by_head(m, nq)
        l_new = cols_by_head(l, nq)
        if first:
            store(ob, qrows, o_new)
            store(mb, qrows, m_new)
            store(lb, qrows, l_new)
        else:
            m_old = load(mb, qrows)
            m_tot = jnp.maximum(m_old, m_new)
            w_old, w_new = jnp.exp(m_old - m_tot), jnp.exp(m_new - m_tot)
            store(ob, qrows, w_old * load(ob, qrows) + w_new * o_new)
            store(lb, qrows, w_old * load(lb, qrows) + w_new * l_new)
            store(mb, qrows, m_tot)

    for pat, (_, dil) in enumerate(C_PATTERNS):
        sub = span // dil
        nq = min(sub, C_MAX_BLOCK)
        for r in range(dil):
            for blk in range(sub // nq):
                attend(pat == 0, r, dil, blk, nq)
    full = slice(0, span)
    o_ref[...] = (load(ob, full) / load(lb, full)).astype(o_ref.dtype)


def _dilated_attention(proj):
    s = proj.shape[0]
    span = 1024
    n = s // span
    per = span // C_HALO
    last = s // C_HALO - 1
    cur = lambda j: pl.BlockSpec((span, GROUP_W), lambda i: (i, j))
    prev = lambda j: pl.BlockSpec((C_HALO, GROUP_W), lambda i: (jnp.maximum(i * per - 1, 0), j))
    nxt = lambda j: pl.BlockSpec((C_HALO, GROUP_W), lambda i: (jnp.minimum((i + 1) * per, last), j))
    win = span + 2 * C_HALO
    return pl.pallas_call(
        functools.partial(_dilated_kernel, seq=s, span=span),
        grid=(n,),
        in_specs=[cur(0), prev(1), cur(1), nxt(1), prev(2), cur(2), nxt(2)],
        out_specs=pl.BlockSpec((span, GROUP_W), lambda i: (i, 0)),
        out_shape=jax.ShapeDtypeStruct((s, GROUP_W), MIX_DTYPE),
        scratch_shapes=[pltpu.VMEM((2, span, LANES), F32),
                        pltpu.VMEM((2, win, LANES), F32), pltpu.VMEM((2, win, LANES), F32),
                        pltpu.VMEM((2, span, LANES), F32), pltpu.VMEM((2, span, LANES), F32),
                        pltpu.VMEM((2, span, LANES), F32)]
        + [pltpu.VMEM((N_HEADS * nq, nq + 2 * C_SIDE), F32) for nq in _dilated_block_sizes(span)],
        compiler_params=_cparams(1),
        name="dilated_attention",
    )(proj, proj, proj, proj, proj, proj, proj)


def _window_kernel(sink_ref, q_ref, kp_ref, kc_ref, kn_ref, vp_ref, vc_ref, vn_ref, o_ref, band_ref, *, seq):
    i = pl.program_id(0)
    tq = q_ref.shape[0]
    hw = D_HALF_WINDOW
    nk = tq + 2 * hw
    kwin = jnp.concatenate([kp_ref[...], kc_ref[...], kn_ref[...]], axis=0).astype(BF16)
    vwin = jnp.concatenate([vp_ref[...], vc_ref[...], vn_ref[...]], axis=0).astype(BF16)
    lo = lax.broadcasted_iota(jnp.int32, (1, LANES), 1) < HEAD_DIM
    q = q_ref[...].astype(F32) * (HEAD_DIM ** -0.5)
    qa, qb = q[:, :LANES], q[:, LANES:]
    swap = lambda x: pltpu.roll(x, HEAD_DIM, 1)
    lhs = (jnp.concatenate([jnp.where(lo, qa, 0.0), jnp.where(lo, swap(qa), 0.0)], axis=0),
           jnp.concatenate([jnp.where(lo, 0.0, swap(qb)), jnp.where(lo, 0.0, qb)], axis=0))
    @pl.when(i == 0)
    def _():
        a = lax.broadcasted_iota(jnp.int32, (2 * tq, nk), 0) & (tq - 1)
        b = lax.broadcasted_iota(jnp.int32, (2 * tq, nk), 1)
        band_ref[...] = jnp.where(jnp.abs(b - hw - a) <= hw, 0.0, NEG_INF)

    kpos = i * tq - hw + lax.broadcasted_iota(jnp.int32, (1, nk), 1)
    bias = band_ref[...] + jnp.where((kpos >= 0) & (kpos < seq), 0.0, NEG_INF)
    second = lax.broadcasted_iota(jnp.int32, (2 * tq, 1), 0) >= tq
    outs = []
    for g in range(2):
        sc = _dot_nt(lhs[g].astype(BF16), kwin) + bias
        m = jnp.max(sc, axis=1, keepdims=True)
        p = jnp.exp(sc - m)
        l = jnp.sum(p, axis=1, keepdims=True)
        o = _dot(p.astype(BF16), vwin)
        sk = jnp.where(second, sink_ref[2 * g + 1], sink_ref[2 * g])
        m_tot = jnp.maximum(m, sk)
        w = jnp.exp(m - m_tot)
        den = l * w + jnp.exp(sk - m_tot)
        outs.append(o * (w / den))
    o_ref[:, :LANES] = jnp.where(lo, outs[0][:tq], swap(outs[0][tq:])).astype(o_ref.dtype)
    o_ref[:, LANES:] = jnp.where(lo, swap(outs[1][:tq]), outs[1][tq:]).astype(o_ref.dtype)


def _window_attention(proj, sink):
    s = proj.shape[0]
    tq = 256
    n = s // tq
    hw = D_HALF_WINDOW
    per = tq // hw
    last = s // hw - 1
    kcol, vcol = (2816 - ATTN_COL0) // LANES, (2944 - ATTN_COL0) // LANES
    cur = lambda j: pl.BlockSpec((tq, LANES), lambda i: (i, j))
    prev = lambda j: pl.BlockSpec((hw, LANES), lambda i: (jnp.maximum(i * per - 1, 0), j))
    nxt = lambda j: pl.BlockSpec((hw, LANES), lambda i: (jnp.minimum((i + 1) * per, last), j))
    return pl.pallas_call(
        functools.partial(_window_kernel, seq=s),
        grid=(n,),
        in_specs=[pl.BlockSpec(memory_space=pltpu.SMEM),
                  pl.BlockSpec((tq, GROUP_W), lambda i: (i, (2560 - ATTN_COL0) // GROUP_W)),
                  prev(kcol), cur(kcol), nxt(kcol), prev(vcol), cur(vcol), nxt(vcol)],
        out_specs=pl.BlockSpec((tq, GROUP_W), lambda i: (i, 0)),
        out_shape=jax.ShapeDtypeStruct((s, GROUP_W), MIX_DTYPE),
        scratch_shapes=[pltpu.VMEM((2 * tq, tq + 2 * hw), F32)],
        compiler_params=_cparams(1),
        name="window_attention",
    )(sink, proj, proj, proj, proj, proj, proj, proj)


def _outproj_kernel(ya_ref, yb_ref, yc_ref, yd_ref, w_ref, x_ref, g_ref, lw_ref, lb_ref, o_ref):
    y = None
    for k, ref in enumerate((ya_ref, yb_ref, yc_ref, yd_ref)):
        part = _dot(ref[...].astype(BF16), w_ref[k * GROUP_W:(k + 1) * GROUP_W, :])
        y = part if y is None else y + part
    r = DEEPNORM_ALPHA * x_ref[...] + (1.0 + g_ref[...]) * y
    o_ref[...] = _layer_norm(r, lw_ref[...], lb_ref[...])


def _outproj(ys, w_bf16, x, gate, ln_w, ln_b):
    s, d = x.shape
    tm = 512
    row = lambda i: (i, 0)
    const = lambda i: (0, 0)
    vec = pl.BlockSpec((1, d), const)
    grp = pl.BlockSpec((tm, GROUP_W), row)
    return pl.pallas_call(
        _outproj_kernel,
        grid=(s // tm,),
        in_specs=[grp, grp, grp, grp, pl.BlockSpec((4 * GROUP_W, d), const), pl.BlockSpec((tm, d), row),
                  vec, vec, vec],
        out_specs=pl.BlockSpec((tm, d), row),
        out_shape=jax.ShapeDtypeStruct((s, d), F32),
        compiler_params=_cparams(1),
        name="outproj_ln",
    )(*ys, w_bf16, x, gate, ln_w, ln_b)


FFN_CHUNK = 256


def _ffn_kernel(x_ref, sc_ref, sh_ref, wu_ref, wd_ref, g_ref, lw_ref, lb_ref, o_ref):
    x = x_ref[...]
    h = (x * (1.0 + sc_ref[...]) + sh_ref[...]).astype(BF16)
    acc = jnp.zeros(x.shape, F32)
    for c0 in range(0, FFN_DIM, FFN_CHUNK):
        gate = _dot(h, wu_ref[:, c0:c0 + FFN_CHUNK])
        up = _dot(h, wu_ref[:, FFN_DIM + c0:FFN_DIM + c0 + FFN_CHUNK])
        acc = acc + _dot((_silu(gate) * up).astype(BF16), wd_ref[c0:c0 + FFN_CHUNK, :])
    r = DEEPNORM_ALPHA * x + (1.0 + g_ref[...]) * acc
    o_ref[...] = _layer_norm(r, lw_ref[...], lb_ref[...])


def _dense_ffn(x, scale, shift, wu_bf16, wd_bf16, gate, ln_w, ln_b):
    s, d = x.shape
    tm = 512
    row = lambda i: (i, 0)
    const = lambda i: (0, 0)
    vec = pl.BlockSpec((1, d), const)
    return pl.pallas_call(
        _ffn_kernel,
        grid=(s // tm,),
        in_specs=[pl.BlockSpec((tm, d), row), vec, vec,
                  pl.BlockSpec((d, 2 * FFN_DIM), const), pl.BlockSpec((FFN_DIM, d), const), vec, vec, vec],
        out_specs=pl.BlockSpec((tm, d), row),
        out_shape=jax.ShapeDtypeStruct((s, d), F32),
        compiler_params=_cparams(1),
        name="dense_ffn_ln",
    )(x, scale, shift, wu_bf16, wd_bf16, gate, ln_w, ln_b)


ROW_TILE = 8


def _to_token_tiles(ref, val):
    n = val.shape[0]
    for c in range(ROW_TILE):
        ref[pl.ds(c, n, stride=ROW_TILE), :] = val[:, c * LANES:(c + 1) * LANES]


def _from_token_tiles(ref, n):
    return jnp.concatenate([ref[pl.ds(c, n, stride=ROW_TILE), :] for c in range(ROW_TILE)], axis=1)


def _router_kernel(x_ref, sc_ref, sh_ref, w_ref, h_ref, e_ref, p_ref, c_ref, cnt_ref):
    h = x_ref[...] * (1.0 + sc_ref[...]) + sh_ref[...]
    _to_token_tiles(h_ref, h)
    h_hi, h_lo = _split_bf16(h)
    w_hi, w_lo = _split_bf16(w_ref[...])
    logits = _dot(h_hi, w_hi) + _dot(h_hi, w_lo) + _dot(h_lo, w_hi)
    lane = lax.broadcasted_iota(jnp.int32, logits.shape, 1)
    logits = jnp.where(lane < N_EXPERTS, logits, NEG_INF)
    m1 = jnp.max(logits, axis=1, keepdims=True)
    e1 = jnp.min(jnp.where(logits == m1, lane, LANES), axis=1, keepdims=True)
    rest = jnp.where(lane == e1, NEG_INF, logits)
    m2 = jnp.max(rest, axis=1, keepdims=True)
    e2 = jnp.min(jnp.where(rest == m2, lane, LANES), axis=1, keepdims=True)
    t = jnp.exp(m2 - m1)
    g1 = 1.0 / (1.0 + t)
    p_ref[...] = jnp.where(lane == 0, g1, jnp.where(lane == 1, t * g1, 0.0))

    @pl.when(pl.program_id(0) == 0)
    def _():
        cnt_ref[...] = jnp.zeros_like(cnt_ref)

    tm = logits.shape[0]
    picks = jnp.where((lane == e1) | (lane == e2), 1.0, 0.0)
    rr = lax.broadcasted_iota(jnp.int32, (tm, tm), 0)
    cc = lax.broadcasted_iota(jnp.int32, (tm, tm), 1)
    before = _dot(jnp.where(cc < rr, 1.0, 0.0).astype(BF16), picks.astype(BF16)) + cnt_ref[...]
    r1 = jnp.sum(jnp.where(lane == e1, before, 0.0), axis=1, keepdims=True).astype(jnp.int32)
    r2 = jnp.sum(jnp.where(lane == e2, before, 0.0), axis=1, keepdims=True).astype(jnp.int32)
    cnt_ref[...] += jnp.sum(picks, axis=0, keepdims=True)
    c_ref[...] = jnp.broadcast_to(cnt_ref[...], c_ref.shape)
    e_ref[...] = jnp.where(lane == 0, e1, jnp.where(lane == 1, e2, jnp.where(lane == 2, r1, jnp.where(lane == 3, r2, 0))))


def _router(x, scale, shift, w_router):
    s, d = x.shape
    tm = 512
    row = lambda i: (i, 0)
    const = lambda i: (0, 0)
    vec = pl.BlockSpec((1, d), const)
    wpad = jnp.pad(w_router, ((0, 0), (0, LANES - N_EXPERTS)))
    return pl.pallas_call(
        _router_kernel,
        grid=(s // tm,),
        in_specs=[pl.BlockSpec((tm, d), row), vec, vec, pl.BlockSpec((d, LANES), const)],
        out_specs=[pl.BlockSpec((tm * ROW_TILE, LANES), row), pl.BlockSpec((tm, LANES), row),
                   pl.BlockSpec((tm, LANES), row), pl.BlockSpec((8, LANES), const)],
        out_shape=[jax.ShapeDtypeStruct((s * ROW_TILE, LANES), F32), jax.ShapeDtypeStruct((s, LANES), jnp.int32),
                   jax.ShapeDtypeStruct((s, LANES), F32), jax.ShapeDtypeStruct((8, LANES), F32)],
        scratch_shapes=[pltpu.VMEM((1, LANES), F32)],
        compiler_params=_cparams(1),
        name="moe_router",
    )(x, scale, shift, wpad)


MOE_TILE = 512
MOE_CHUNK = 1792
ROW_UNROLL = 16
TILE_ROWS = MOE_TILE * ROW_TILE


def _expert_kernel(te_ref, nu_ref, tokc_ref, tokn_ref, dst_ref, h_hbm, wg_ref, wu_ref, wd_ref, y_hbm,
                   xbuf, xb16, acc, ybuf, gsem, ssem):
    i, f = pl.program_id(0), pl.program_id(1)
    nt, nf = pl.num_programs(0), pl.num_programs(1)
    n_used = nu_ref[0]
    slot = i & 1
    tail = y_hbm.shape[0] - TILE_ROWS

    def rows_loop(fn):
        def body(g, carry):
            base = pl.multiple_of(g * ROW_UNROLL, ROW_UNROLL)
            for j in range(ROW_UNROLL):
                fn(base + j, j % 2)
            return carry
        lax.fori_loop(0, MOE_TILE // ROW_UNROLL, body, 0)

    def token(ref, r):
        return ref.at[pl.ds(pl.multiple_of(r, ROW_TILE), ROW_TILE)]

    def gather_rows(ids_ref, s):
        rows_loop(lambda r, pri: pltpu.make_async_copy(token(h_hbm, ids_ref[0, r]), token(xbuf.at[s], r * ROW_TILE),
                                                       gsem.at[s]).start(priority=pri))

    def gather_wait(s):
        pltpu.make_async_copy(h_hbm.at[pl.ds(0, TILE_ROWS)], xbuf.at[s], gsem.at[s]).wait()

    def scatter_rows(s):
        rows_loop(lambda r, pri: pltpu.make_async_copy(token(ybuf.at[s], r * ROW_TILE), token(y_hbm, dst_ref[0, r]),
                                                       ssem.at[s]).start(priority=pri))

    def scatter_wait(s):
        pltpu.make_async_copy(ybuf.at[s], y_hbm.at[pl.ds(0, TILE_ROWS)], ssem.at[s]).wait()

    @pl.when(f == 0)
    def _():
        @pl.when(i == 0)
        def _():
            ybuf[1] = jnp.zeros(ybuf.shape[1:], F32)
            zero_tail = pltpu.make_async_copy(ybuf.at[1], y_hbm.at[pl.ds(tail, TILE_ROWS)], ssem.at[1])
            zero_tail.start()
            zero_tail.wait()
            gather_rows(tokc_ref, 0)

        @pl.when(i + 1 < n_used)
        def _():
            gather_rows(tokn_ref, 1 - slot)

        @pl.when(i < n_used)
        def _():
            gather_wait(slot)
            xb16[...] = _from_token_tiles(xbuf.at[slot], MOE_TILE).astype(BF16)

    @pl.when(i < n_used)
    def _():
        x = xb16[...]
        act = _silu(_dot(x, wg_ref[...])) * _dot(x, wu_ref[...])
        part = _dot(act.astype(BF16), wd_ref[...])

        @pl.when(f == 0)
        def _():
            acc[...] = part

        @pl.when((f > 0) & (f < nf - 1))
        def _():
            acc[...] += part

        @pl.when(f == nf - 1)
        def _():
            _to_token_tiles(ybuf.at[slot], acc[...] + part)

    @pl.when(f == nf - 1)
    def _():
        @pl.when((i >= 1) & (i - 1 < n_used))
        def _():
            scatter_wait(1 - slot)

        @pl.when(i < n_used)
        def _():
            scatter_rows(slot)

        @pl.when((i == nt - 1) & (i < n_used))
        def _():
            scatter_wait(slot)


def _experts(h, slot_tok, slot_dst, tile_expert, n_used, wu_bf16, wd_bf16, n_rows_out):
    d = D_MODEL
    n_tiles = slot_tok.shape[0] // MOE_TILE
    nf = EXPERT_DIM // MOE_CHUNK
    assert nf >= 2
    fidx = lambda i, f, nu: jnp.where(i < nu[0], f, nf - 1)
    ids = lambda fn: pl.BlockSpec((None, 1, MOE_TILE), lambda i, f, te, nu: (fn(i), 0, 0), memory_space=pltpu.SMEM)
    grid_spec = pltpu.PrefetchScalarGridSpec(
        num_scalar_prefetch=2,
        grid=(n_tiles, nf),
        in_specs=[ids(lambda i: i), ids(lambda i: jnp.minimum(i + 1, n_tiles - 1)), ids(lambda i: i),
                  pl.BlockSpec(memory_space=pl.ANY),
                  pl.BlockSpec((None, d, MOE_CHUNK), lambda i, f, te, nu: (te[i], 0, fidx(i, f, nu))),
                  pl.BlockSpec((None, d, MOE_CHUNK), lambda i, f, te, nu: (te[i], 0, nf + fidx(i, f, nu))),
                  pl.BlockSpec((None, MOE_CHUNK, d), lambda i, f, te, nu: (te[i], fidx(i, f, nu), 0))],
        out_specs=pl.BlockSpec(memory_space=pl.ANY),
        scratch_shapes=[pltpu.VMEM((2, TILE_ROWS, LANES), F32), pltpu.VMEM((MOE_TILE, d), BF16),
                        pltpu.VMEM((MOE_TILE, d), F32), pltpu.VMEM((2, TILE_ROWS, LANES), F32),
                        pltpu.SemaphoreType.DMA((2,)), pltpu.SemaphoreType.DMA((2,))],
    )
    tok3 = (slot_tok * ROW_TILE).reshape(n_tiles, 1, MOE_TILE)
    dst3 = (slot_dst * ROW_TILE).reshape(n_tiles, 1, MOE_TILE)
    return pl.pallas_call(
        _expert_kernel,
        grid_spec=grid_spec,
        out_shape=jax.ShapeDtypeStruct((n_rows_out * ROW_TILE, LANES), F32),
        compiler_params=_cparams(2),
        name="moe_experts",
    )(tile_expert, n_used, tok3, tok3, dst3, h, wu_bf16, wu_bf16, wd_bf16)


def _combine_kernel(x_ref, y1_ref, y2_ref, p_ref, g_ref, lw_ref, lb_ref, o_ref):
    p = p_ref[...]
    n = x_ref.shape[0]
    f = p[:, 0:1] * _from_token_tiles(y1_ref, n) + p[:, 1:2] * _from_token_tiles(y2_ref, n)
    r = DEEPNORM_ALPHA * x_ref[...] + (1.0 + g_ref[...]) * f
    o_ref[...] = _layer_norm(r, lw_ref[...], lb_ref[...])


def _combine(x, y, probs, gate, ln_w, ln_b):
    s, d = x.shape
    tm = 512
    row = lambda i: (i, 0)
    vec = pl.BlockSpec((1, d), lambda i: (0, 0))
    big = pl.BlockSpec((tm, d), row)
    return pl.pallas_call(
        _combine_kernel,
        grid=(s // tm,),
        in_specs=[big, pl.BlockSpec((tm * ROW_TILE, LANES), row),
                  pl.BlockSpec((tm * ROW_TILE, LANES), lambda i: (s // tm + i, 0)),
                  pl.BlockSpec((tm, LANES), row), vec, vec, vec],
        out_specs=big,
        out_shape=jax.ShapeDtypeStruct((s, d), F32),
        compiler_params=_cparams(1),
        name="moe_combine_ln",
    )(x, y, y, probs, gate, ln_w, ln_b)


def _moe(x, scale, shift, w_router, wu_bf16, wd_bf16, gate, ln_w, ln_b):
    t, d = x.shape
    h, sel, probs, cnt = _router(x, scale, shift, w_router)
    flat_e = sel[:, :TOP_K].reshape(-1)
    rank = sel[:, TOP_K:2 * TOP_K].reshape(-1)
    counts = cnt[0, :N_EXPERTS].astype(jnp.int32)
    n_assign = t * TOP_K
    n_tiles = -(-n_assign // MOE_TILE) + N_EXPERTS
    n_slots = n_tiles * MOE_TILE
    padded = (counts + MOE_TILE - 1) // MOE_TILE * MOE_TILE
    padded_end = jnp.cumsum(padded)
    slot = (padded_end - padded)[flat_e] + rank
    assign = jnp.full((n_slots,), -1, jnp.int32).at[slot].set(jnp.arange(n_assign, dtype=jnp.int32))
    used = assign >= 0
    slot_tok = jnp.where(used, assign >> 1, 0)
    slot_dst = jnp.where(used, (assign & 1) * t + (assign >> 1),
                         n_assign + (jnp.arange(n_slots, dtype=jnp.int32) & (MOE_TILE - 1)))
    n_used = (padded_end[-1] // MOE_TILE).astype(jnp.int32).reshape(1)
    tile_ids = jnp.minimum(jnp.arange(n_tiles, dtype=jnp.int32), n_used[0] - 1)
    tile_expert = jnp.minimum(jnp.sum((tile_ids * MOE_TILE)[:, None] >= padded_end[None, :], axis=1),
                              N_EXPERTS - 1).astype(jnp.int32)
    y = _experts(h, slot_tok, slot_dst, tile_expert, n_used, wu_bf16, wd_bf16, n_assign + MOE_TILE)
    return _combine(x, y, probs, gate, ln_w, ln_b)


def kernel(x, c, positions, w_ada, b_ada, w_in, w_out, a_lower_bound, a_norm_w, b_conv_w, b_conv_b, b_norm_w,
           b_norm_b, d_sink, ln_w, ln_b, ffn_w_up, ffn_w_down, moe_router, moe_w_up, moe_w_down):
    batch, s, d = x.shape
    assert batch == 1 and d == D_MODEL
    x = x.reshape(s, d)
    mod = _adaln(c, w_ada, b_ada)
    rope = _rope_tables(positions)
    lb_cum = jnp.cumsum(jax.nn.softmax(a_lower_bound.astype(F32), axis=0), axis=0)
    lb_all = lb_cum - lb_cum[0]
    row = lambda v: v.reshape(1, -1)
    for layer in range(DEPTH):
        shift1, scale1, gate1, shift2, scale2, gate2 = [mod[layer, :, k * d:(k + 1) * d] for k in range(6)]
        proj, attn = _inproj(x, scale1, shift1, w_in[layer].astype(BF16), rope)
        ya = _hgrn2(proj, row(lb_all[layer]), row(jnp.tile(a_norm_w[layer], GROUP_W // HEAD_DIM)))
        yb = _conv_module(proj, b_conv_w[layer], row(b_conv_b[layer]), row(b_norm_w[layer]), row(b_norm_b[layer]))
        yc = _dilated_attention(attn)
        yd = _window_attention(attn, d_sink[layer])
        x = _outproj((ya, yb, yc, yd), w_out[layer].astype(BF16), x, gate1, row(ln_w[layer, 0]), row(ln_b[layer, 0]))
        lw, lbias = row(ln_w[layer, 1]), row(ln_b[layer, 1])
        if layer % 2 == 0:
            x = _dense_ffn(x, scale2, shift2, ffn_w_up[layer // 2].astype(BF16),
                           ffn_w_down[layer // 2].astype(BF16), gate2, lw, lbias)
        else:
            x = _moe(x, scale2, shift2, moe_router[layer // 2], moe_w_up[layer // 2].astype(BF16),
                     moe_w_down[layer // 2].astype(BF16), gate2, lw, lbias)
    return x.reshape(batch, s, d)
```
